```python
import math
import jax, jax.numpy as jnp
from jax import lax
import numpy as np

D_MODEL = 1024
BATCH = 8
SEQ = 4096
DEPTH = 1

N_META = 16
HEAD_DIM = 64
ATT_Q_HEADS = D_MODEL // HEAD_DIM
ATT_KV_HEADS = max(1, ATT_Q_HEADS // 8)
ATT_GROUP = ATT_Q_HEADS // ATT_KV_HEADS
WINDOW = 128
BLOCK = 128
N_BUCKETS = 32
MAX_EXACT = N_BUCKETS // 2
MAX_DISTANCE = 128
RWKV_HEADS = D_MODEL // HEAD_DIM
RWKV_WIDTH = RWKV_HEADS * HEAD_DIM
DECAY_LORA = 64
ICLR_LORA = 64
GATE_LORA = 128
D_FF = 4 * D_MODEL
LN_EPS = 1e-5
GN_EPS = 1e-5 * HEAD_DIM
ALPHA = (2.0 * DEPTH) ** 0.25
BETA = (8.0 * DEPTH) ** -0.25
ATT_Q_COLS = ATT_Q_HEADS * HEAD_DIM
ATT_KV_COLS = ATT_KV_HEADS * HEAD_DIM
RWKV_COLS = 3 * RWKV_WIDTH + DECAY_LORA + ICLR_LORA + GATE_LORA
GATE_COLS = 2 * D_MODEL
IN_COLS = ATT_Q_COLS + 2 * ATT_KV_COLS + RWKV_COLS + GATE_COLS

kernel_name = 'hybrid_swa_sink_rwkv7_gated_deepnorm'


def layer_norm(x, g, b):
    xf = x.astype(jnp.float32)
    mu = jnp.mean(xf, -1, keepdims=True)
    var = jnp.mean(jnp.square(xf - mu), -1, keepdims=True)
    y = (xf - mu) * lax.rsqrt(var + LN_EPS)
    return (y * g.astype(jnp.float32) + b.astype(jnp.float32)).astype(x.dtype)


def t5_bucket(dist):
    small = dist < MAX_EXACT
    d = jnp.maximum(dist, 1).astype(jnp.float32)
    large = MAX_EXACT + (jnp.log(d / MAX_EXACT) / math.log(MAX_DISTANCE / MAX_EXACT)
                         * (N_BUCKETS - MAX_EXACT)).astype(jnp.int32)
    large = jnp.minimum(large, N_BUCKETS - 1)
    return jnp.where(small, dist, large)


def swa_one(q, k, v, rel_bias, sinks):
    T = q.shape[0]
    pad = (-T) % BLOCK
    nb = (T + pad) // BLOCK
    qb = jnp.pad(q, ((pad, 0), (0, 0), (0, 0), (0, 0))).reshape(nb, BLOCK, ATT_KV_HEADS, ATT_GROUP, HEAD_DIM)

    def windows(z):
        zb = jnp.pad(z, ((pad + BLOCK, 0), (0, 0), (0, 0))).reshape(nb + 1, BLOCK, ATT_KV_HEADS, HEAD_DIM)
        return jnp.concatenate([zb[:-1], zb[1:]], axis=1)

    kw, vw = windows(k), windows(v)
    logits = jnp.einsum('nqkgd,nskd->kgnqs', qb, kw).astype(jnp.float32) * (HEAD_DIM ** -0.5)
    dist = jnp.arange(BLOCK)[:, None] + BLOCK - jnp.arange(2 * BLOCK)[None, :]
    in_window = (dist >= 0) & (dist < WINDOW)
    bias = rel_bias[t5_bucket(jnp.maximum(dist, 0))].astype(jnp.float32)
    bias = jnp.transpose(bias, (2, 0, 1)).reshape(ATT_KV_HEADS, ATT_GROUP, 1, BLOCK, 2 * BLOCK)
    key_idx = jnp.arange(nb)[:, None] * BLOCK + jnp.arange(2 * BLOCK)[None, :]
    key_valid = key_idx >= pad + BLOCK
    mask = in_window[None] & key_valid[:, None, :]
    logits = jnp.where(mask, logits + bias, -jnp.inf)
    sink = sinks.astype(jnp.float32).reshape(ATT_KV_HEADS, ATT_GROUP, 1, 1, 1)
    m = jnp.maximum(jnp.max(logits, -1, keepdims=True), sink)
    p = jnp.exp(logits - m)
    denom = jnp.sum(p, -1, keepdims=True) + jnp.exp(sink - m)
    probs = (p / denom).astype(v.dtype)
    out = jnp.einsum('kgnqs,nskd->nqkgd', probs, vw)
    return out.reshape(nb * BLOCK, ATT_Q_COLS)[pad:]


def token_shift(z, mu):
    prev = jnp.pad(z, ((0, 0), (1, 0), (0, 0)))[:, :-1]
    return z + (prev - z) * mu


def rwkv7_time_mix(z, mu, w0, w2, a0, a2, g2, k_k, k_a, r_k, lnx_g, lnx_b):
    f32 = jnp.float32
    B, T = z.shape[0], z.shape[1]
    H, N = RWKV_HEADS, HEAD_DIM
    z = token_shift(z, mu)
    W = RWKV_WIDTH
    r, k, v, wl, al, gl = jnp.split(z, [W, 2 * W, 3 * W, 3 * W + DECAY_LORA, 3 * W + DECAY_LORA + ICLR_LORA], axis=-1)
    w = -jax.nn.softplus(-(w0 + jnp.tanh(wl) @ w2)) - 0.5
    decay = jnp.exp(-jnp.exp(w.astype(f32)))
    a = jax.nn.sigmoid(a0 + al @ a2)
    g = jax.nn.sigmoid(gl) @ g2
    heads = lambda t: t.astype(f32).reshape(B, T, H, N)
    rh, kh, vh, wh, ah = heads(r), heads(k), heads(v), heads(decay), heads(a)
    kk = kh * k_k.astype(f32).reshape(H, N)
    kk = kk / jnp.maximum(jnp.sqrt(jnp.sum(kk * kk, -1, keepdims=True)), 1e-12)
    kh = kh * (1.0 + (ah - 1.0) * k_a.astype(f32).reshape(H, N))

    def step(S, inp):
        r_t, w_t, k_t, v_t, kk_t, a_t = inp
        sa = jnp.einsum('bhij,bhj->bhi', S, -kk_t)
        S = S * w_t[:, :, None, :] + sa[..., None] * (kk_t * a_t)[:, :, None, :] + v_t[..., None] * k_t[:, :, None, :]
        return S, jnp.einsum('bhij,bhj->bhi', S, r_t)

    tm = lambda t: jnp.swapaxes(t, 0, 1)
    S0 = jnp.zeros((B, H, N, N), f32)
    _, y = lax.scan(step, S0, (tm(rh), tm(wh), tm(kh), tm(vh), tm(kk), tm(ah)))
    y = tm(y)
    ym = jnp.mean(y, -1, keepdims=True)
    yv = jnp.mean(jnp.square(y - ym), -1, keepdims=True)
    y = ((y - ym) * lax.rsqrt(yv + GN_EPS)).reshape(B, T, W) * lnx_g.astype(f32) + lnx_b.astype(f32)
    bonus = jnp.sum(rh * kh * r_k.astype(f32), -1, keepdims=True) * vh
    y = (y + bonus.reshape(B, T, W)) * g.astype(f32)
    return y.astype(z.dtype)


def setup_inputs(seed: int = 0) -> dict:
    key = jax.random.key(seed)
    ks = jax.random.split(key, 26)
    f32 = jnp.float32
    nrm = lambda k, shape, s: s * jax.random.normal(k, shape, f32)
    L = DEPTH
    decay_base = jnp.tile(jnp.linspace(-6.0, -1.0, HEAD_DIM), RWKV_HEADS)
    return {
        'x': nrm(ks[0], (BATCH, SEQ, D_MODEL), 1.0),
        'meta_tokens': nrm(ks[1], (N_META, D_MODEL), 1.0),
        'ln0_g': 1.0 + nrm(ks[2], (D_MODEL,), 0.02),
        'ln0_b': nrm(ks[3], (D_MODEL,), 0.02),
        'rel_bias': nrm(ks[4], (N_BUCKETS, ATT_Q_HEADS), 0.5),
        'w_in': nrm(ks[5], (L, D_MODEL, IN_COLS), D_MODEL ** -0.5),
        'shift_mu': jax.random.uniform(ks[6], (L, RWKV_COLS), f32),
        'attn_sinks': nrm(ks[7], (L, ATT_Q_HEADS), 0.5),
        'decay_w0': decay_base + nrm(ks[8], (L, RWKV_WIDTH), 0.1),
        'decay_w2': nrm(ks[9], (L, DECAY_LORA, RWKV_WIDTH), 0.5 * DECAY_LORA ** -0.5),
        'iclr_a0': nrm(ks[10], (L, RWKV_WIDTH), 0.3),
        'iclr_a2': nrm(ks[11], (L, ICLR_LORA, RWKV_WIDTH), ICLR_LORA ** -0.5),
        'gate_w2': nrm(ks[12], (L, GATE_LORA, RWKV_WIDTH), GATE_LORA ** -0.5),
        'k_k': 0.85 + nrm(ks[13], (L, RWKV_WIDTH), 0.05),
        'k_a': 1.0 + nrm(ks[14], (L, RWKV_WIDTH), 0.05),
        'r_k': nrm(ks[15], (L, RWKV_HEADS, HEAD_DIM), 0.1),
        'lnx_g': 1.0 + nrm(ks[16], (L, RWKV_WIDTH), 0.02),
        'lnx_b': nrm(ks[17], (L, RWKV_WIDTH), 0.02),
        'w_out': nrm(ks[18], (L, D_MODEL, D_MODEL), BETA * D_MODEL ** -0.5),
        'ln1_g': 1.0 + nrm(ks[19], (L, D_MODEL), 0.02),
        'ln1_b': nrm(ks[20], (L, D_MODEL), 0.02),
        'w_ff1': nrm(ks[21], (L, D_MODEL, D_FF), D_MODEL ** -0.5),
        'w_ff2': nrm(ks[22], (L, D_FF, D_MODEL), BETA * D_FF ** -0.5),
        'ln2_g': 1.0 + nrm(ks[23], (L, D_MODEL), 0.02),
        'ln2_b': nrm(ks[24], (L, D_MODEL), 0.02),
    }


def reference(x, meta_tokens, ln0_g, ln0_b, rel_bias, w_in, shift_mu, attn_sinks, decay_w0, decay_w2,
              iclr_a0, iclr_a2, gate_w2, k_k, k_a, r_k, lnx_g, lnx_b, w_out, ln1_g, ln1_b,
              w_ff1, w_ff2, ln2_g, ln2_b):
    B = x.shape[0]
    meta = jnp.broadcast_to(meta_tokens[None].astype(x.dtype), (B, N_META, D_MODEL))
    h = layer_norm(jnp.concatenate([meta, x], axis=1), ln0_g, ln0_b)
    T = h.shape[1]
    c0 = ATT_Q_COLS
    c1 = c0 + ATT_KV_COLS
    c2 = c1 + ATT_KV_COLS
    c3 = c2 + RWKV_COLS
    c4 = c3 + D_MODEL
    for l in range(DEPTH):
        proj = h @ w_in[l]
        q, k, v, zr, gate_att, gate_rwkv = jnp.split(proj, [c0, c1, c2, c3, c4], axis=-1)
        q = q.reshape(B, T, ATT_KV_HEADS, ATT_GROUP, HEAD_DIM)
        k = k.reshape(B, T, ATT_KV_HEADS, HEAD_DIM)
        v = v.reshape(B, T, ATT_KV_HEADS, HEAD_DIM)
        sinks_l = attn_sinks[l]
        att = lax.map(lambda qkv: swa_one(qkv[0], qkv[1], qkv[2], rel_bias, sinks_l), (q, k, v))
        rw = rwkv7_time_mix(zr, shift_mu[l], decay_w0[l], decay_w2[l], iclr_a0[l], iclr_a2[l], gate_w2[l],
                            k_k[l], k_a[l], r_k[l], lnx_g[l], lnx_b[l])
        merged = jax.nn.sigmoid(gate_att) * att + jax.nn.sigmoid(gate_rwkv) * rw
        h = layer_norm(ALPHA * h + merged @ w_out[l], ln1_g[l], ln1_b[l])
        ff = jnp.square(jax.nn.relu(h @ w_ff1[l])) @ w_ff2[l]
        h = layer_norm(ALPHA * h + ff, ln2_g[l], ln2_b[l])
    return h[:, N_META:]
```

```python
import functools
import math

import numpy as np
import jax
import jax.numpy as jnp
from jax import lax
from jax.experimental import pallas as pl
from jax.experimental.pallas import tpu as pltpu

F32 = jnp.float32
BF16 = jnp.bfloat16

D_MODEL = 1024
N_META = 16
HEAD_DIM = 64
N_HEADS = D_MODEL // HEAD_DIM
N_KV = 2
BLOCK = 128
N_BUCKETS = 32
MAX_EXACT = 16
MAX_DISTANCE = 128
DECAY_LORA = 64
ICLR_LORA = 64
GATE_LORA = 128
D_FF = 4 * D_MODEL
LN_EPS = 1e-5
GN_EPS = 1e-5 * HEAD_DIM
DEPTH = 1
ALPHA = (2.0 * DEPTH) ** 0.25

LANES = 128
N_PAIRS = D_MODEL // LANES
CHUNK = 64
RWKV_ROWS = 256
PAD_ROWS = BLOCK - N_META

OFF_Q = 0
OFF_GA = 1024
OFF_KX = 2048
OFF_VX = 2304
OFF_R = 2560
OFF_K = 3584
OFF_V = 4608
OFF_L = 5632
OFF_GR = 5888
P_COLS = 6912

VMEM_LIMIT = 48 * 1024 * 1024

_NN = (((1,), (0,)), ((), ()))
_NT = (((1,), (1,)), ((), ()))


def _dg(a, b, dims=_NN):
    return lax.dot_general(a, b, dims, preferred_element_type=F32)


def _split(x):
    hi = x.astype(BF16)
    lo = (x - hi.astype(F32)).astype(BF16)
    return hi, lo


def _mm3(a, b, dims=_NN):
    ah, al = _split(a)
    bh, bl = _split(b)
    return _dg(ah, bh, dims) + (_dg(ah, bl, dims) + _dg(al, bh, dims))


def _mm1(a, b, dims=_NN):
    return _dg(a.astype(BF16), b.astype(BF16), dims)


def _mm_exact_rhs(a, b_bf16):
    ah, al = _split(a)
    return _dg(ah, b_bf16) + _dg(al, b_bf16)


def _layer_norm(x, g, b):
    mu = jnp.mean(x, axis=-1, keepdims=True)
    xc = x - mu
    var = jnp.mean(xc * xc, axis=-1, keepdims=True)
    return xc * lax.rsqrt(var + LN_EPS) * g + b


def _bias_kernel(bucket_ref, rel_ref, o_ref):
    for v in range(2):
        bk = bucket_ref[v]
        for h in range(N_HEADS):
            acc = jnp.where(bk < 0, -jnp.inf, 0.0).astype(F32)
            for b in range(N_BUCKETS):
                acc = jnp.where(bk == b, rel_ref[b, h], acc)
            o_ref[v, h] = acc


def _bucket_table():
    q = np.arange(BLOCK)[:, None]
    s = np.arange(2 * BLOCK)[None, :]
    dist = q + BLOCK - s
    in_window = (dist >= 0) & (dist < BLOCK)
    d0 = np.maximum(dist, 0)
    d = np.maximum(d0, 1).astype(np.float32)
    large = MAX_EXACT + (np.log(d / np.float32(MAX_EXACT)) / np.float32(math.log(MAX_DISTANCE / MAX_EXACT))
                         * (N_BUCKETS - MAX_EXACT)).astype(np.int32)
    large = np.minimum(large, N_BUCKETS - 1)
    bucket = np.where(d0 < MAX_EXACT, d0, large).astype(np.int32)
    general = np.where(in_window, bucket, -1)
    first = np.where(in_window & (s >= PAD_ROWS), bucket, -1)
    return np.stack([first, general]).astype(np.int32)


def _bias_table(rel_bias):
    return pl.pallas_call(
        _bias_kernel,
        out_shape=jax.ShapeDtypeStruct((2, N_HEADS, BLOCK, 2 * BLOCK), F32),
        in_specs=[pl.BlockSpec(memory_space=pltpu.VMEM), pl.BlockSpec(memory_space=pltpu.SMEM)],
        out_specs=pl.BlockSpec(memory_space=pltpu.VMEM),
        name="bias_table",
    )(jnp.asarray(_bucket_table()), rel_bias.astype(F32))


def _ln_proj_kernel(x_ref, g_ref, b_ref, w_ref, o_ref, *, n_pad, tn):
    y = _layer_norm(x_ref[...], g_ref[...], b_ref[...])
    if n_pad:
        row = lax.broadcasted_iota(jnp.int32, y.shape, 0)
        y = jnp.where(row < n_pad, 0.0, y)
    yb = y.astype(BF16)
    for j in range(P_COLS // tn):
        cs = slice(j * tn, (j + 1) * tn)
        o_ref[:, cs] = _dg(yb, w_ref[:, cs])


def _const_spec(shape):
    return pl.BlockSpec(shape, lambda *_: (0,) * len(shape), pipeline_mode=pl.Buffered(1))


def _ln_proj(x2, g, b, w, *, tm, tn=768, n_pad=0):
    m = x2.shape[0]
    return pl.pallas_call(
        functools.partial(_ln_proj_kernel, n_pad=n_pad, tn=tn),
        out_shape=jax.ShapeDtypeStruct((m, P_COLS), F32),
        grid=(m // tm,),
        in_specs=[
            pl.BlockSpec((tm, D_MODEL), lambda i: (i, 0)),
            _const_spec((1, D_MODEL)),
            _const_spec((1, D_MODEL)),
            _const_spec((D_MODEL, P_COLS)),
        ],
        out_specs=pl.BlockSpec((tm, P_COLS), lambda i: (i, 0)),
        compiler_params=pltpu.CompilerParams(
            dimension_semantics=("parallel",), vmem_limit_bytes=VMEM_LIMIT),
        name="ln_proj",
    )(x2, g, b, w)


(_P_MU_R, _P_MU_K, _P_MU_V, _P_MU_WA, _P_MU_G, _P_W0, _P_A0, _P_KK, _P_KA, _P_RK, _P_LNG, _P_LNB) = range(12)
_P_ROWS = 16


def _rwkv_kernel(r_ref, k_ref, v_ref, l_ref, gr_ref, pr_ref, pk_ref, pv_ref, plr_ref,
                 par_ref, wl_ref, g2_ref, s0_ref, o_ref, s_ref,
                 h_scr, zr_scr, zk_scr, zv_scr, zl_scr, *, rows):
    t = pl.program_id(2)

    @pl.when(t == 0)
    def _():
        h_scr[...] = s0_ref[0]
        zr_scr[...] = pr_ref[...]
        zk_scr[...] = pk_ref[...]
        zv_scr[...] = pv_ref[...]
        zl_scr[...] = plr_ref[...]

    par = par_ref[0]
    prow = lambda i: par[i:i + 1, :]
    row_id = lax.broadcasted_iota(jnp.int32, (rows, 1), 0)
    lane = lax.broadcasted_iota(jnp.int32, (1, LANES), 1)
    m0 = (lane < HEAD_DIM).astype(F32)
    m1 = 1.0 - m0

    def shift(z, scr, mu):
        prev = jnp.where(row_id == 0, scr[7:8, :], pltpu.roll(z, 1, 0))
        scr[...] = z[rows - 8:rows, :]
        return z + (prev - z) * mu

    r = shift(r_ref[0], zr_scr, prow(_P_MU_R))
    k = shift(k_ref[0], zk_scr, prow(_P_MU_K))
    v = shift(v_ref[0], zv_scr, prow(_P_MU_V))
    mu_l = jnp.concatenate([prow(_P_MU_WA), prow(_P_MU_G)], axis=1)
    zl = shift(l_ref[0], zl_scr, mu_l)
    wa = zl[:, :LANES]
    gl = zl[:, LANES:]

    lo = _mm1(jnp.where(lane < DECAY_LORA, jnp.tanh(wa), wa), wl_ref[0])
    wpre = -(prow(_P_W0) + lo[:, :LANES])
    w = -(jnp.maximum(wpre, 0.0) + jnp.log(1.0 + jnp.exp(-jnp.abs(wpre)))) - 0.5
    logw = -jnp.exp(w)
    a = 1.0 / (1.0 + jnp.exp(-(prow(_P_A0) + lo[:, LANES:])))
    g = _mm1(1.0 / (1.0 + jnp.exp(-gl)), g2_ref[...])

    r2 = lax.broadcasted_iota(jnp.int32, (LANES, LANES), 0)
    c2 = lax.broadcasted_iota(jnp.int32, (LANES, LANES), 1)
    same_head = (r2 < HEAD_DIM) == (c2 < HEAD_DIM)
    ones_bd = same_head.astype(BF16)
    tri_s = (same_head & (r2 > c2)).astype(F32)
    tri_i = (same_head & (r2 >= c2)).astype(F32)
    eye = (r2 == c2).astype(F32)

    kk = k * prow(_P_KK)
    n2 = _mm_exact_rhs(kk * kk, ones_bd)
    kk = kk / jnp.maximum(jnp.sqrt(n2), 1e-12)
    k2 = k * (1.0 + (a - 1.0) * prow(_P_KA))
    ab = -kk
    bb = kk * a

    rt = lax.broadcasted_iota(jnp.int32, (rows, rows), 0)
    ct = lax.broadcasted_iota(jnp.int32, (rows, rows), 1)
    cum = ((jnp.right_shift(rt, 6) == jnp.right_shift(ct, 6)) & (rt >= ct)).astype(BF16)
    lc = _mm3_exact_lhs(cum, logw)
    e_in = jnp.exp(lc)
    e_ex = jnp.exp(lc - logw)
    e_neg = jnp.exp(-lc)
    rt_ = r * e_in
    at_ = ab * e_ex
    kt_ = k2 * e_neg
    bt_ = bb * e_neg

    h = h_scr[...]
    ys = []
    for c in range(rows // CHUNK):
        sl = slice(c * CHUNK, (c + 1) * CHUNK)
        gc = e_in[(c + 1) * CHUNK - 1:(c + 1) * CHUNK, :]
        at_c, rt_c, kt_c, bt_c, v_c = at_[sl], rt_[sl], kt_[sl], bt_[sl], v[sl]
        a_exp = jnp.concatenate([at_c * m0, at_c * m1], axis=0)
        lhs = jnp.concatenate([a_exp, rt_c * m0, rt_c * m1], axis=0)
        rhs = jnp.concatenate([bt_c * m0, bt_c * m1, kt_c * m0, kt_c * m1], axis=0)
        a_all = _mm3(lhs, rhs, _NT)
        n_ab = a_all[:LANES, :LANES] * tri_s
        a_ak = a_all[:LANES, LANES:] * tri_s
        a_rb = a_all[LANES:, :LANES] * tri_i
        a_rk = a_all[LANES:, LANES:] * tri_i
        v_bd = jnp.concatenate([v_c * m0, v_c * m1], axis=0)
        x = jnp.concatenate([a_exp, _mm3(a_ak, v_bd)], axis=1)
        p = n_ab
        for i in range(6):
            x = x + _mm3(p, x)
            if i < 5:
                p = _mm3(p, p)
        rhs2 = jnp.concatenate(
            [x, jnp.concatenate([jnp.zeros((LANES, LANES), F32), v_bd], axis=1)], axis=0)
        o4 = _mm3(jnp.concatenate([a_rb, a_rk], axis=1), rhs2)
        q_eff = rt_c + o4[:CHUNK, :LANES] + o4[CHUNK:, :LANES]
        y_in = o4[:CHUNK, LANES:] + o4[CHUNK:, LANES:]
        bk = jnp.concatenate([bt_c * m0, bt_c * m1, kt_c * m0, kt_c * m1], axis=0) * gc
        o5 = _mm3(bk.T, rhs2)
        m_c = eye * gc + o5[:, :LANES]
        g_c = o5[:, LANES:]
        ys.append(_mm3(q_eff, h) + y_in)
        h = _mm3(m_c, h) + g_c
    h_scr[...] = h

    @pl.when(t == pl.num_programs(2) - 1)
    def _():
        s_ref[0, 0] = h

    y = jnp.concatenate(ys, axis=0)
    avg_bd = (same_head.astype(F32) * (1.0 / HEAD_DIM)).astype(BF16)
    ym = _mm_exact_rhs(y, avg_bd)
    yc = y - ym
    yv = _mm_exact_rhs(yc * yc, avg_bd)
    yn = yc * lax.rsqrt(yv + GN_EPS) * prow(_P_LNG) + prow(_P_LNB)
    bonus = _mm_exact_rhs(r * k2 * prow(_P_RK), ones_bd) * v
    gate = 1.0 / (1.0 + jnp.exp(-gr_ref[0]))
    o_ref[0] = (yn + bonus) * g * gate


def _mm3_exact_lhs(a_bf16, b):
    b1 = b.astype(BF16)
    r1 = b - b1.astype(F32)
    b2 = r1.astype(BF16)
    b3 = (r1 - b2.astype(F32)).astype(BF16)
    return _dg(a_bf16, b1) + (_dg(a_bf16, b2) + _dg(a_bf16, b3))


def _rwkv(proj3, prev8, par, wl, g2, s0, *, rows):
    bsz, t_len, _ = proj3.shape
    col = lambda off: (lambda b, p, t: (b, t, off // LANES + p))
    pcol = lambda off: (lambda b, p, t: (0, off // LANES + p))
    return pl.pallas_call(
        functools.partial(_rwkv_kernel, rows=rows),
        out_shape=(jax.ShapeDtypeStruct((bsz, t_len, D_MODEL), F32),
                   jax.ShapeDtypeStruct((bsz, N_PAIRS, LANES, LANES), F32)),
        grid=(bsz, N_PAIRS, t_len // rows),
        in_specs=[
            pl.BlockSpec((1, rows, LANES), col(OFF_R)),
            pl.BlockSpec((1, rows, LANES), col(OFF_K)),
            pl.BlockSpec((1, rows, LANES), col(OFF_V)),
            pl.BlockSpec((1, rows, 2 * LANES), lambda b, p, t: (b, t, OFF_L // (2 * LANES))),
            pl.BlockSpec((1, rows, LANES), col(OFF_GR)),
            pl.BlockSpec((8, LANES), pcol(OFF_R)),
            pl.BlockSpec((8, LANES), pcol(OFF_K)),
            pl.BlockSpec((8, LANES), pcol(OFF_V)),
            pl.BlockSpec((8, 2 * LANES), lambda b, p, t: (0, OFF_L // (2 * LANES))),
            pl.BlockSpec((1, _P_ROWS, LANES), lambda b, p, t: (p, 0, 0)),
            pl.BlockSpec((1, LANES, 2 * LANES), lambda b, p, t: (p, 0, 0)),
            pl.BlockSpec((GATE_LORA, LANES), lambda b, p, t: (0, p)),
            pl.BlockSpec((1, LANES, LANES), lambda b, p, t: (p, 0, 0)),
        ],
        out_specs=(pl.BlockSpec((1, rows, LANES), lambda b, p, t: (b, t, p)),
                   pl.BlockSpec((1, 1, LANES, LANES), lambda b, p, t: (b, p, 0, 0))),
        scratch_shapes=[pltpu.VMEM((LANES, LANES), F32),
                        pltpu.VMEM((8, LANES), F32), pltpu.VMEM((8, LANES), F32),
                        pltpu.VMEM((8, LANES), F32), pltpu.VMEM((8, 2 * LANES), F32)],
        compiler_params=pltpu.CompilerParams(
            dimension_semantics=("parallel", "parallel", "arbitrary"), vmem_limit_bytes=VMEM_LIMIT),
        name="rwkv",
    )(proj3, proj3, proj3, proj3, proj3, prev8, prev8, prev8, prev8, par, wl, g2, s0)


def _attn_kernel(sink_ref, q_ref, ga_ref, kc_ref, kp_ref, vc_ref, vp_ref, km_ref, vm_ref, bias_ref, rw_ref, o_ref):
    n = pl.program_id(1)
    first = n == 0
    lane = lax.broadcasted_iota(jnp.int32, (1, LANES), 1)
    lo_half = lane < HEAD_DIM
    k_prev = jnp.where(first, km_ref[...], kp_ref[0])
    v_prev = jnp.where(first, vm_ref[...], vp_ref[0])
    kw = jnp.concatenate([k_prev, kc_ref[0]], axis=0).astype(BF16)
    vw = jnp.concatenate([v_prev, vc_ref[0]], axis=0).astype(BF16)
    for p in range(N_PAIRS):
        kh = p // (N_PAIRS // N_KV)
        kx = kw[:, kh * LANES:(kh + 1) * LANES]
        vx = vw[:, kh * LANES:(kh + 1) * LANES]
        qt = q_ref[0, :, p * LANES:(p + 1) * LANES]
        outs = []
        for sub in range(2):
            hd = 2 * p + sub
            qm = jnp.where(lo_half if sub == 0 else jnp.logical_not(lo_half), qt, 0.0).astype(BF16)
            logits = _dg(qm, kx, _NT) + bias_ref[0, hd]
            sink = sink_ref[hd]
            m = jnp.maximum(jnp.max(logits, axis=-1, keepdims=True), sink)
            pe = jnp.exp(logits - m)
            denom = jnp.sum(pe, axis=-1, keepdims=True) + jnp.exp(sink - m)
            outs.append(_dg(pe.astype(BF16), vx) / denom)
        att = jnp.where(lo_half, outs[0], outs[1])
        cs = slice(p * LANES, (p + 1) * LANES)
        gate = 1.0 / (1.0 + jnp.exp(-ga_ref[0, :, cs]))
        o_ref[0, :, cs] = gate * att + rw_ref[0, :, cs]


def _attn_merge(proj3, kvm, bias, sinks, rwg):
    bsz, t_len, _ = proj3.shape
    nb = t_len // BLOCK
    kv_blk = 2 * LANES
    return pl.pallas_call(
        _attn_kernel,
        out_shape=jax.ShapeDtypeStruct((bsz, t_len, D_MODEL), F32),
        grid=(bsz, nb),
        in_specs=[
            pl.BlockSpec(memory_space=pltpu.SMEM),
            pl.BlockSpec((1, BLOCK, D_MODEL), lambda b, n: (b, n, OFF_Q // D_MODEL)),
            pl.BlockSpec((1, BLOCK, D_MODEL), lambda b, n: (b, n, OFF_GA // D_MODEL)),
            pl.BlockSpec((1, BLOCK, kv_blk), lambda b, n: (b, n, OFF_KX // kv_blk)),
            pl.BlockSpec((1, BLOCK, kv_blk), lambda b, n: (b, jnp.maximum(n - 1, 0), OFF_KX // kv_blk)),
            pl.BlockSpec((1, BLOCK, kv_blk), lambda b, n: (b, n, OFF_VX // kv_blk)),
            pl.BlockSpec((1, BLOCK, kv_blk), lambda b, n: (b, jnp.maximum(n - 1, 0), OFF_VX // kv_blk)),
            pl.BlockSpec((BLOCK, kv_blk), lambda b, n: (0, OFF_KX // kv_blk)),
            pl.BlockSpec((BLOCK, kv_blk), lambda b, n: (0, OFF_VX // kv_blk)),
            pl.BlockSpec((1, N_HEADS, BLOCK, 2 * BLOCK), lambda b, n: (jnp.minimum(n, 1), 0, 0, 0)),
            pl.BlockSpec((1, BLOCK, D_MODEL), lambda b, n: (b, n, 0)),
        ],
        out_specs=pl.BlockSpec((1, BLOCK, D_MODEL), lambda b, n: (b, n, 0)),
        compiler_params=pltpu.CompilerParams(
            dimension_semantics=("parallel", "arbitrary"), vmem_limit_bytes=VMEM_LIMIT),
        name="attn_merge",
    )(sinks, proj3, proj3, proj3, proj3, proj3, proj3, kvm, kvm, bias, rwg)


def _out_ffn_kernel(x_ref, mg_ref, ln_ref, wo_ref, w1_ref, w2_ref, o_ref, *, ff_tile):
    ln = ln_ref[...]
    h0 = _layer_norm(x_ref[...], ln[0:1], ln[1:2])
    h1 = _layer_norm(ALPHA * h0 + _dg(mg_ref[...].astype(BF16), wo_ref[...]), ln[2:3], ln[3:4])
    h1b = h1.astype(BF16)
    acc = ALPHA * h1
    for j in range(D_FF // ff_tile):
        cs = slice(j * ff_tile, (j + 1) * ff_tile)
        u = jnp.maximum(_dg(h1b, w1_ref[:, cs]), 0.0)
        acc = acc + _dg((u * u).astype(BF16), w2_ref[cs, :])
    o_ref[...] = _layer_norm(acc, ln[4:5], ln[5:6])


def _out_ffn(x2, merged2, ln_par, wo, w1, w2, *, tm, ff_tile=1024):
    m = x2.shape[0]
    return pl.pallas_call(
        functools.partial(_out_ffn_kernel, ff_tile=ff_tile),
        out_shape=jax.ShapeDtypeStruct((m, D_MODEL), F32),
        grid=(m // tm,),
        in_specs=[
            pl.BlockSpec((tm, D_MODEL), lambda i: (i, 0)),
            pl.BlockSpec((tm, D_MODEL), lambda i: (i, 0)),
            _const_spec((8, D_MODEL)),
            _const_spec((D_MODEL, D_MODEL)),
            _const_spec((D_MODEL, D_FF)),
            _const_spec((D_FF, D_MODEL)),
        ],
        out_specs=pl.BlockSpec((tm, D_MODEL), lambda i: (i, 0)),
        compiler_params=pltpu.CompilerParams(
            dimension_semantics=("parallel",), vmem_limit_bytes=VMEM_LIMIT),
        name="out_ffn",
    )(x2, merged2, ln_par, wo, w1, w2)


def _dup_heads(w):
    h0, h1 = w[:, :HEAD_DIM], w[:, HEAD_DIM:]
    return jnp.concatenate([h0, h0, h1, h1], axis=1)


def kernel(x, meta_tokens, ln0_g, ln0_b, rel_bias, w_in, shift_mu, attn_sinks, decay_w0, decay_w2, iclr_a0, iclr_a2, gate_w2, k_k, k_a, r_k, lnx_g, lnx_b, w_out, ln1_g, ln1_b, w_ff1, w_ff2, ln2_g, ln2_b):
    bsz, seq, _ = x.shape
    assert seq % RWKV_ROWS == 0 and D_MODEL == x.shape[2]
    W = D_MODEL
    wi = w_in[0]
    c0, c1, c2 = W, W + N_KV * HEAD_DIM, W + 2 * N_KV * HEAD_DIM
    zr0 = c2
    c3 = c2 + 3 * W + DECAY_LORA + ICLR_LORA + GATE_LORA
    w_perm = jnp.concatenate([
        wi[:, :c0] * (HEAD_DIM ** -0.5),
        wi[:, c3:c3 + W],
        _dup_heads(wi[:, c0:c1]), _dup_heads(wi[:, c1:c2]),
        wi[:, zr0:zr0 + 3 * W],
        wi[:, zr0 + 3 * W:c3],
        wi[:, c3 + W:],
    ], axis=1).astype(BF16)
    assert w_perm.shape[1] == P_COLS
    mu = shift_mu[0]
    rows_ = [mu[:W], mu[W:2 * W], mu[2 * W:3 * W]]
    vec = lambda a: a.reshape(N_PAIRS, 1, LANES)
    par = jnp.concatenate(
        [vec(rows_[0]), vec(rows_[1]), vec(rows_[2]),
         jnp.broadcast_to(mu[3 * W:3 * W + LANES].reshape(1, 1, LANES), (N_PAIRS, 1, LANES)),
         jnp.broadcast_to(mu[3 * W + LANES:].reshape(1, 1, LANES), (N_PAIRS, 1, LANES)),
         vec(decay_w0[0]), vec(iclr_a0[0]), vec(k_k[0]), vec(k_a[0]), vec(r_k[0].reshape(-1)),
         vec(lnx_g[0]), vec(lnx_b[0]),
         jnp.zeros((N_PAIRS, _P_ROWS - 12, LANES), F32)], axis=1).astype(F32)
    w2p = decay_w2[0].reshape(DECAY_LORA, N_PAIRS, LANES).transpose(1, 0, 2)
    a2p = iclr_a2[0].reshape(ICLR_LORA, N_PAIRS, LANES).transpose(1, 0, 2)
    z = jnp.zeros_like(w2p)
    wl = jnp.concatenate([jnp.concatenate([w2p, z], axis=2), jnp.concatenate([z, a2p], axis=2)], axis=1).astype(BF16)
    g2 = gate_w2[0].astype(BF16)
    ln_par = jnp.stack([ln0_g, ln0_b, ln1_g[0], ln1_b[0], ln2_g[0], ln2_b[0],
                        jnp.zeros_like(ln0_g), jnp.zeros_like(ln0_g)]).astype(F32)
    g0 = ln0_g.reshape(1, W).astype(F32)
    b0 = ln0_b.reshape(1, W).astype(F32)

    bias = _bias_table(rel_bias)

    meta_blk = jnp.concatenate([jnp.zeros((PAD_ROWS, W), F32), meta_tokens.astype(F32)], axis=0)
    proj_m = _ln_proj(meta_blk, g0, b0, w_perm, tm=BLOCK, n_pad=PAD_ROWS)
    zeros8 = jnp.zeros((8, P_COLS), F32)
    s_zero = jnp.zeros((N_PAIRS, LANES, LANES), F32)
    _, s_meta = _rwkv(proj_m[None], zeros8, par, wl, g2, s_zero, rows=BLOCK)
    prev8 = proj_m[BLOCK - 8:]

    x2 = x.reshape(bsz * seq, W)
    proj = _ln_proj(x2, g0, b0, w_perm, tm=256)
    proj3 = proj.reshape(bsz, seq, P_COLS)
    rwg, _ = _rwkv(proj3, prev8, par, wl, g2, s_meta[0], rows=RWKV_ROWS)
    merged = _attn_merge(proj3, proj_m, bias, attn_sinks[0].astype(F32), rwg)
    out = _out_ffn(x2, merged.reshape(bsz * seq, W), ln_par, w_out[0].astype(BF16),
                   w_ff1[0].astype(BF16), w_ff2[0].astype(BF16), tm=512)
    return out.reshape(bsz, seq, W)
```

```python
import functools
import math

import numpy as np
import jax
import jax.numpy as jnp
from jax import lax
from jax.experimental import pallas as pl
from jax.experimental.pallas import tpu as pltpu

F32 = jnp.float32
BF16 = jnp.bfloat16

D_MODEL = 1024
N_META = 16
HEAD_DIM = 64
N_HEADS = D_MODEL // HEAD_DIM
N_KV = 2
BLOCK = 128
N_BUCKETS = 32
MAX_EXACT = 16
MAX_DISTANCE = 128
DECAY_LORA = 64
ICLR_LORA = 64
GATE_LORA = 128
D_FF = 4 * D_MODEL
LN_EPS = 1e-5
GN_EPS = 1e-5 * HEAD_DIM
DEPTH = 1
ALPHA = (2.0 * DEPTH) ** 0.25

LANES = 128
N_PAIRS = D_MODEL // LANES
CHUNK = 64
RWKV_ROWS = 1024
RWKV_GROUP = 256
PAIRS_PER_STEP = 2
PAD_ROWS = BLOCK - N_META

OFF_Q = 0
OFF_GA = 1024
OFF_KX = 2048
OFF_VX = 2304
OFF_R = 2560
OFF_K = 3584
OFF_V = 4608
OFF_L = 5632
OFF_GR = 5888
P_COLS = 6912

VMEM_LIMIT = 48 * 1024 * 1024

_NN = (((1,), (0,)), ((), ()))
_NT = (((1,), (1,)), ((), ()))


def _dg(a, b, dims=_NN):
    return lax.dot_general(a, b, dims, preferred_element_type=F32)


def _split(x):
    hi = x.astype(BF16)
    lo = (x - hi.astype(F32)).astype(BF16)
    return hi, lo


def _mm1(a, b, dims=_NN):
    return _dg(a.astype(BF16), b.astype(BF16), dims)


def _layer_norm(x, g, b):
    mu = jnp.mean(x, axis=-1, keepdims=True)
    xc = x - mu
    var = jnp.mean(xc * xc, axis=-1, keepdims=True)
    return xc * lax.rsqrt(var + LN_EPS) * g + b


def _const_spec(shape):
    return pl.BlockSpec(shape, lambda *_: (0,) * len(shape), pipeline_mode=pl.Buffered(1))


def _bias_kernel(bucket_ref, rel_ref, o_ref):
    for v in range(2):
        bk = bucket_ref[v]
        for h in range(N_HEADS):
            acc = jnp.where(bk < 0, -jnp.inf, 0.0).astype(F32)
            for b in range(N_BUCKETS):
                acc = jnp.where(bk == b, rel_ref[b, h], acc)
            o_ref[v, h] = acc


def _bucket_table():
    q = np.arange(BLOCK)[:, None]
    s = np.arange(2 * BLOCK)[None, :]
    dist = q + BLOCK - s
    in_window = (dist >= 0) & (dist < BLOCK)
    d0 = np.maximum(dist, 0)
    d = np.maximum(d0, 1).astype(np.float32)
    large = MAX_EXACT + (np.log(d / np.float32(MAX_EXACT)) / np.float32(math.log(MAX_DISTANCE / MAX_EXACT))
                         * (N_BUCKETS - MAX_EXACT)).astype(np.int32)
    large = np.minimum(large, N_BUCKETS - 1)
    bucket = np.where(d0 < MAX_EXACT, d0, large).astype(np.int32)
    general = np.where(in_window, bucket, -1)
    first = np.where(in_window & (s >= PAD_ROWS), bucket, -1)
    return np.stack([first, general]).astype(np.int32)


def _bias_table(rel_bias):
    return pl.pallas_call(
        _bias_kernel,
        out_shape=jax.ShapeDtypeStruct((2, N_HEADS, BLOCK, 2 * BLOCK), F32),
        in_specs=[pl.BlockSpec(memory_space=pltpu.VMEM), pl.BlockSpec(memory_space=pltpu.SMEM)],
        out_specs=pl.BlockSpec(memory_space=pltpu.VMEM),
        name="bias_table",
    )(jnp.asarray(_bucket_table()), rel_bias.astype(F32))


def _ln_proj_kernel(x_ref, g_ref, b_ref, w_ref, o_ref, *, n_pad, tn):
    y = _layer_norm(x_ref[...], g_ref[...], b_ref[...])
    if n_pad:
        row = lax.broadcasted_iota(jnp.int32, y.shape, 0)
        y = jnp.where(row < n_pad, 0.0, y)
    yb = y.astype(BF16)
    for j in range(P_COLS // tn):
        cs = slice(j * tn, (j + 1) * tn)
        o_ref[:, cs] = _dg(yb, w_ref[:, cs])


def _ln_proj(x2, g, b, w, *, tm, tn=768, n_pad=0):
    m = x2.shape[0]
    return pl.pallas_call(
        functools.partial(_ln_proj_kernel, n_pad=n_pad, tn=tn),
        out_shape=jax.ShapeDtypeStruct((m, P_COLS), F32),
        grid=(m // tm,),
        in_specs=[
            pl.BlockSpec((tm, D_MODEL), lambda i: (i, 0)),
            _const_spec((1, D_MODEL)),
            _const_spec((1, D_MODEL)),
            _const_spec((D_MODEL, P_COLS)),
        ],
        out_specs=pl.BlockSpec((tm, P_COLS), lambda i: (i, 0)),
        compiler_params=pltpu.CompilerParams(
            dimension_semantics=("parallel",), vmem_limit_bytes=VMEM_LIMIT),
        name="ln_proj",
    )(x2, g, b, w)


(_P_MU_R, _P_MU_K, _P_MU_V, _P_MU_WA, _P_MU_G, _P_W0, _P_A0, _P_KK, _P_KA, _P_RK, _P_LNG, _P_LNB) = range(12)
_P_ROWS = 16


def _rwkv_consts():
    r2, c2 = np.indices((LANES, LANES))
    same = (r2 < HEAD_DIM) == (c2 < HEAD_DIM)
    low = (r2 % HEAD_DIM) > (c2 % HEAD_DIM)
    masks = np.stack([low & same,
                      low & ~same,
                      (r2 % HEAD_DIM) >= (c2 % HEAD_DIM),
                      same, ~same, r2 == c2]).astype(np.float32)
    bf = np.stack([same.astype(np.float32), same.astype(np.float32) / HEAD_DIM])
    t, s = np.indices((CHUNK, CHUNK))
    return jnp.asarray(masks), jnp.asarray(bf, dtype=BF16), jnp.asarray((t >= s).astype(np.float32), dtype=BF16)


def _rwkv_kernel(r_ref, k_ref, v_ref, l_ref, gr_ref, pr_ref, pk_ref, pv_ref, plr_ref,
                 par_ref, wl_ref, g2_ref, s0_ref, msk_ref, bfm_ref, cum_ref, o_ref, s_ref,
                 h_scr, zr_scr, zk_scr, zv_scr, zl_scr, *, rows, group):
    t = pl.program_id(2)

    @pl.when(t == 0)
    def _():
        h_scr[...] = s0_ref[...]
        zr_scr[...] = pr_ref[...]
        zk_scr[...] = pk_ref[...]
        zv_scr[...] = pv_ref[...]
        zl_scr[...] = plr_ref[...]

    cat = lambda parts, axis=0: jnp.concatenate(parts, axis=axis)
    pairs = range(PAIRS_PER_STEP)
    prow = lambda q, i: par_ref[q, i:i + 1, :]
    prow2 = lambda i: cat([prow(q, i) for q in pairs], 1)
    psl = [slice(q * LANES, (q + 1) * LANES) for q in pairs]
    row_id = lax.broadcasted_iota(jnp.int32, (group, 1), 0)
    lane = lax.broadcasted_iota(jnp.int32, (1, LANES), 1)
    lo_half = lane < HEAD_DIM
    m0 = lo_half.astype(F32)
    m1 = 1.0 - m0
    mask_p, mask_a, tri_i = (msk_ref[i].astype(BF16) for i in range(3))
    head_diag, head_anti, eye = (msk_ref[i] for i in range(3, 6))
    ones_bd, avg_bd = bfm_ref[0], bfm_ref[1]
    cum = cum_ref[...]
    mu_l = cat([prow(0, _P_MU_WA), prow(0, _P_MU_G)], 1)
    cpg = group // CHUNK
    n_groups = rows // group
    csl = [slice(c * CHUNK, (c + 1) * CHUNK) for c in range(cpg)]
    zeros_q = jnp.zeros((CHUNK, LANES), BF16)

    def shifted(ref, scr, mu, g):
        z = ref[0, g * group:(g + 1) * group, :]
        prev_row = scr[7:8, :] if g == 0 else ref[0, g * group - 1:g * group, :]
        prev = jnp.where(row_id == 0, prev_row, pltpu.roll(z, 1, 0))
        return z + (prev - z) * mu

    def prep(g):
        r2 = shifted(r_ref, zr_scr, prow2(_P_MU_R), g)
        k2_ = shifted(k_ref, zk_scr, prow2(_P_MU_K), g)
        v2 = shifted(v_ref, zv_scr, prow2(_P_MU_V), g)
        zl = shifted(l_ref, zl_scr, mu_l, g)
        wa, gl = zl[:, :LANES], zl[:, LANES:]
        lora_in = jnp.where(lane < DECAY_LORA, jnp.tanh(wa), wa).astype(BF16)
        gate_in = (1.0 / (1.0 + jnp.exp(-gl))).astype(BF16)
        out = []
        for q in pairs:
            r, k, v = r2[:, psl[q]], k2_[:, psl[q]], v2[:, psl[q]]
            lo = _dg(lora_in, wl_ref[q])
            wpre = -(prow(q, _P_W0) + lo[:, :LANES])
            w = -(jnp.maximum(wpre, 0.0) + jnp.log(1.0 + jnp.exp(-jnp.abs(wpre)))) - 0.5
            logw = -jnp.exp(w)
            a = 1.0 / (1.0 + jnp.exp(-(prow(q, _P_A0) + lo[:, LANES:])))
            gate = _dg(gate_in, g2_ref[:, psl[q]])
            kk = k * prow(q, _P_KK)
            kk = kk / jnp.maximum(jnp.sqrt(_dg((kk * kk).astype(BF16), ones_bd)), 1e-12)
            k2 = k * (1.0 + (a - 1.0) * prow(q, _P_KA))
            hi, lw = _split(logw)
            hl = cat([hi, lw], 1)
            lcs = [_dg(cum, hl[s]) for s in csl]
            lc = cat([x[:, :LANES] + x[:, LANES:] for x in lcs])
            e_in, e_neg = jnp.exp(lc), jnp.exp(-lc)
            rt_ = r * e_in
            at_ = -kk * jnp.exp(lc - logw)
            v_sw = pltpu.roll(v, HEAD_DIM, 1)
            out.append(dict(q=q, r=r, k2=k2, v=v, gate=gate, e_in=e_in, rt=rt_, kt=k2 * e_neg, bt=kk * a * e_neg,
                            at0=at_ * m0, at1=at_ * m1, rt0=rt_ * m0, rt1=rt_ * m1,
                            vs0=(v_sw * m0).astype(BF16), vs1=(v_sw * m1).astype(BF16)))
        return out

    def phase1(pp, out):
        units = [(p_, s) for p_ in pp for s in csl]
        get = lambda n: [p_[n][s] for p_, s in units]
        at0, at1, rt0, rt1, bt_, kt_, vs0, vs1, rt_ = (get(n) for n in ("at0", "at1", "rt0", "rt1", "bt", "kt", "vs0", "vs1", "rt"))
        gcs = [p_["e_in"][s.stop - 1:s.stop, :] for p_, s in units]
        d_a = [_mm1(cat([a0, r0]), cat([b_, k_]), _NT) for a0, r0, b_, k_ in zip(at0, rt0, bt_, kt_)]
        yield
        d_b = [_mm1(cat([a1, r1]), cat([k_, b_]), _NT) for a1, r1, b_, k_ in zip(at1, rt1, bt_, kt_)]
        yield
        top = [cat([da[:CHUNK], db[:CHUNK]]).astype(BF16) for da, db in zip(d_a, d_b)]
        bot = [cat([da[CHUNK:], db[CHUNK:]]).astype(BF16) * tri_i for da, db in zip(d_a, d_b)]
        p = [tp * mask_p for tp in top]
        xc = [cat([a0, a1]) + _dg(tp * mask_a, cat([w0, w1])) for a0, a1, tp, w0, w1 in zip(at0, at1, top, vs0, vs1)]
        yield
        for i in range(6):
            xc = [x + _dg(pw, x.astype(BF16)) for x, pw in zip(xc, p)]
            yield
            if i < 5:
                p = [_dg(pw, pw).astype(BF16) for pw in p]
                yield
        xb = [x.astype(BF16) for x in xc]
        gst = [cat([x[:CHUNK], w1, w0, x[CHUNK:]]) for x, w0, w1 in zip(xb, vs0, vs1)]
        o4 = [_dg(cat([cat([b_[:CHUNK], zeros_q], 1), cat([zeros_q, b_[CHUNK:]], 1)]), g_)
              for b_, g_ in zip(bot, gst)]
        yield
        lhs5 = []
        for b_, k_, gc in zip(bt_, kt_, gcs):
            bg, kg = b_ * gc, k_ * gc
            lhs5.append(cat([bg * m0, kg * m0, kg * m1, bg * m1]).T.astype(BF16))
        o5 = [_dg(l5, g_) for l5, g_ in zip(lhs5, gst)]
        yield
        out["q_eff"] = [(r_ + jnp.where(lo_half, o[:CHUNK], o[CHUNK:])).astype(BF16) for r_, o in zip(rt_, o4)]
        out["y_in"] = [jnp.where(lo_half, o[CHUNK:], o[:CHUNK]) for o in o4]
        out["m_c"] = [(o * head_diag + eye * gc).astype(BF16) for o, gc in zip(o5, gcs)]
        out["g_c"] = [o * head_anti for o in o5]

    def chain_step(f, c, hs, ys):
        for q in pairs:
            u = q * cpg + c
            hb = hs[q].astype(BF16)
            ys[q].append(_dg(f["q_eff"][u], hb) + f["y_in"][u])
            hs[q] = _dg(f["m_c"][u], hb) + f["g_c"][u]

    def epilogue(g, pp, ys):
        gsl = slice(g * group, (g + 1) * group)
        for p_ in pp:
            q = p_["q"]
            y = pltpu.roll(cat(ys[q]), HEAD_DIM, 1)
            yc = y - _dg(y.astype(BF16), avg_bd)
            yv = _dg((yc * yc).astype(BF16), avg_bd)
            yn = yc * lax.rsqrt(yv + GN_EPS) * prow(q, _P_LNG) + prow(q, _P_LNB)
            bonus = _dg((p_["r"] * p_["k2"] * prow(q, _P_RK)).astype(BF16), ones_bd) * p_["v"]
            o_ref[0, gsl, psl[q]] = (yn + bonus) * p_["gate"] / (1.0 + jnp.exp(-gr_ref[0, gsl, psl[q]]))

    hs = [h_scr[q] for q in pairs]
    p_cur = prep(0)
    f_cur = {}
    for _ in phase1(p_cur, f_cur):
        pass
    for g in range(1, n_groups):
        p_nxt = prep(g)
        f_nxt = {}
        ys = [[] for _ in pairs]
        for i, _ in enumerate(phase1(p_nxt, f_nxt)):
            if i % 4 == 3 and len(ys[0]) < cpg:
                chain_step(f_cur, len(ys[0]), hs, ys)
        while len(ys[0]) < cpg:
            chain_step(f_cur, len(ys[0]), hs, ys)
        epilogue(g - 1, p_cur, ys)
        p_cur, f_cur = p_nxt, f_nxt
    ys = [[] for _ in pairs]
    for c in range(cpg):
        chain_step(f_cur, c, hs, ys)
    epilogue(n_groups - 1, p_cur, ys)
    for q in pairs:
        h_scr[q] = hs[q]
    zr_scr[...] = r_ref[0, rows - 8:rows, :]
    zk_scr[...] = k_ref[0, rows - 8:rows, :]
    zv_scr[...] = v_ref[0, rows - 8:rows, :]
    zl_scr[...] = l_ref[0, rows - 8:rows, :]

    @pl.when(t == pl.num_programs(2) - 1)
    def _():
        for q in pairs:
            s_ref[0, q] = hs[q]


def _rwkv(proj3, prev8, par, wl, g2, s0, *, rows):
    bsz, t_len, _ = proj3.shape
    masks, bfm, cum = _rwkv_consts()
    pps = PAIRS_PER_STEP
    wide = pps * LANES
    col = lambda off: (lambda b, p, t: (b, t, off // wide + p))
    pcol = lambda off: (lambda b, p, t: (0, off // wide + p))
    return pl.pallas_call(
        functools.partial(_rwkv_kernel, rows=rows, group=min(RWKV_GROUP, rows)),
        out_shape=(jax.ShapeDtypeStruct((bsz, t_len, D_MODEL), F32),
                   jax.ShapeDtypeStruct((bsz, N_PAIRS, LANES, LANES), F32)),
        grid=(bsz, N_PAIRS // pps, t_len // rows),
        in_specs=[
            pl.BlockSpec((1, rows, wide), col(OFF_R)),
            pl.BlockSpec((1, rows, wide), col(OFF_K)),
            pl.BlockSpec((1, rows, wide), col(OFF_V)),
            pl.BlockSpec((1, rows, 2 * LANES), lambda b, p, t: (b, t, OFF_L // (2 * LANES))),
            pl.BlockSpec((1, rows, wide), col(OFF_GR)),
            pl.BlockSpec((8, wide), pcol(OFF_R)),
            pl.BlockSpec((8, wide), pcol(OFF_K)),
            pl.BlockSpec((8, wide), pcol(OFF_V)),
            pl.BlockSpec((8, 2 * LANES), lambda b, p, t: (0, OFF_L // (2 * LANES))),
            pl.BlockSpec((pps, _P_ROWS, LANES), lambda b, p, t: (p, 0, 0)),
            pl.BlockSpec((pps, LANES, 2 * LANES), lambda b, p, t: (p, 0, 0)),
            pl.BlockSpec((GATE_LORA, wide), lambda b, p, t: (0, p)),
            pl.BlockSpec((pps, LANES, LANES), lambda b, p, t: (p, 0, 0)),
            _const_spec(masks.shape), _const_spec(bfm.shape), _const_spec(cum.shape),
        ],
        out_specs=(pl.BlockSpec((1, rows, wide), lambda b, p, t: (b, t, p)),
                   pl.BlockSpec((1, pps, LANES, LANES), lambda b, p, t: (b, p, 0, 0))),
        scratch_shapes=[pltpu.VMEM((pps, LANES, LANES), F32),
                        pltpu.VMEM((8, wide), F32), pltpu.VMEM((8, wide), F32),
                        pltpu.VMEM((8, wide), F32), pltpu.VMEM((8, 2 * LANES), F32)],
        compiler_params=pltpu.CompilerParams(
            dimension_semantics=("parallel", "parallel", "arbitrary"), vmem_limit_bytes=VMEM_LIMIT),
        name="rwkv",
    )(proj3, proj3, proj3, proj3, proj3, prev8, prev8, prev8, prev8, par, wl, g2, s0, masks, bfm, cum)


def _attn_kernel(sink_ref, q_ref, ga_ref, kc_ref, kp_ref, vc_ref, vp_ref, km_ref, vm_ref, bias_ref, rw_ref, o_ref):
    n = pl.program_id(1)
    first = n == 0
    lane = lax.broadcasted_iota(jnp.int32, (1, LANES), 1)
    lo_half = lane < HEAD_DIM
    k_prev = jnp.where(first, km_ref[...], kp_ref[0])
    v_prev = jnp.where(first, vm_ref[...], vp_ref[0])
    kw = jnp.concatenate([k_prev, kc_ref[0]], axis=0).astype(BF16)
    vw = jnp.concatenate([v_prev, vc_ref[0]], axis=0).astype(BF16)
    for p in range(N_PAIRS):
        kh = p // (N_PAIRS // N_KV)
        kx = kw[:, kh * LANES:(kh + 1) * LANES]
        vx = vw[:, kh * LANES:(kh + 1) * LANES]
        qt = q_ref[0, :, p * LANES:(p + 1) * LANES]
        outs = []
        for sub in range(2):
            hd = 2 * p + sub
            qm = jnp.where(lo_half if sub == 0 else jnp.logical_not(lo_half), qt, 0.0).astype(BF16)
            logits = _dg(qm, kx, _NT) + bias_ref[0, hd]
            sink = sink_ref[hd]
            m = jnp.maximum(jnp.max(logits, axis=-1, keepdims=True), sink)
            pe = jnp.exp(logits - m)
            denom = jnp.sum(pe, axis=-1, keepdims=True) + jnp.exp(sink - m)
            outs.append(_dg(pe.astype(BF16), vx) / denom)
        att = jnp.where(lo_half, outs[0], outs[1])
        cs = slice(p * LANES, (p + 1) * LANES)
        gate = 1.0 / (1.0 + jnp.exp(-ga_ref[0, :, cs]))
        o_ref[0, :, cs] = gate * att + rw_ref[0, :, cs]


def _attn_merge(proj3, kvm, bias, sinks, rwg):
    bsz, t_len, _ = proj3.shape
    nb = t_len // BLOCK
    kv_blk = 2 * LANES
    return pl.pallas_call(
        _attn_kernel,
        out_shape=jax.ShapeDtypeStruct((bsz, t_len, D_MODEL), F32),
        grid=(bsz, nb),
        in_specs=[
            pl.BlockSpec(memory_space=pltpu.SMEM),
            pl.BlockSpec((1, BLOCK, D_MODEL), lambda b, n: (b, n, OFF_Q // D_MODEL)),
            pl.BlockSpec((1, BLOCK, D_MODEL), lambda b, n: (b, n, OFF_GA // D_MODEL)),
            pl.BlockSpec((1, BLOCK, kv_blk), lambda b, n: (b, n, OFF_KX // kv_blk)),
            pl.BlockSpec((1, BLOCK, kv_blk), lambda b, n: (b, jnp.maximum(n - 1, 0), OFF_KX // kv_blk)),
            pl.BlockSpec((1, BLOCK, kv_blk), lambda b, n: (b, n, OFF_VX // kv_blk)),
            pl.BlockSpec((1, BLOCK, kv_blk), lambda b, n: (b, jnp.maximum(n - 1, 0), OFF_VX // kv_blk)),
            pl.BlockSpec((BLOCK, kv_blk), lambda b, n: (0, OFF_KX // kv_blk)),
            pl.BlockSpec((BLOCK, kv_blk), lambda b, n: (0, OFF_VX // kv_blk)),
            pl.BlockSpec((1, N_HEADS, BLOCK, 2 * BLOCK), lambda b, n: (jnp.minimum(n, 1), 0, 0, 0)),
            pl.BlockSpec((1, BLOCK, D_MODEL), lambda b, n: (b, n, 0)),
        ],
        out_specs=pl.BlockSpec((1, BLOCK, D_MODEL), lambda b, n: (b, n, 0)),
        compiler_params=pltpu.CompilerParams(
            dimension_semantics=("parallel", "arbitrary"), vmem_limit_bytes=VMEM_LIMIT),
        name="attn_merge",
    )(sinks, proj3, proj3, proj3, proj3, proj3, proj3, kvm, kvm, bias, rwg)


def _out_ffn_kernel(x_ref, mg_ref, ln_ref, wo_ref, w1_ref, w2_ref, o_ref, *, ff_tile):
    ln = ln_ref[...]
    h0 = _layer_norm(x_ref[...], ln[0:1], ln[1:2])
    h1 = _layer_norm(ALPHA * h0 + _dg(mg_ref[...].astype(BF16), wo_ref[...]), ln[2:3], ln[3:4])
    h1b = h1.astype(BF16)
    acc = ALPHA * h1
    for j in range(D_FF // ff_tile):
        cs = slice(j * ff_tile, (j + 1) * ff_tile)
        u = jnp.maximum(_dg(h1b, w1_ref[:, cs]), 0.0)
        acc = acc + _dg((u * u).astype(BF16), w2_ref[cs, :])
    o_ref[...] = _layer_norm(acc, ln[4:5], ln[5:6])


def _out_ffn(x2, merged2, ln_par, wo, w1, w2, *, tm, ff_tile=1024):
    m = x2.shape[0]
    return pl.pallas_call(
        functools.partial(_out_ffn_kernel, ff_tile=ff_tile),
        out_shape=jax.ShapeDtypeStruct((m, D_MODEL), F32),
        grid=(m // tm,),
        in_specs=[
            pl.BlockSpec((tm, D_MODEL), lambda i: (i, 0)),
            pl.BlockSpec((tm, D_MODEL), lambda i: (i, 0)),
            _const_spec((8, D_MODEL)),
            _const_spec((D_MODEL, D_MODEL)),
            _const_spec((D_MODEL, D_FF)),
            _const_spec((D_FF, D_MODEL)),
        ],
        out_specs=pl.BlockSpec((tm, D_MODEL), lambda i: (i, 0)),
        compiler_params=pltpu.CompilerParams(
            dimension_semantics=("parallel",), vmem_limit_bytes=VMEM_LIMIT),
        name="out_ffn",
    )(x2, merged2, ln_par, wo, w1, w2)


def _dup_heads(w):
    h0, h1 = w[:, :HEAD_DIM], w[:, HEAD_DIM:]
    return jnp.concatenate([h0, h0, h1, h1], axis=1)


def kernel(x, meta_tokens, ln0_g, ln0_b, rel_bias, w_in, shift_mu, attn_sinks, decay_w0, decay_w2, iclr_a0, iclr_a2, gate_w2, k_k, k_a, r_k, lnx_g, lnx_b, w_out, ln1_g, ln1_b, w_ff1, w_ff2, ln2_g, ln2_b):
    bsz, seq, _ = x.shape
    assert seq % RWKV_ROWS == 0 and D_MODEL == x.shape[2]
    W = D_MODEL
    wi = w_in[0]
    c0, c1, c2 = W, W + N_KV * HEAD_DIM, W + 2 * N_KV * HEAD_DIM
    zr0 = c2
    c3 = c2 + 3 * W + DECAY_LORA + ICLR_LORA + GATE_LORA
    w_perm = jnp.concatenate([
        wi[:, :c0] * (HEAD_DIM ** -0.5),
        wi[:, c3:c3 + W],
        _dup_heads(wi[:, c0:c1]), _dup_heads(wi[:, c1:c2]),
        wi[:, zr0:zr0 + 3 * W],
        wi[:, zr0 + 3 * W:c3],
        wi[:, c3 + W:],
    ], axis=1).astype(BF16)
    assert w_perm.shape[1] == P_COLS
    mu = shift_mu[0]
    rows_ = [mu[:W], mu[W:2 * W], mu[2 * W:3 * W]]
    vec = lambda a: a.reshape(N_PAIRS, 1, LANES)
    par = jnp.concatenate(
        [vec(rows_[0]), vec(rows_[1]), vec(rows_[2]),
         jnp.broadcast_to(mu[3 * W:3 * W + LANES].reshape(1, 1, LANES), (N_PAIRS, 1, LANES)),
         jnp.broadcast_to(mu[3 * W + LANES:].reshape(1, 1, LANES), (N_PAIRS, 1, LANES)),
         vec(decay_w0[0]), vec(iclr_a0[0]), vec(k_k[0]), vec(k_a[0]), vec(r_k[0].reshape(-1)),
         vec(lnx_g[0]), vec(lnx_b[0]),
         jnp.zeros((N_PAIRS, _P_ROWS - 12, LANES), F32)], axis=1).astype(F32)
    w2p = decay_w2[0].reshape(DECAY_LORA, N_PAIRS, LANES).transpose(1, 0, 2)
    a2p = iclr_a2[0].reshape(ICLR_LORA, N_PAIRS, LANES).transpose(1, 0, 2)
    z = jnp.zeros_like(w2p)
    wl = jnp.concatenate([jnp.concatenate([w2p, z], axis=2), jnp.concatenate([z, a2p], axis=2)], axis=1).astype(BF16)
    g2 = gate_w2[0].astype(BF16)
    ln_par = jnp.stack([ln0_g, ln0_b, ln1_g[0], ln1_b[0], ln2_g[0], ln2_b[0],
                        jnp.zeros_like(ln0_g), jnp.zeros_like(ln0_g)]).astype(F32)
    g0 = ln0_g.reshape(1, W).astype(F32)
    b0 = ln0_b.reshape(1, W).astype(F32)

    bias = _bias_table(rel_bias)

    meta_blk = jnp.concatenate([jnp.zeros((PAD_ROWS, W), F32), meta_tokens.astype(F32)], axis=0)
    proj_m = _ln_proj(meta_blk, g0, b0, w_perm, tm=BLOCK, n_pad=PAD_ROWS)
    zeros8 = jnp.zeros((8, P_COLS), F32)
    s_zero = jnp.zeros((N_PAIRS, LANES, LANES), F32)
    _, s_meta = _rwkv(proj_m[None], zeros8, par, wl, g2, s_zero, rows=BLOCK)
    prev8 = proj_m[BLOCK - 8:]

    x2 = x.reshape(bsz * seq, W)
    proj = _ln_proj(x2, g0, b0, w_perm, tm=256)
    proj3 = proj.reshape(bsz, seq, P_COLS)
    rwg, _ = _rwkv(proj3, prev8, par, wl, g2, s_meta[0], rows=RWKV_ROWS)
    merged = _attn_merge(proj3, proj_m, bias, attn_sinks[0].astype(F32), rwg)
    out = _out_ffn(x2, merged.reshape(bsz * seq, W), ln_par, w_out[0].astype(BF16),
                   w_ff1[0].astype(BF16), w_ff2[0].astype(BF16), tm=512)
    return out.reshape(bsz, seq, W)
```

```python
import functools
import math

import numpy as np
import jax
import jax.numpy as jnp
from jax import lax
from jax.experimental import pallas as pl
from jax.experimental.pallas import tpu as pltpu

F32 = jnp.float32
BF16 = jnp.bfloat16

D_MODEL = 1024
N_META = 16
HEAD_DIM = 64
N_HEADS = D_MODEL // HEAD_DIM
N_KV = 2
BLOCK = 128
N_BUCKETS = 32
MAX_EXACT = 16
MAX_DISTANCE = 128
DECAY_LORA = 64
ICLR_LORA = 64
GATE_LORA = 128
D_FF = 4 * D_MODEL
LN_EPS = 1e-5
GN_EPS = 1e-5 * HEAD_DIM
DEPTH = 1
ALPHA = (2.0 * DEPTH) ** 0.25

LANES = 128
N_PAIRS = D_MODEL // LANES
CHUNK = 64
RWKV_ROWS = 1024
RWKV_GROUP = 256
PAIRS_PER_STEP = 2
PAD_ROWS = BLOCK - N_META

OFF_Q = 0
OFF_GA = 1024
OFF_KX = 2048
OFF_VX = 2304
OFF_R = 2560
OFF_K = 3584
OFF_V = 4608
OFF_L = 5632
OFF_GR = 5888
P_COLS = 6912

VMEM_LIMIT = 48 * 1024 * 1024

_NN = (((1,), (0,)), ((), ()))
_NT = (((1,), (1,)), ((), ()))


def _dg(a, b, dims=_NN):
    return lax.dot_general(a, b, dims, preferred_element_type=F32)


def _split(x):
    hi = x.astype(BF16)
    lo = (x - hi.astype(F32)).astype(BF16)
    return hi, lo


def _mm1(a, b, dims=_NN):
    return _dg(a.astype(BF16), b.astype(BF16), dims)


def _layer_norm(x, g, b):
    mu = jnp.mean(x, axis=-1, keepdims=True)
    xc = x - mu
    var = jnp.mean(xc * xc, axis=-1, keepdims=True)
    return xc * lax.rsqrt(var + LN_EPS) * g + b


def _const_spec(shape):
    return pl.BlockSpec(shape, lambda *_: (0,) * len(shape), pipeline_mode=pl.Buffered(1))


def _bias_kernel(bucket_ref, rel_ref, o_ref):
    for v in range(2):
        bk = bucket_ref[v]
        for h in range(N_HEADS):
            acc = jnp.where(bk < 0, -jnp.inf, 0.0).astype(F32)
            for b in range(N_BUCKETS):
                acc = jnp.where(bk == b, rel_ref[b, h], acc)
            o_ref[v, h] = acc


def _bucket_table():
    q = np.arange(BLOCK)[:, None]
    s = np.arange(2 * BLOCK)[None, :]
    dist = q + BLOCK - s
    in_window = (dist >= 0) & (dist < BLOCK)
    d0 = np.maximum(dist, 0)
    d = np.maximum(d0, 1).astype(np.float32)
    large = MAX_EXACT + (np.log(d / np.float32(MAX_EXACT)) / np.float32(math.log(MAX_DISTANCE / MAX_EXACT))
                         * (N_BUCKETS - MAX_EXACT)).astype(np.int32)
    large = np.minimum(large, N_BUCKETS - 1)
    bucket = np.where(d0 < MAX_EXACT, d0, large).astype(np.int32)
    general = np.where(in_window, bucket, -1)
    first = np.where(in_window & (s >= PAD_ROWS), bucket, -1)
    return np.stack([first, general]).astype(np.int32)


def _bias_table(rel_bias):
    return pl.pallas_call(
        _bias_kernel,
        out_shape=jax.ShapeDtypeStruct((2, N_HEADS, BLOCK, 2 * BLOCK), F32),
        in_specs=[pl.BlockSpec(memory_space=pltpu.VMEM), pl.BlockSpec(memory_space=pltpu.SMEM)],
        out_specs=pl.BlockSpec(memory_space=pltpu.VMEM),
        name="bias_table",
    )(jnp.asarray(_bucket_table()), rel_bias.astype(F32))


def _ln_proj_kernel(x_ref, g_ref, b_ref, w_ref, o_ref, *, n_pad, tn):
    y = _layer_norm(x_ref[...], g_ref[...], b_ref[...])
    if n_pad:
        row = lax.broadcasted_iota(jnp.int32, y.shape, 0)
        y = jnp.where(row < n_pad, 0.0, y)
    yb = y.astype(BF16)
    for j in range(P_COLS // tn):
        cs = slice(j * tn, (j + 1) * tn)
        o_ref[:, cs] = _dg(yb, w_ref[:, cs])


def _ln_proj(x2, g, b, w, *, tm, tn=768, n_pad=0):
    m = x2.shape[0]
    return pl.pallas_call(
        functools.partial(_ln_proj_kernel, n_pad=n_pad, tn=tn),
        out_shape=jax.ShapeDtypeStruct((m, P_COLS), F32),
        grid=(m // tm,),
        in_specs=[
            pl.BlockSpec((tm, D_MODEL), lambda i: (i, 0)),
            _const_spec((1, D_MODEL)),
            _const_spec((1, D_MODEL)),
            _const_spec((D_MODEL, P_COLS)),
        ],
        out_specs=pl.BlockSpec((tm, P_COLS), lambda i: (i, 0)),
        compiler_params=pltpu.CompilerParams(
            dimension_semantics=("parallel",), vmem_limit_bytes=VMEM_LIMIT),
        name="ln_proj",
    )(x2, g, b, w)


(_P_MU_R, _P_MU_K, _P_MU_V, _P_MU_WA, _P_MU_G, _P_W0, _P_A0, _P_KK, _P_KA, _P_RK, _P_LNG, _P_LNB) = range(12)
_P_ROWS = 16


def _rwkv_consts():
    r2, c2 = np.indices((LANES, LANES))
    same = (r2 < HEAD_DIM) == (c2 < HEAD_DIM)
    low = (r2 % HEAD_DIM) > (c2 % HEAD_DIM)
    masks = np.stack([low & same,
                      low & ~same,
                      (r2 % HEAD_DIM) >= (c2 % HEAD_DIM),
                      same, ~same, r2 == c2]).astype(np.float32)
    bf = np.stack([same.astype(np.float32), same.astype(np.float32) / HEAD_DIM])
    t, s = np.indices((CHUNK, CHUNK))
    return jnp.asarray(masks), jnp.asarray(bf, dtype=BF16), jnp.asarray((t >= s).astype(np.float32), dtype=BF16)


def _rwkv_kernel(r_ref, k_ref, v_ref, l_ref, gr_ref, pr_ref, pk_ref, pv_ref, plr_ref,
                 par_ref, wl_ref, g2_ref, s0_ref, msk_ref, bfm_ref, cum_ref, o_ref, s_ref,
                 h_scr, zr_scr, zk_scr, zv_scr, zl_scr, *, rows, group):
    t = pl.program_id(2)

    @pl.when(t == 0)
    def _():
        h_scr[...] = s0_ref[...]
        zr_scr[...] = pr_ref[...]
        zk_scr[...] = pk_ref[...]
        zv_scr[...] = pv_ref[...]
        zl_scr[...] = plr_ref[...]

    cat = lambda parts, axis=0: jnp.concatenate(parts, axis=axis)
    pairs = range(PAIRS_PER_STEP)
    prow = lambda q, i: par_ref[q, i:i + 1, :]
    prow2 = lambda i: cat([prow(q, i) for q in pairs], 1)
    psl = [slice(q * LANES, (q + 1) * LANES) for q in pairs]
    row_id = lax.broadcasted_iota(jnp.int32, (group, 1), 0)
    lane = lax.broadcasted_iota(jnp.int32, (1, LANES), 1)
    lo_half = lane < HEAD_DIM
    m0 = lo_half.astype(F32)
    m1 = 1.0 - m0
    mask_p, mask_a, tri_i = (msk_ref[i].astype(BF16) for i in range(3))
    head_diag, head_anti, eye = (msk_ref[i] for i in range(3, 6))
    ones_bd, avg_bd = bfm_ref[0], bfm_ref[1]
    cum = cum_ref[...]
    mu_l = cat([prow(0, _P_MU_WA), prow(0, _P_MU_G)], 1)
    cpg = group // CHUNK
    n_groups = rows // group
    csl = [slice(c * CHUNK, (c + 1) * CHUNK) for c in range(cpg)]
    zeros_q = jnp.zeros((CHUNK, LANES), BF16)

    def shifted(ref, scr, mu, g):
        z = ref[0, g * group:(g + 1) * group, :]
        prev_row = scr[7:8, :] if g == 0 else ref[0, g * group - 1:g * group, :]
        prev = jnp.where(row_id == 0, prev_row, pltpu.roll(z, 1, 0))
        return z + (prev - z) * mu

    def prep(g, out):
        r2 = shifted(r_ref, zr_scr, prow2(_P_MU_R), g)
        k2_ = shifted(k_ref, zk_scr, prow2(_P_MU_K), g)
        v2 = shifted(v_ref, zv_scr, prow2(_P_MU_V), g)
        yield
        zl = shifted(l_ref, zl_scr, mu_l, g)
        wa, gl = zl[:, :LANES], zl[:, LANES:]
        lora_in = jnp.where(lane < DECAY_LORA, jnp.tanh(wa), wa).astype(BF16)
        gate_in = (1.0 / (1.0 + jnp.exp(-gl))).astype(BF16)
        yield
        for q in pairs:
            r, k, v = r2[:, psl[q]], k2_[:, psl[q]], v2[:, psl[q]]
            lo = _dg(lora_in, wl_ref[q])
            gate = _dg(gate_in, g2_ref[:, psl[q]])
            kk = k * prow(q, _P_KK)
            n2 = _dg((kk * kk).astype(BF16), ones_bd)
            yield
            wpre = -(prow(q, _P_W0) + lo[:, :LANES])
            w = -(jnp.maximum(wpre, 0.0) + jnp.log(1.0 + jnp.exp(-jnp.abs(wpre)))) - 0.5
            logw = -jnp.exp(w)
            hi, lw = _split(logw)
            hl = cat([hi, lw], 1)
            lcs = [_dg(cum, hl[s]) for s in csl]
            yield
            a = 1.0 / (1.0 + jnp.exp(-(prow(q, _P_A0) + lo[:, LANES:])))
            kk = kk / jnp.maximum(jnp.sqrt(n2), 1e-12)
            k2 = k * (1.0 + (a - 1.0) * prow(q, _P_KA))
            lc = cat([x[:, :LANES] + x[:, LANES:] for x in lcs])
            e_in, e_neg = jnp.exp(lc), jnp.exp(-lc)
            yield
            rt_ = r * e_in
            at_ = -kk * jnp.exp(lc - logw)
            v_sw = pltpu.roll(v, HEAD_DIM, 1)
            kt_, bt_ = k2 * e_neg, kk * a * e_neg
            out.append(dict(q=q, r=r, k2=k2, v=v, gate=gate, e_in=e_in, rt=rt_, at=at_,
                            kt0=kt_ * m0, kt1=kt_ * m1, bt0=bt_ * m0, bt1=bt_ * m1,
                            vs0=(v_sw * m0).astype(BF16), vs1=(v_sw * m1).astype(BF16)))
            yield

    def phase1(pp, out):
        units = [(p_, s) for p_ in pp for s in csl]
        get = lambda n: [p_[n][s] for p_, s in units]
        at_, rt_, bt0, bt1, kt0, kt1, vs0, vs1 = (get(n) for n in ("at", "rt", "bt0", "bt1", "kt0", "kt1", "vs0", "vs1"))
        gcs = [p_["e_in"][s.stop - 1:s.stop, :] for p_, s in units]
        d_ab = [_mm1(cat([a_, r_]), cat([b0, k0, k1, b1]), _NT)
                for a_, r_, b0, k0, k1, b1 in zip(at_, rt_, bt0, kt0, kt1, bt1)]
        yield
        top = [cat([d[:CHUNK, :LANES], d[:CHUNK, LANES:]]).astype(BF16) for d in d_ab]
        bot = [cat([d[CHUNK:, :LANES], d[CHUNK:, LANES:]]).astype(BF16) * tri_i for d in d_ab]
        p = [tp * mask_p for tp in top]
        xb = [(cat([a_ * m0, a_ * m1]) + _dg(tp * mask_a, cat([w0, w1]))).astype(BF16)
              for a_, tp, w0, w1 in zip(at_, top, vs0, vs1)]
        yield
        for i in range(6):
            if i < 5:
                px = [_dg(pw, cat([x, pw], 1)) for x, pw in zip(xb, p)]
                p = [d[:, LANES:].astype(BF16) for d in px]
                xb = [x + d[:, :LANES].astype(BF16) for x, d in zip(xb, px)]
            else:
                xb = [x + _dg(pw, x).astype(BF16) for x, pw in zip(xb, p)]
            yield
        gst = [cat([x[:CHUNK], w1, w0, x[CHUNK:]]) for x, w0, w1 in zip(xb, vs0, vs1)]
        lhs45 = []
        for b_, b0, k0, k1, b1, gc in zip(bot, bt0, kt0, kt1, bt1, gcs):
            lhs45.append(cat([cat([b_[:CHUNK], zeros_q], 1), cat([zeros_q, b_[CHUNK:]], 1),
                              cat([b0 * gc, k0 * gc, k1 * gc, b1 * gc]).T.astype(BF16)]))
        o45 = [_dg(l_, g_) for l_, g_ in zip(lhs45, gst)]
        yield
        out["lhs"] = [cat([(r_ + jnp.where(lo_half, o[:CHUNK], o[CHUNK:LANES])).astype(BF16),
                           (o[LANES:] * head_diag + eye * gc).astype(BF16)])
                      for r_, o, gc in zip(rt_, o45, gcs)]
        out["y_in"] = [jnp.where(lo_half, o[CHUNK:LANES], o[:CHUNK]) for o in o45]
        out["g_c"] = [o[LANES:] * head_anti for o in o45]

    def chain_step(f, c, hs, ys):
        for q in pairs:
            u = q * cpg + c
            yh = _dg(f["lhs"][u], hs[q].astype(BF16))
            ys[q].append(yh[:CHUNK] + f["y_in"][u])
            hs[q] = yh[CHUNK:] + f["g_c"][u]

    def epilogue(g, pp, ys):
        gsl = slice(g * group, (g + 1) * group)
        for p_ in pp:
            q = p_["q"]
            y = pltpu.roll(cat(ys[q]), HEAD_DIM, 1)
            yc = y - _dg(y.astype(BF16), avg_bd)
            yv = _dg((yc * yc).astype(BF16), avg_bd)
            yn = yc * lax.rsqrt(yv + GN_EPS) * prow(q, _P_LNG) + prow(q, _P_LNB)
            bonus = _dg((p_["r"] * p_["k2"] * prow(q, _P_RK)).astype(BF16), ones_bd) * p_["v"]
            o_ref[0, gsl, psl[q]] = (yn + bonus) * p_["gate"] / (1.0 + jnp.exp(-gr_ref[0, gsl, psl[q]]))

    n_stages, n_pieces = 9, 2 + 4 * PAIRS_PER_STEP
    hs = [h_scr[q] for q in pairs]
    p_all = [[] for _ in range(n_groups)]
    f_all = [{} for _ in range(n_groups)]
    for _ in prep(0, p_all[0]):
        pass
    for g in range(n_groups):
        side = prep(g + 1, p_all[g + 1]) if g + 1 < n_groups else iter(())
        ys = [[] for _ in pairs]
        done = 0
        for i, _ in enumerate(phase1(p_all[g], f_all[g])):
            while done * n_stages < (i + 1) * n_pieces:
                next(side, None)
                done += 1
            if g > 0 and i % 2 == 1 and len(ys[0]) < cpg:
                chain_step(f_all[g - 1], len(ys[0]), hs, ys)
        for _ in side:
            pass
        if g > 0:
            while len(ys[0]) < cpg:
                chain_step(f_all[g - 1], len(ys[0]), hs, ys)
            epilogue(g - 1, p_all[g - 1], ys)
    ys = [[] for _ in pairs]
    for c in range(cpg):
        chain_step(f_all[n_groups - 1], c, hs, ys)
    epilogue(n_groups - 1, p_all[n_groups - 1], ys)
    for q in pairs:
        h_scr[q] = hs[q]
    zr_scr[...] = r_ref[0, rows - 8:rows, :]
    zk_scr[...] = k_ref[0, rows - 8:rows, :]
    zv_scr[...] = v_ref[0, rows - 8:rows, :]
    zl_scr[...] = l_ref[0, rows - 8:rows, :]

    @pl.when(t == pl.num_programs(2) - 1)
    def _():
        for q in pairs:
            s_ref[0, q] = hs[q]


def _rwkv(proj3, prev8, par, wl, g2, s0, *, rows):
    bsz, t_len, _ = proj3.shape
    masks, bfm, cum = _rwkv_consts()
    pps = PAIRS_PER_STEP
    wide = pps * LANES
    col = lambda off: (lambda b, p, t: (b, t, off // wide + p))
    pcol = lambda off: (lambda b, p, t: (0, off // wide + p))
    return pl.pallas_call(
        functools.partial(_rwkv_kernel, rows=rows, group=min(RWKV_GROUP, rows)),
        out_shape=(jax.ShapeDtypeStruct((bsz, t_len, D_MODEL), F32),
                   jax.ShapeDtypeStruct((bsz, N_PAIRS, LANES, LANES), F32)),
        grid=(bsz, N_PAIRS // pps, t_len // rows),
        in_specs=[
            pl.BlockSpec((1, rows, wide), col(OFF_R)),
            pl.BlockSpec((1, rows, wide), col(OFF_K)),
            pl.BlockSpec((1, rows, wide), col(OFF_V)),
            pl.BlockSpec((1, rows, 2 * LANES), lambda b, p, t: (b, t, OFF_L // (2 * LANES))),
            pl.BlockSpec((1, rows, wide), col(OFF_GR)),
            pl.BlockSpec((8, wide), pcol(OFF_R)),
            pl.BlockSpec((8, wide), pcol(OFF_K)),
            pl.BlockSpec((8, wide), pcol(OFF_V)),
            pl.BlockSpec((8, 2 * LANES), lambda b, p, t: (0, OFF_L // (2 * LANES))),
            pl.BlockSpec((pps, _P_ROWS, LANES), lambda b, p, t: (p, 0, 0)),
            pl.BlockSpec((pps, LANES, 2 * LANES), lambda b, p, t: (p, 0, 0)),
            pl.BlockSpec((GATE_LORA, wide), lambda b, p, t: (0, p)),
            pl.BlockSpec((pps, LANES, LANES), lambda b, p, t: (p, 0, 0)),
            _const_spec(masks.shape), _const_spec(bfm.shape), _const_spec(cum.shape),
        ],
        out_specs=(pl.BlockSpec((1, rows, wide), lambda b, p, t: (b, t, p)),
                   pl.BlockSpec((1, pps, LANES, LANES), lambda b, p, t: (b, p, 0, 0))),
        scratch_shapes=[pltpu.VMEM((pps, LANES, LANES), F32),
                        pltpu.VMEM((8, wide), F32), pltpu.VMEM((8, wide), F32),
                        pltpu.VMEM((8, wide), F32), pltpu.VMEM((8, 2 * LANES), F32)],
        compiler_params=pltpu.CompilerParams(
            dimension_semantics=("parallel", "parallel", "arbitrary"), vmem_limit_bytes=VMEM_LIMIT),
        name="rwkv",
    )(proj3, proj3, proj3, proj3, proj3, prev8, prev8, prev8, prev8, par, wl, g2, s0, masks, bfm, cum)


def _attn_kernel(sink_ref, q_ref, ga_ref, kc_ref, kp_ref, vc_ref, vp_ref, km_ref, vm_ref, bias_ref, rw_ref, o_ref):
    n = pl.program_id(1)
    first = n == 0
    lane = lax.broadcasted_iota(jnp.int32, (1, LANES), 1)
    lo_half = lane < HEAD_DIM
    k_prev = jnp.where(first, km_ref[...], kp_ref[0])
    v_prev = jnp.where(first, vm_ref[...], vp_ref[0])
    kw = jnp.concatenate([k_prev, kc_ref[0]], axis=0).astype(BF16)
    vw = jnp.concatenate([v_prev, vc_ref[0]], axis=0).astype(BF16)
    for p in range(N_PAIRS):
        kh = p // (N_PAIRS // N_KV)
        kx = kw[:, kh * LANES:(kh + 1) * LANES]
        vx = vw[:, kh * LANES:(kh + 1) * LANES]
        qt = q_ref[0, :, p * LANES:(p + 1) * LANES]
        outs = []
        for sub in range(2):
            hd = 2 * p + sub
            qm = jnp.where(lo_half if sub == 0 else jnp.logical_not(lo_half), qt, 0.0).astype(BF16)
            logits = _dg(qm, kx, _NT) + bias_ref[0, hd]
            sink = sink_ref[hd]
            m = jnp.maximum(jnp.max(logits, axis=-1, keepdims=True), sink)
            pe = jnp.exp(logits - m)
            denom = jnp.sum(pe, axis=-1, keepdims=True) + jnp.exp(sink - m)
            outs.append(_dg(pe.astype(BF16), vx) / denom)
        att = jnp.where(lo_half, outs[0], outs[1])
        cs = slice(p * LANES, (p + 1) * LANES)
        gate = 1.0 / (1.0 + jnp.exp(-ga_ref[0, :, cs]))
        o_ref[0, :, cs] = gate * att + rw_ref[0, :, cs]


def _attn_merge(proj3, kvm, bias, sinks, rwg):
    bsz, t_len, _ = proj3.shape
    nb = t_len // BLOCK
    kv_blk = 2 * LANES
    return pl.pallas_call(
        _attn_kernel,
        out_shape=jax.ShapeDtypeStruct((bsz, t_len, D_MODEL), F32),
        grid=(bsz, nb),
        in_specs=[
            pl.BlockSpec(memory_space=pltpu.SMEM),
            pl.BlockSpec((1, BLOCK, D_MODEL), lambda b, n: (b, n, OFF_Q // D_MODEL)),
            pl.BlockSpec((1, BLOCK, D_MODEL), lambda b, n: (b, n, OFF_GA // D_MODEL)),
            pl.BlockSpec((1, BLOCK, kv_blk), lambda b, n: (b, n, OFF_KX // kv_blk)),
            pl.BlockSpec((1, BLOCK, kv_blk), lambda b, n: (b, jnp.maximum(n - 1, 0), OFF_KX // kv_blk)),
            pl.BlockSpec((1, BLOCK, kv_blk), lambda b, n: (b, n, OFF_VX // kv_blk)),
            pl.BlockSpec((1, BLOCK, kv_blk), lambda b, n: (b, jnp.maximum(n - 1, 0), OFF_VX // kv_blk)),
            pl.BlockSpec((BLOCK, kv_blk), lambda b, n: (0, OFF_KX // kv_blk)),
            pl.BlockSpec((BLOCK, kv_blk), lambda b, n: (0, OFF_VX // kv_blk)),
            pl.BlockSpec((1, N_HEADS, BLOCK, 2 * BLOCK), lambda b, n: (jnp.minimum(n, 1), 0, 0, 0)),
            pl.BlockSpec((1, BLOCK, D_MODEL), lambda b, n: (b, n, 0)),
        ],
        out_specs=pl.BlockSpec((1, BLOCK, D_MODEL), lambda b, n: (b, n, 0)),
        compiler_params=pltpu.CompilerParams(
            dimension_semantics=("parallel", "arbitrary"), vmem_limit_bytes=VMEM_LIMIT),
        name="attn_merge",
    )(sinks, proj3, proj3, proj3, proj3, proj3, proj3, kvm, kvm, bias, rwg)


def _out_ffn_kernel(x_ref, mg_ref, ln_ref, wo_ref, w1_ref, w2_ref, o_ref, *, ff_tile):
    ln = ln_ref[...]
    h0 = _layer_norm(x_ref[...], ln[0:1], ln[1:2])
    h1 = _layer_norm(ALPHA * h0 + _dg(mg_ref[...].astype(BF16), wo_ref[...]), ln[2:3], ln[3:4])
    h1b = h1.astype(BF16)
    acc = ALPHA * h1
    for j in range(D_FF // ff_tile):
        cs = slice(j * ff_tile, (j + 1) * ff_tile)
        u = jnp.maximum(_dg(h1b, w1_ref[:, cs]), 0.0)
        acc = acc + _dg((u * u).astype(BF16), w2_ref[cs, :])
    o_ref[...] = _layer_norm(acc, ln[4:5], ln[5:6])


def _out_ffn(x2, merged2, ln_par, wo, w1, w2, *, tm, ff_tile=1024):
    m = x2.shape[0]
    return pl.pallas_call(
        functools.partial(_out_ffn_kernel, ff_tile=ff_tile),
        out_shape=jax.ShapeDtypeStruct((m, D_MODEL), F32),
        grid=(m // tm,),
        in_specs=[
            pl.BlockSpec((tm, D_MODEL), lambda i: (i, 0)),
            pl.BlockSpec((tm, D_MODEL), lambda i: (i, 0)),
            _const_spec((8, D_MODEL)),
            _const_spec((D_MODEL, D_MODEL)),
            _const_spec((D_MODEL, D_FF)),
            _const_spec((D_FF, D_MODEL)),
        ],
        out_specs=pl.BlockSpec((tm, D_MODEL), lambda i: (i, 0)),
        compiler_params=pltpu.CompilerParams(
            dimension_semantics=("parallel",), vmem_limit_bytes=VMEM_LIMIT),
        name="out_ffn",
    )(x2, merged2, ln_par, wo, w1, w2)


def _dup_heads(w):
    h0, h1 = w[:, :HEAD_DIM], w[:, HEAD_DIM:]
    return jnp.concatenate([h0, h0, h1, h1], axis=1)


def kernel(x, meta_tokens, ln0_g, ln0_b, rel_bias, w_in, shift_mu, attn_sinks, decay_w0, decay_w2, iclr_a0, iclr_a2, gate_w2, k_k, k_a, r_k, lnx_g, lnx_b, w_out, ln1_g, ln1_b, w_ff1, w_ff2, ln2_g, ln2_b):
    bsz, seq, _ = x.shape
    assert seq % RWKV_ROWS == 0 and D_MODEL == x.shape[2]
    W = D_MODEL
    wi = w_in[0]
    c0, c1, c2 = W, W + N_KV * HEAD_DIM, W + 2 * N_KV * HEAD_DIM
    zr0 = c2
    c3 = c2 + 3 * W + DECAY_LORA + ICLR_LORA + GATE_LORA
    w_perm = jnp.concatenate([
        wi[:, :c0] * (HEAD_DIM ** -0.5),
        wi[:, c3:c3 + W],
        _dup_heads(wi[:, c0:c1]), _dup_heads(wi[:, c1:c2]),
        wi[:, zr0:zr0 + 3 * W],
        wi[:, zr0 + 3 * W:c3],
        wi[:, c3 + W:],
    ], axis=1).astype(BF16)
    assert w_perm.shape[1] == P_COLS
    mu = shift_mu[0]
    rows_ = [mu[:W], mu[W:2 * W], mu[2 * W:3 * W]]
    vec = lambda a: a.reshape(N_PAIRS, 1, LANES)
    par = jnp.concatenate(
        [vec(rows_[0]), vec(rows_[1]), vec(rows_[2]),
         jnp.broadcast_to(mu[3 * W:3 * W + LANES].reshape(1, 1, LANES), (N_PAIRS, 1, LANES)),
         jnp.broadcast_to(mu[3 * W + LANES:].reshape(1, 1, LANES), (N_PAIRS, 1, LANES)),
         vec(decay_w0[0]), vec(iclr_a0[0]), vec(k_k[0]), vec(k_a[0]), vec(r_k[0].reshape(-1)),
         vec(lnx_g[0]), vec(lnx_b[0]),
         jnp.zeros((N_PAIRS, _P_ROWS - 12, LANES), F32)], axis=1).astype(F32)
    w2p = decay_w2[0].reshape(DECAY_LORA, N_PAIRS, LANES).transpose(1, 0, 2)
    a2p = iclr_a2[0].reshape(ICLR_LORA, N_PAIRS, LANES).transpose(1, 0, 2)
    z = jnp.zeros_like(w2p)
    wl = jnp.concatenate([jnp.concatenate([w2p, z], axis=2), jnp.concatenate([z, a2p], axis=2)], axis=1).astype(BF16)
    g2 = gate_w2[0].astype(BF16)
    ln_par = jnp.stack([ln0_g, ln0_b, ln1_g[0], ln1_b[0], ln2_g[0], ln2_b[0],
                        jnp.zeros_like(ln0_g), jnp.zeros_like(ln0_g)]).astype(F32)
    g0 = ln0_g.reshape(1, W).astype(F32)
    b0 = ln0_b.reshape(1, W).astype(F32)

    bias = _bias_table(rel_bias)

    meta_blk = jnp.concatenate([jnp.zeros((PAD_ROWS, W), F32), meta_tokens.astype(F32)], axis=0)
    proj_m = _ln_proj(meta_blk, g0, b0, w_perm, tm=BLOCK, n_pad=PAD_ROWS)
    zeros8 = jnp.zeros((8, P_COLS), F32)
    s_zero = jnp.zeros((N_PAIRS, LANES, LANES), F32)
    _, s_meta = _rwkv(proj_m[None], zeros8, par, wl, g2, s_zero, rows=BLOCK)
    prev8 = proj_m[BLOCK - 8:]

    x2 = x.reshape(bsz * seq, W)
    proj = _ln_proj(x2, g0, b0, w_perm, tm=256)
    proj3 = proj.reshape(bsz, seq, P_COLS)
    rwg, _ = _rwkv(proj3, prev8, par, wl, g2, s_meta[0], rows=RWKV_ROWS)
    merged = _attn_merge(proj3, proj_m, bias, attn_sinks[0].astype(F32), rwg)
    out = _out_ffn(x2, merged.reshape(bsz * seq, W), ln_par, w_out[0].astype(BF16),
                   w_ff1[0].astype(BF16), w_ff2[0].astype(BF16), tm=512)
    return out.reshape(bsz, seq, W)
```

```python
import functools
import math

import numpy as np
import jax
import jax.numpy as jnp
from jax import lax
from jax.experimental import pallas as pl
from jax.experimental.pallas import tpu as pltpu

F32 = jnp.float32
BF16 = jnp.bfloat16

D_MODEL = 1024
N_META = 16
HEAD_DIM = 64
N_HEADS = D_MODEL // HEAD_DIM
N_KV = 2
BLOCK = 128
N_BUCKETS = 32
MAX_EXACT = 16
MAX_DISTANCE = 128
DECAY_LORA = 64
ICLR_LORA = 64
GATE_LORA = 128
D_FF = 4 * D_MODEL
LN_EPS = 1e-5
GN_EPS = 1e-5 * HEAD_DIM
DEPTH = 1
ALPHA = (2.0 * DEPTH) ** 0.25

LANES = 128
N_PAIRS = D_MODEL // LANES
CHUNK = 64
RWKV_ROWS = 1024
RWKV_GROUP = 256
PAIRS_PER_STEP = 2
PAD_ROWS = BLOCK - N_META
ATT_ROWS = 2 * BLOCK

OFF_Q = 0
OFF_GA = 1024
OFF_KX = 2048
OFF_VX = 2304
OFF_R = 2560
OFF_K = 3584
OFF_V = 4608
OFF_L = 5632
OFF_GR = 5888
P_COLS = 6912

VMEM_LIMIT = 56 * 1024 * 1024

_NN = (((1,), (0,)), ((), ()))
_NT = (((1,), (1,)), ((), ()))


def _dg(a, b, dims=_NN):
    return lax.dot_general(a, b, dims, preferred_element_type=F32)


def _split(x):
    hi = x.astype(BF16)
    lo = (x - hi.astype(F32)).astype(BF16)
    return hi, lo


def _mm1(a, b, dims=_NN):
    return _dg(a.astype(BF16), b.astype(BF16), dims)


def _layer_norm(x, g, b):
    mu = jnp.mean(x, axis=-1, keepdims=True)
    xc = x - mu
    var = jnp.mean(xc * xc, axis=-1, keepdims=True)
    return xc * lax.rsqrt(var + LN_EPS) * g + b


def _const_spec(shape):
    return pl.BlockSpec(shape, lambda *_: (0,) * len(shape), pipeline_mode=pl.Buffered(1))


def _bias_kernel(bucket_ref, rel_ref, o_ref):
    for v in range(2):
        bk = bucket_ref[v]
        for h in range(N_HEADS):
            acc = jnp.where(bk < 0, -jnp.inf, 0.0).astype(F32)
            for b in range(N_BUCKETS):
                acc = jnp.where(bk == b, rel_ref[b, h], acc)
            o_ref[v, h] = acc


def _bucket_table():
    q = np.arange(BLOCK)[:, None]
    s = np.arange(2 * BLOCK)[None, :]
    dist = q + BLOCK - s
    in_window = (dist >= 0) & (dist < BLOCK)
    d0 = np.maximum(dist, 0)
    d = np.maximum(d0, 1).astype(np.float32)
    large = MAX_EXACT + (np.log(d / np.float32(MAX_EXACT)) / np.float32(math.log(MAX_DISTANCE / MAX_EXACT))
                         * (N_BUCKETS - MAX_EXACT)).astype(np.int32)
    large = np.minimum(large, N_BUCKETS - 1)
    bucket = np.where(d0 < MAX_EXACT, d0, large).astype(np.int32)
    general = np.where(in_window, bucket, -1)
    first = np.where(in_window & (s >= PAD_ROWS), bucket, -1)
    return np.stack([first.T, general.T]).astype(np.int32)


def _bias_table(rel_bias):
    return pl.pallas_call(
        _bias_kernel,
        out_shape=jax.ShapeDtypeStruct((2, N_HEADS, 2 * BLOCK, BLOCK), F32),
        in_specs=[pl.BlockSpec(memory_space=pltpu.VMEM), pl.BlockSpec(memory_space=pltpu.SMEM)],
        out_specs=pl.BlockSpec(memory_space=pltpu.VMEM),
        name="bias_table",
    )(jnp.asarray(_bucket_table()), rel_bias.astype(F32))


def _ln_proj_kernel(x_ref, g_ref, b_ref, w_ref, o_ref, *, n_pad, tn):
    y = _layer_norm(x_ref[...], g_ref[...], b_ref[...])
    if n_pad:
        row = lax.broadcasted_iota(jnp.int32, y.shape, 0)
        y = jnp.where(row < n_pad, 0.0, y)
    yb = y.astype(BF16)
    for j in range(P_COLS // tn):
        cs = slice(j * tn, (j + 1) * tn)
        o_ref[:, cs] = _dg(yb, w_ref[:, cs])


def _ln_proj(x2, g, b, w, *, tm, tn=768, n_pad=0):
    m = x2.shape[0]
    return pl.pallas_call(
        functools.partial(_ln_proj_kernel, n_pad=n_pad, tn=tn),
        out_shape=jax.ShapeDtypeStruct((m, P_COLS), F32),
        grid=(m // tm,),
        in_specs=[
            pl.BlockSpec((tm, D_MODEL), lambda i: (i, 0)),
            _const_spec((1, D_MODEL)),
            _const_spec((1, D_MODEL)),
            _const_spec((D_MODEL, P_COLS)),
        ],
        out_specs=pl.BlockSpec((tm, P_COLS), lambda i: (i, 0)),
        compiler_params=pltpu.CompilerParams(
            dimension_semantics=("parallel",), vmem_limit_bytes=VMEM_LIMIT),
        name="ln_proj",
    )(x2, g, b, w)


(_P_MU_R, _P_MU_K, _P_MU_V, _P_MU_WA, _P_MU_G, _P_W0, _P_A0, _P_KK, _P_KA, _P_RK, _P_LNG, _P_LNB) = range(12)
_P_ROWS = 16


def _rwkv_consts():
    r2, c2 = np.indices((LANES, LANES))
    same = (r2 < HEAD_DIM) == (c2 < HEAD_DIM)
    low = (r2 % HEAD_DIM) > (c2 % HEAD_DIM)
    masks = np.stack([low & same,
                      low & ~same,
                      (r2 % HEAD_DIM) >= (c2 % HEAD_DIM),
                      same, ~same, r2 == c2]).astype(np.float32)
    bf = np.stack([same.astype(np.float32), same.astype(np.float32) / HEAD_DIM])
    t, s = np.indices((CHUNK, CHUNK))
    return jnp.asarray(masks), jnp.asarray(bf, dtype=BF16), jnp.asarray((t >= s).astype(np.float32), dtype=BF16)


def _rwkv_kernel(r_ref, k_ref, v_ref, l_ref, gr_ref, pr_ref, pk_ref, pv_ref, plr_ref,
                 par_ref, wl_ref, g2_ref, s0_ref, msk_ref, bfm_ref, cum_ref, o_ref, s_ref,
                 h_scr, zr_scr, zk_scr, zv_scr, zl_scr, *, rows, group):
    t = pl.program_id(2)

    @pl.when(t == 0)
    def _():
        h_scr[...] = s0_ref[...]
        zr_scr[...] = pr_ref[...]
        zk_scr[...] = pk_ref[...]
        zv_scr[...] = pv_ref[...]
        zl_scr[...] = plr_ref[...]

    cat = lambda parts, axis=0: jnp.concatenate(parts, axis=axis)
    pairs = range(PAIRS_PER_STEP)
    prow = lambda q, i: par_ref[q, i:i + 1, :]
    prow2 = lambda i: cat([prow(q, i) for q in pairs], 1)
    psl = [slice(q * LANES, (q + 1) * LANES) for q in pairs]
    row_id = lax.broadcasted_iota(jnp.int32, (group, 1), 0)
    lane = lax.broadcasted_iota(jnp.int32, (1, LANES), 1)
    lo_half = lane < HEAD_DIM
    m0 = lo_half.astype(F32)
    m1 = 1.0 - m0
    mask_p, mask_a, tri_i = (msk_ref[i].astype(BF16) for i in range(3))
    head_diag, head_anti, eye = (msk_ref[i] for i in range(3, 6))
    ones_bd, avg_bd = bfm_ref[0], bfm_ref[1]
    cum = cum_ref[...]
    mu_l = cat([prow(0, _P_MU_WA), prow(0, _P_MU_G)], 1)
    cpg = group // CHUNK
    n_groups = rows // group
    csl = [slice(c * CHUNK, (c + 1) * CHUNK) for c in range(cpg)]
    zeros_q = jnp.zeros((CHUNK, LANES), BF16)

    def shifted(ref, scr, mu, g):
        z = ref[0, g * group:(g + 1) * group, :]
        prev_row = scr[7:8, :] if g == 0 else ref[0, g * group - 1:g * group, :]
        prev = jnp.where(row_id == 0, prev_row, pltpu.roll(z, 1, 0))
        return z + (prev - z) * mu

    def prep(g, out):
        r2 = shifted(r_ref, zr_scr, prow2(_P_MU_R), g)
        k2_ = shifted(k_ref, zk_scr, prow2(_P_MU_K), g)
        v2 = shifted(v_ref, zv_scr, prow2(_P_MU_V), g)
        yield
        zl = shifted(l_ref, zl_scr, mu_l, g)
        wa, gl = zl[:, :LANES], zl[:, LANES:]
        lora_in = jnp.where(lane < DECAY_LORA, jnp.tanh(wa), wa).astype(BF16)
        gate_in = (1.0 / (1.0 + jnp.exp(-gl))).astype(BF16)
        yield
        for q in pairs:
            r, k, v = r2[:, psl[q]], k2_[:, psl[q]], v2[:, psl[q]]
            lo = _dg(lora_in, wl_ref[q])
            gate = _dg(gate_in, g2_ref[:, psl[q]])
            kk = k * prow(q, _P_KK)
            n2 = _dg((kk * kk).astype(BF16), ones_bd)
            yield
            wpre = -(prow(q, _P_W0) + lo[:, :LANES])
            w = -(jnp.maximum(wpre, 0.0) + jnp.log(1.0 + jnp.exp(-jnp.abs(wpre)))) - 0.5
            logw = -jnp.exp(w)
            hi, lw = _split(logw)
            hl = cat([hi, lw], 1)
            lcs = [_dg(cum, hl[s]) for s in csl]
            yield
            a = 1.0 / (1.0 + jnp.exp(-(prow(q, _P_A0) + lo[:, LANES:])))
            kk = kk / jnp.maximum(jnp.sqrt(n2), 1e-12)
            k2 = k * (1.0 + (a - 1.0) * prow(q, _P_KA))
            lc = cat([x[:, :LANES] + x[:, LANES:] for x in lcs])
            e_in, e_neg = jnp.exp(lc), jnp.exp(-lc)
            yield
            rt_ = r * e_in
            at_ = -kk * jnp.exp(lc - logw)
            v_sw = pltpu.roll(v, HEAD_DIM, 1)
            kt_, bt_ = k2 * e_neg, kk * a * e_neg
            out.append(dict(q=q, r=r, k2=k2, v=v, gate=gate, e_in=e_in, rt=rt_, at=at_,
                            kt0=kt_ * m0, kt1=kt_ * m1, bt0=bt_ * m0, bt1=bt_ * m1,
                            vs0=(v_sw * m0).astype(BF16), vs1=(v_sw * m1).astype(BF16)))
            yield

    def phase1(pp, out):
        units = [(p_, s) for p_ in pp for s in csl]
        get = lambda n: [p_[n][s] for p_, s in units]
        at_, rt_, bt0, bt1, kt0, kt1, vs0, vs1 = (get(n) for n in ("at", "rt", "bt0", "bt1", "kt0", "kt1", "vs0", "vs1"))
        gcs = [p_["e_in"][s.stop - 1:s.stop, :] for p_, s in units]
        d_ab = [_mm1(cat([a_, r_]), cat([b0, k0, k1, b1]), _NT)
                for a_, r_, b0, k0, k1, b1 in zip(at_, rt_, bt0, kt0, kt1, bt1)]
        yield
        top = [cat([d[:CHUNK, :LANES], d[:CHUNK, LANES:]]).astype(BF16) for d in d_ab]
        bot = [cat([d[CHUNK:, :LANES], d[CHUNK:, LANES:]]).astype(BF16) * tri_i for d in d_ab]
        p = [tp * mask_p for tp in top]
        xb = [(cat([a_ * m0, a_ * m1]) + _dg(tp * mask_a, cat([w0, w1]))).astype(BF16)
              for a_, tp, w0, w1 in zip(at_, top, vs0, vs1)]
        yield
        for i in range(6):
            if i < 5:
                px = [_dg(pw, cat([x, pw], 1)) for x, pw in zip(xb, p)]
                p = [d[:, LANES:].astype(BF16) for d in px]
                xb = [x + d[:, :LANES].astype(BF16) for x, d in zip(xb, px)]
            else:
                xb = [x + _dg(pw, x).astype(BF16) for x, pw in zip(xb, p)]
            yield
        gst = [cat([x[:CHUNK], w1, w0, x[CHUNK:]]) for x, w0, w1 in zip(xb, vs0, vs1)]
        lhs45 = []
        for b_, b0, k0, k1, b1, gc in zip(bot, bt0, kt0, kt1, bt1, gcs):
            lhs45.append(cat([cat([b_[:CHUNK], zeros_q], 1), cat([zeros_q, b_[CHUNK:]], 1),
                              cat([b0 * gc, k0 * gc, k1 * gc, b1 * gc]).T.astype(BF16)]))
        o45 = [_dg(l_, g_) for l_, g_ in zip(lhs45, gst)]
        yield
        out["lhs"] = [cat([(r_ + jnp.where(lo_half, o[:CHUNK], o[CHUNK:LANES])).astype(BF16),
                           (o[LANES:] * head_diag + eye * gc).astype(BF16)])
                      for r_, o, gc in zip(rt_, o45, gcs)]
        out["y_in"] = [jnp.where(lo_half, o[CHUNK:LANES], o[:CHUNK]) for o in o45]
        out["g_c"] = [o[LANES:] * head_anti for o in o45]

    def chain_step(f, c, hs, ys):
        for q in pairs:
            u = q * cpg + c
            yh = _dg(f["lhs"][u], hs[q].astype(BF16))
            ys[q].append(yh[:CHUNK] + f["y_in"][u])
            hs[q] = yh[CHUNK:] + f["g_c"][u]

    def epilogue(g, pp, ys):
        gsl = slice(g * group, (g + 1) * group)
        for p_ in pp:
            q = p_["q"]
            y = pltpu.roll(cat(ys[q]), HEAD_DIM, 1)
            yc = y - _dg(y.astype(BF16), avg_bd)
            yv = _dg((yc * yc).astype(BF16), avg_bd)
            yn = yc * lax.rsqrt(yv + GN_EPS) * prow(q, _P_LNG) + prow(q, _P_LNB)
            bonus = _dg((p_["r"] * p_["k2"] * prow(q, _P_RK)).astype(BF16), ones_bd) * p_["v"]
            o_ref[0, gsl, psl[q]] = (yn + bonus) * p_["gate"] / (1.0 + jnp.exp(-gr_ref[0, gsl, psl[q]]))

    n_stages, n_pieces = 9, 2 + 4 * PAIRS_PER_STEP
    hs = [h_scr[q] for q in pairs]
    p_all = [[] for _ in range(n_groups)]
    f_all = [{} for _ in range(n_groups)]
    for _ in prep(0, p_all[0]):
        pass
    for g in range(n_groups):
        side = prep(g + 1, p_all[g + 1]) if g + 1 < n_groups else iter(())
        ys = [[] for _ in pairs]
        done = 0
        for i, _ in enumerate(phase1(p_all[g], f_all[g])):
            while done * n_stages < (i + 1) * n_pieces:
                next(side, None)
                done += 1
            if g > 0 and i % 2 == 1 and len(ys[0]) < cpg:
                chain_step(f_all[g - 1], len(ys[0]), hs, ys)
        for _ in side:
            pass
        if g > 0:
            while len(ys[0]) < cpg:
                chain_step(f_all[g - 1], len(ys[0]), hs, ys)
            epilogue(g - 1, p_all[g - 1], ys)
    ys = [[] for _ in pairs]
    for c in range(cpg):
        chain_step(f_all[n_groups - 1], c, hs, ys)
    epilogue(n_groups - 1, p_all[n_groups - 1], ys)
    for q in pairs:
        h_scr[q] = hs[q]
    zr_scr[...] = r_ref[0, rows - 8:rows, :]
    zk_scr[...] = k_ref[0, rows - 8:rows, :]
    zv_scr[...] = v_ref[0, rows - 8:rows, :]
    zl_scr[...] = l_ref[0, rows - 8:rows, :]

    @pl.when(t == pl.num_programs(2) - 1)
    def _():
        for q in pairs:
            s_ref[0, q] = hs[q]


def _rwkv(proj3, prev8, par, wl, g2, s0, *, rows):
    bsz, t_len, _ = proj3.shape
    masks, bfm, cum = _rwkv_consts()
    pps = PAIRS_PER_STEP
    wide = pps * LANES
    col = lambda off: (lambda b, p, t: (b, t, off // wide + p))
    pcol = lambda off: (lambda b, p, t: (0, off // wide + p))
    return pl.pallas_call(
        functools.partial(_rwkv_kernel, rows=rows, group=min(RWKV_GROUP, rows)),
        out_shape=(jax.ShapeDtypeStruct((bsz, t_len, D_MODEL), F32),
                   jax.ShapeDtypeStruct((bsz, N_PAIRS, LANES, LANES), F32)),
        grid=(bsz, N_PAIRS // pps, t_len // rows),
        in_specs=[
            pl.BlockSpec((1, rows, wide), col(OFF_R)),
            pl.BlockSpec((1, rows, wide), col(OFF_K)),
            pl.BlockSpec((1, rows, wide), col(OFF_V)),
            pl.BlockSpec((1, rows, 2 * LANES), lambda b, p, t: (b, t, OFF_L // (2 * LANES))),
            pl.BlockSpec((1, rows, wide), col(OFF_GR)),
            pl.BlockSpec((8, wide), pcol(OFF_R)),
            pl.BlockSpec((8, wide), pcol(OFF_K)),
            pl.BlockSpec((8, wide), pcol(OFF_V)),
            pl.BlockSpec((8, 2 * LANES), lambda b, p, t: (0, OFF_L // (2 * LANES))),
            pl.BlockSpec((pps, _P_ROWS, LANES), lambda b, p, t: (p, 0, 0)),
            pl.BlockSpec((pps, LANES, 2 * LANES), lambda b, p, t: (p, 0, 0)),
            pl.BlockSpec((GATE_LORA, wide), lambda b, p, t: (0, p)),
            pl.BlockSpec((pps, LANES, LANES), lambda b, p, t: (p, 0, 0)),
            _const_spec(masks.shape), _const_spec(bfm.shape), _const_spec(cum.shape),
        ],
        out_specs=(pl.BlockSpec((1, rows, wide), lambda b, p, t: (b, t, p)),
                   pl.BlockSpec((1, pps, LANES, LANES), lambda b, p, t: (b, p, 0, 0))),
        scratch_shapes=[pltpu.VMEM((pps, LANES, LANES), F32),
                        pltpu.VMEM((8, wide), F32), pltpu.VMEM((8, wide), F32),
                        pltpu.VMEM((8, wide), F32), pltpu.VMEM((8, 2 * LANES), F32)],
        compiler_params=pltpu.CompilerParams(
            dimension_semantics=("parallel", "parallel", "arbitrary"), vmem_limit_bytes=VMEM_LIMIT),
        name="rwkv",
    )(proj3, proj3, proj3, proj3, proj3, prev8, prev8, prev8, prev8, par, wl, g2, s0, masks, bfm, cum)


def _attn_kernel(sink_ref, qg_ref, kvc_ref, kvp_ref, kvm_ref, bf_ref, bg_ref, rw_ref, o_ref):
    first = pl.program_id(1) == 0
    lane = lax.broadcasted_iota(jnp.int32, (1, LANES), 1)
    lo_lane = lane < HEAD_DIM
    lo_row = lax.broadcasted_iota(jnp.int32, (LANES, 1), 0) < HEAD_DIM
    kv_w = 2 * LANES
    kc, vc = kvc_ref[0, :, :kv_w], kvc_ref[0, :, kv_w:]
    kv_prev = jnp.where(first, kvm_ref[...], kvp_ref[0])
    k_prev, v_prev = kv_prev[:, :kv_w], kv_prev[:, kv_w:]
    ppk = N_PAIRS // N_KV

    def scores(sb, kh):
        rs = slice(sb * BLOCK, (sb + 1) * BLOCK)
        kp_, vp_ = (k_prev, v_prev) if sb == 0 else (kc[rs.start - BLOCK:rs.start], vc[rs.start - BLOCK:rs.start])
        kx = jnp.concatenate([kp_, kc[rs]], axis=0)[:, kh * LANES:(kh + 1) * LANES].astype(BF16)
        vx = jnp.concatenate([vp_, vc[rs]], axis=0)[:, kh * LANES:(kh + 1) * LANES]
        lts = []
        for p in range(kh * ppk, (kh + 1) * ppk):
            qt = qg_ref[0, rs, p * LANES:(p + 1) * LANES]
            q2 = jnp.concatenate([jnp.where(lo_lane, qt, 0.0), jnp.where(lo_lane, 0.0, qt)], axis=0).astype(BF16)
            lts.append(_dg(kx, q2, _NT))
        return dict(sb=sb, kh=kh, rs=rs, lts=lts, vxt=vx.T.astype(BF16))

    def finish(g):
        bias_ref = bf_ref if g["sb"] == 0 else bg_ref
        pes, invs = [], []
        for j, lt in enumerate(g["lts"]):
            for sub in range(2):
                hd = 2 * (g["kh"] * ppk + j) + sub
                lg = lt[:, sub * LANES:(sub + 1) * LANES] + bias_ref[0, hd]
                sink = sink_ref[hd]
                m = jnp.maximum(jnp.max(lg, axis=0, keepdims=True), sink)
                pe = jnp.exp(lg - m)
                invs.append(1.0 / (jnp.sum(pe, axis=0, keepdims=True) + jnp.exp(sink - m)))
                pes.append(pe.astype(BF16))
        ots = [_dg(g["vxt"], jnp.concatenate(pes[2 * j:2 * j + 2], axis=1)) for j in range(ppk)]
        for j, ot in enumerate(ots):
            cs = slice((g["kh"] * ppk + j) * LANES, (g["kh"] * ppk + j + 1) * LANES)
            att = jnp.where(lo_row, ot[:, :LANES] * invs[2 * j], ot[:, LANES:] * invs[2 * j + 1]).T
            gate = 1.0 / (1.0 + jnp.exp(-qg_ref[0, g["rs"], D_MODEL + cs.start:D_MODEL + cs.stop]))
            o_ref[0, g["rs"], cs] = gate * att + rw_ref[0, g["rs"], cs]

    order = [(sb, kh) for sb in range(ATT_ROWS // BLOCK) for kh in range(N_KV)]
    cur = scores(*order[0])
    for nxt in order[1:]:
        ahead = scores(*nxt)
        finish(cur)
        cur = ahead
    finish(cur)


def _attn_merge(proj3, kvm, bias, sinks, rwg):
    bsz, t_len, _ = proj3.shape
    qg_w, kv_w = 2 * D_MODEL, 4 * LANES
    assert OFF_GA == OFF_Q + D_MODEL and OFF_VX == OFF_KX + kv_w // 2
    per = ATT_ROWS // BLOCK
    return pl.pallas_call(
        _attn_kernel,
        out_shape=jax.ShapeDtypeStruct((bsz, t_len, D_MODEL), F32),
        grid=(bsz, t_len // ATT_ROWS),
        in_specs=[
            pl.BlockSpec(memory_space=pltpu.SMEM),
            pl.BlockSpec((1, ATT_ROWS, qg_w), lambda b, n: (b, n, OFF_Q // qg_w)),
            pl.BlockSpec((1, ATT_ROWS, kv_w), lambda b, n: (b, n, OFF_KX // kv_w)),
            pl.BlockSpec((1, BLOCK, kv_w), lambda b, n: (b, jnp.maximum(n * per - 1, 0), OFF_KX // kv_w)),
            pl.BlockSpec((BLOCK, kv_w), lambda b, n: (0, OFF_KX // kv_w)),
            pl.BlockSpec((1, N_HEADS, 2 * BLOCK, BLOCK), lambda b, n: (jnp.minimum(n, 1), 0, 0, 0)),
            pl.BlockSpec((1, N_HEADS, 2 * BLOCK, BLOCK), lambda b, n: (1, 0, 0, 0)),
            pl.BlockSpec((1, ATT_ROWS, D_MODEL), lambda b, n: (b, n, 0)),
        ],
        out_specs=pl.BlockSpec((1, ATT_ROWS, D_MODEL), lambda b, n: (b, n, 0)),
        compiler_params=pltpu.CompilerParams(
            dimension_semantics=("parallel", "arbitrary"), vmem_limit_bytes=VMEM_LIMIT),
        name="attn_merge",
    )(sinks, proj3, proj3, proj3, kvm, bias, bias, rwg)


def _out_ffn_kernel(x_ref, mg_ref, ln_ref, wo_ref, w1_ref, w2_ref, o_ref, *, ff_tile):
    ln = ln_ref[...]
    h0 = _layer_norm(x_ref[...], ln[0:1], ln[1:2])
    h1 = _layer_norm(ALPHA * h0 + _dg(mg_ref[...].astype(BF16), wo_ref[...]), ln[2:3], ln[3:4])
    h1b = h1.astype(BF16)
    acc = ALPHA * h1
    for j in range(D_FF // ff_tile):
        cs = slice(j * ff_tile, (j + 1) * ff_tile)
        u = jnp.maximum(_dg(h1b, w1_ref[:, cs]), 0.0)
        acc = acc + _dg((u * u).astype(BF16), w2_ref[cs, :])
    o_ref[...] = _layer_norm(acc, ln[4:5], ln[5:6])


def _out_ffn(x2, merged2, ln_par, wo, w1, w2, *, tm, ff_tile=1024):
    m = x2.shape[0]
    return pl.pallas_call(
        functools.partial(_out_ffn_kernel, ff_tile=ff_tile),
        out_shape=jax.ShapeDtypeStruct((m, D_MODEL), F32),
        grid=(m // tm,),
        in_specs=[
            pl.BlockSpec((tm, D_MODEL), lambda i: (i, 0)),
            pl.BlockSpec((tm, D_MODEL), lambda i: (i, 0)),
            _const_spec((8, D_MODEL)),
            _const_spec((D_MODEL, D_MODEL)),
            _const_spec((D_MODEL, D_FF)),
            _const_spec((D_FF, D_MODEL)),
        ],
        out_specs=pl.BlockSpec((tm, D_MODEL), lambda i: (i, 0)),
        compiler_params=pltpu.CompilerParams(
            dimension_semantics=("parallel",), vmem_limit_bytes=VMEM_LIMIT),
        name="out_ffn",
    )(x2, merged2, ln_par, wo, w1, w2)


def _dup_heads(w):
    h0, h1 = w[:, :HEAD_DIM], w[:, HEAD_DIM:]
    return jnp.concatenate([h0, h0, h1, h1], axis=1)


def kernel(x, meta_tokens, ln0_g, ln0_b, rel_bias, w_in, shift_mu, attn_sinks, decay_w0, decay_w2, iclr_a0, iclr_a2, gate_w2, k_k, k_a, r_k, lnx_g, lnx_b, w_out, ln1_g, ln1_b, w_ff1, w_ff2, ln2_g, ln2_b):
    bsz, seq, _ = x.shape
    assert seq % RWKV_ROWS == 0 and D_MODEL == x.shape[2]
    W = D_MODEL
    wi = w_in[0]
    c0, c1, c2 = W, W + N_KV * HEAD_DIM, W + 2 * N_KV * HEAD_DIM
    zr0 = c2
    c3 = c2 + 3 * W + DECAY_LORA + ICLR_LORA + GATE_LORA
    w_perm = jnp.concatenate([
        wi[:, :c0] * (HEAD_DIM ** -0.5),
        wi[:, c3:c3 + W],
        _dup_heads(wi[:, c0:c1]), _dup_heads(wi[:, c1:c2]),
        wi[:, zr0:zr0 + 3 * W],
        wi[:, zr0 + 3 * W:c3],
        wi[:, c3 + W:],
    ], axis=1).astype(BF16)
    assert w_perm.shape[1] == P_COLS
    mu = shift_mu[0]
    rows_ = [mu[:W], mu[W:2 * W], mu[2 * W:3 * W]]
    vec = lambda a: a.reshape(N_PAIRS, 1, LANES)
    par = jnp.concatenate(
        [vec(rows_[0]), vec(rows_[1]), vec(rows_[2]),
         jnp.broadcast_to(mu[3 * W:3 * W + LANES].reshape(1, 1, LANES), (N_PAIRS, 1, LANES)),
         jnp.broadcast_to(mu[3 * W + LANES:].reshape(1, 1, LANES), (N_PAIRS, 1, LANES)),
         vec(decay_w0[0]), vec(iclr_a0[0]), vec(k_k[0]), vec(k_a[0]), vec(r_k[0].reshape(-1)),
         vec(lnx_g[0]), vec(lnx_b[0]),
         jnp.zeros((N_PAIRS, _P_ROWS - 12, LANES), F32)], axis=1).astype(F32)
    w2p = decay_w2[0].reshape(DECAY_LORA, N_PAIRS, LANES).transpose(1, 0, 2)
    a2p = iclr_a2[0].reshape(ICLR_LORA, N_PAIRS, LANES).transpose(1, 0, 2)
    z = jnp.zeros_like(w2p)
    wl = jnp.concatenate([jnp.concatenate([w2p, z], axis=2), jnp.concatenate([z, a2p], axis=2)], axis=1).astype(BF16)
    g2 = gate_w2[0].astype(BF16)
    ln_par = jnp.stack([ln0_g, ln0_b, ln1_g[0], ln1_b[0], ln2_g[0], ln2_b[0],
                        jnp.zeros_like(ln0_g), jnp.zeros_like(ln0_g)]).astype(F32)
    g0 = ln0_g.reshape(1, W).astype(F32)
    b0 = ln0_b.reshape(1, W).astype(F32)

    bias = _bias_table(rel_bias)

    meta_blk = jnp.concatenate([jnp.zeros((PAD_ROWS, W), F32), meta_tokens.astype(F32)], axis=0)
    proj_m = _ln_proj(meta_blk, g0, b0, w_perm, tm=BLOCK, n_pad=PAD_ROWS)
    zeros8 = jnp.zeros((8, P_COLS), F32)
    s_zero = jnp.zeros((N_PAIRS, LANES, LANES), F32)
    _, s_meta = _rwkv(proj_m[None], zeros8, par, wl, g2, s_zero, rows=BLOCK)
    prev8 = proj_m[BLOCK - 8:]

    x2 = x.reshape(bsz * seq, W)
    proj = _ln_proj(x2, g0, b0, w_perm, tm=512)
    proj3 = proj.reshape(bsz, seq, P_COLS)
    rwg, _ = _rwkv(proj3, prev8, par, wl, g2, s_meta[0], rows=RWKV_ROWS)
    merged = _attn_merge(proj3, proj_m, bias, attn_sinks[0].astype(F32), rwg)
    out = _out_ffn(x2, merged.reshape(bsz * seq, W), ln_par, w_out[0].astype(BF16),
                   w_ff1[0].astype(BF16), w_ff2[0].astype(BF16), tm=512)
    return out.reshape(bsz, seq, W)
```

```python
import functools
import math

import numpy as np
import jax
import jax.numpy as jnp
from jax import lax
from jax.experimental import pallas as pl
from jax.experimental.pallas import tpu as pltpu

F32 = jnp.float32
BF16 = jnp.bfloat16

D_MODEL = 1024
N_META = 16
HEAD_DIM = 64
N_HEADS = D_MODEL // HEAD_DIM
N_KV = 2
BLOCK = 128
N_BUCKETS = 32
MAX_EXACT = 16
MAX_DISTANCE = 128
DECAY_LORA = 64
ICLR_LORA = 64
GATE_LORA = 128
D_FF = 4 * D_MODEL
LN_EPS = 1e-5
GN_EPS = 1e-5 * HEAD_DIM
DEPTH = 1
ALPHA = (2.0 * DEPTH) ** 0.25
LOG2E = math.log2(math.e)

LANES = 128
N_PAIRS = D_MODEL // LANES
CHUNK = 64
RWKV_ROWS = 2048
RWKV_GROUP = 256
PAIRS_PER_STEP = 2
PAD_ROWS = BLOCK - N_META
ATT_ROWS = 2 * BLOCK
SUM_ROWS = 16

OFF_Q = 0
OFF_GA = 1024
OFF_KX = 2048
OFF_VX = 2304
OFF_R = 2560
OFF_K = 3584
OFF_V = 4608
OFF_L = 5632
OFF_GR = 5888
P_COLS = 6912

VMEM_LIMIT = 56 * 1024 * 1024

_NN = (((1,), (0,)), ((), ()))
_NT = (((1,), (1,)), ((), ()))


def _dg(a, b, dims=_NN):
    return lax.dot_general(a, b, dims, preferred_element_type=F32)


def _split(x):
    hi = x.astype(BF16)
    lo = (x - hi.astype(F32)).astype(BF16)
    return hi, lo


def _mm1(a, b, dims=_NN):
    return _dg(a.astype(BF16), b.astype(BF16), dims)


def _layer_norm(x, g, b):
    mu = jnp.mean(x, axis=-1, keepdims=True)
    xc = x - mu
    var = jnp.mean(xc * xc, axis=-1, keepdims=True)
    return xc * lax.rsqrt(var + LN_EPS) * g + b


def _const_spec(shape):
    return pl.BlockSpec(shape, lambda *_: (0,) * len(shape), pipeline_mode=pl.Buffered(1))


def _bias_kernel(bucket_ref, rel_ref, o_ref):
    first, bk = bucket_ref[0], bucket_ref[1]
    for h in range(N_HEADS):
        acc = jnp.where(bk < 0, -jnp.inf, 0.0).astype(F32)
        for b in range(N_BUCKETS):
            acc = jnp.where(bk == b, rel_ref[b, h] * LOG2E, acc)
        o_ref[1, h] = acc
        o_ref[0, h] = jnp.where(first < 0, -jnp.inf, acc)


def _bucket_table():
    q = np.arange(BLOCK)[:, None]
    s = np.arange(2 * BLOCK)[None, :]
    dist = q + BLOCK - s
    in_window = (dist >= 0) & (dist < BLOCK)
    d0 = np.maximum(dist, 0)
    d = np.maximum(d0, 1).astype(np.float32)
    large = MAX_EXACT + (np.log(d / np.float32(MAX_EXACT)) / np.float32(math.log(MAX_DISTANCE / MAX_EXACT))
                         * (N_BUCKETS - MAX_EXACT)).astype(np.int32)
    large = np.minimum(large, N_BUCKETS - 1)
    bucket = np.where(d0 < MAX_EXACT, d0, large).astype(np.int32)
    general = np.where(in_window, bucket, -1)
    first = np.where(in_window & (s >= PAD_ROWS), bucket, -1)
    return np.stack([first.T, general.T]).astype(np.int32)


def _bias_table(rel_bias):
    return pl.pallas_call(
        _bias_kernel,
        out_shape=jax.ShapeDtypeStruct((2, N_HEADS, 2 * BLOCK, BLOCK), F32),
        in_specs=[pl.BlockSpec(memory_space=pltpu.VMEM), pl.BlockSpec(memory_space=pltpu.SMEM)],
        out_specs=pl.BlockSpec(memory_space=pltpu.VMEM),
        name="bias_table",
    )(jnp.asarray(_bucket_table()), rel_bias.astype(F32))


def _ln_proj_kernel(x_ref, g_ref, b_ref, w_ref, o_ref, *, n_pad, tn):
    y = _layer_norm(x_ref[...], g_ref[...], b_ref[...])
    if n_pad:
        row = lax.broadcasted_iota(jnp.int32, y.shape, 0)
        y = jnp.where(row < n_pad, 0.0, y)
    yb = y.astype(BF16)
    for j in range(P_COLS // tn):
        cs = slice(j * tn, (j + 1) * tn)
        o_ref[:, cs] = _dg(yb, w_ref[:, cs])


def _ln_proj(x2, g, b, w, *, tm, tn=768, n_pad=0):
    m = x2.shape[0]
    return pl.pallas_call(
        functools.partial(_ln_proj_kernel, n_pad=n_pad, tn=tn),
        out_shape=jax.ShapeDtypeStruct((m, P_COLS), F32),
        grid=(m // tm,),
        in_specs=[
            pl.BlockSpec((tm, D_MODEL), lambda i: (i, 0)),
            _const_spec((1, D_MODEL)),
            _const_spec((1, D_MODEL)),
            _const_spec((D_MODEL, P_COLS)),
        ],
        out_specs=pl.BlockSpec((tm, P_COLS), lambda i: (i, 0)),
        compiler_params=pltpu.CompilerParams(
            dimension_semantics=("parallel",), vmem_limit_bytes=VMEM_LIMIT),
        name="ln_proj",
    )(x2, g, b, w)


(_P_MU_R, _P_MU_K, _P_MU_V, _P_MU_WA, _P_MU_G, _P_W0, _P_A0, _P_KK, _P_KA, _P_RK, _P_LNG, _P_LNB) = range(12)
_P_ROWS = 16


def _rwkv_consts():
    r2, c2 = np.indices((LANES, LANES))
    same = (r2 < HEAD_DIM) == (c2 < HEAD_DIM)
    low = (r2 % HEAD_DIM) > (c2 % HEAD_DIM)
    masks = np.stack([low & same,
                      low & ~same,
                      (r2 % HEAD_DIM) >= (c2 % HEAD_DIM),
                      same, ~same, r2 == c2]).astype(np.float32)
    bf = np.stack([same.astype(np.float32), same.astype(np.float32) / HEAD_DIM])
    t, s = np.indices((CHUNK, CHUNK))
    return jnp.asarray(masks), jnp.asarray(bf, dtype=BF16), jnp.asarray((t >= s).astype(np.float32), dtype=BF16)


def _rwkv_kernel(r_ref, k_ref, v_ref, l_ref, gr_ref, pr_ref, pk_ref, pv_ref, plr_ref,
                 par_ref, wl_ref, g2_ref, s0_ref, msk_ref, bfm_ref, cum_ref, o_ref, s_ref,
                 h_scr, zr_scr, zk_scr, zv_scr, zl_scr, *, rows, group):
    t = pl.program_id(2)

    @pl.when(t == 0)
    def _():
        h_scr[...] = s0_ref[...]
        zr_scr[...] = pr_ref[...]
        zk_scr[...] = pk_ref[...]
        zv_scr[...] = pv_ref[...]
        zl_scr[...] = plr_ref[...]

    cat = lambda parts, axis=0: jnp.concatenate(parts, axis=axis)
    pairs = range(PAIRS_PER_STEP)
    prow = lambda q, i: par_ref[q, i:i + 1, :]
    prow2 = lambda i: cat([prow(q, i) for q in pairs], 1)
    psl = [slice(q * LANES, (q + 1) * LANES) for q in pairs]
    row_id = lax.broadcasted_iota(jnp.int32, (group, 1), 0)
    lane = lax.broadcasted_iota(jnp.int32, (1, LANES), 1)
    lo_half = lane < HEAD_DIM
    m0 = lo_half.astype(F32)
    m1 = 1.0 - m0
    mask_p, mask_a, tri_i = (msk_ref[i].astype(BF16) for i in range(3))
    head_diag, head_anti, eye = (msk_ref[i] for i in range(3, 6))
    ones_bd, avg_bd = bfm_ref[0], bfm_ref[1]
    cum = cum_ref[...]
    mu_l = cat([prow(0, _P_MU_WA), prow(0, _P_MU_G)], 1)
    cpg = group // CHUNK
    n_groups = rows // group
    csl = [slice(c * CHUNK, (c + 1) * CHUNK) for c in range(cpg)]
    zeros_q = jnp.zeros((CHUNK, LANES), BF16)

    def shifted(ref, scr, mu, g):
        z = ref[0, g * group:(g + 1) * group, :]
        prev_row = scr[7:8, :] if g == 0 else ref[0, g * group - 1:g * group, :]
        prev = jnp.where(row_id == 0, prev_row, pltpu.roll(z, 1, 0))
        return z + (prev - z) * mu

    def prep(g, out):
        r2 = shifted(r_ref, zr_scr, prow2(_P_MU_R), g)
        k2_ = shifted(k_ref, zk_scr, prow2(_P_MU_K), g)
        v2 = shifted(v_ref, zv_scr, prow2(_P_MU_V), g)
        yield
        zl = shifted(l_ref, zl_scr, mu_l, g)
        wa, gl = zl[:, :LANES], zl[:, LANES:]
        lora_in = jnp.where(lane < DECAY_LORA, jnp.tanh(wa), wa).astype(BF16)
        gate_in = (1.0 / (1.0 + jnp.exp(-gl))).astype(BF16)
        yield
        for q in pairs:
            r, k, v = r2[:, psl[q]], k2_[:, psl[q]], v2[:, psl[q]]
            lo = _dg(lora_in, wl_ref[q])
            gate = _dg(gate_in, g2_ref[:, psl[q]])
            kk = k * prow(q, _P_KK)
            n2 = _dg((kk * kk).astype(BF16), ones_bd)
            yield
            wpre = -(prow(q, _P_W0) + lo[:, :LANES])
            w = -(jnp.maximum(wpre, 0.0) + jnp.log(1.0 + jnp.exp(-jnp.abs(wpre)))) - 0.5
            logw = -jnp.exp(w)
            hi, lw = _split(logw)
            hl = cat([hi, lw], 1)
            lcs = [_dg(cum, hl[s]) for s in csl]
            yield
            a = 1.0 / (1.0 + jnp.exp(-(prow(q, _P_A0) + lo[:, LANES:])))
            kk = kk / jnp.maximum(jnp.sqrt(n2), 1e-12)
            k2 = k * (1.0 + (a - 1.0) * prow(q, _P_KA))
            lc = cat([x[:, :LANES] + x[:, LANES:] for x in lcs])
            e_in, e_neg = jnp.exp(lc), jnp.exp(-lc)
            yield
            rt_ = r * e_in
            at_ = -kk * jnp.exp(lc - logw)
            v_sw = pltpu.roll(v, HEAD_DIM, 1)
            kt_, bt_ = k2 * e_neg, kk * a * e_neg
            out.append(dict(q=q, r=r, k2=k2, v=v, gate=gate, e_in=e_in, rt=rt_, at=at_,
                            kt0=kt_ * m0, kt1=kt_ * m1, bt0=bt_ * m0, bt1=bt_ * m1,
                            vs0=(v_sw * m0).astype(BF16), vs1=(v_sw * m1).astype(BF16)))
            yield

    def phase1(pp, out):
        units = [(p_, s) for p_ in pp for s in csl]
        get = lambda n: [p_[n][s] for p_, s in units]
        at_, rt_, bt0, bt1, kt0, kt1, vs0, vs1 = (get(n) for n in ("at", "rt", "bt0", "bt1", "kt0", "kt1", "vs0", "vs1"))
        gcs = [p_["e_in"][s.stop - 1:s.stop, :] for p_, s in units]
        d_ab = [_mm1(cat([a_, r_]), cat([b0, k0, k1, b1]), _NT)
                for a_, r_, b0, k0, k1, b1 in zip(at_, rt_, bt0, kt0, kt1, bt1)]
        yield
        top = [cat([d[:CHUNK, :LANES], d[:CHUNK, LANES:]]).astype(BF16) for d in d_ab]
        bot = [cat([d[CHUNK:, :LANES], d[CHUNK:, LANES:]]).astype(BF16) * tri_i for d in d_ab]
        p = [tp * mask_p for tp in top]
        xb = [(cat([a_ * m0, a_ * m1]) + _dg(tp * mask_a, cat([w0, w1]))).astype(BF16)
              for a_, tp, w0, w1 in zip(at_, top, vs0, vs1)]
        yield
        for i in range(6):
            if i < 5:
                px = [_dg(pw, cat([x, pw], 1)) for x, pw in zip(xb, p)]
                p = [d[:, LANES:].astype(BF16) for d in px]
                xb = [x + d[:, :LANES].astype(BF16) for x, d in zip(xb, px)]
            else:
                xb = [x + _dg(pw, x).astype(BF16) for x, pw in zip(xb, p)]
            yield
        gst = [cat([x[:CHUNK], w1, w0, x[CHUNK:]]) for x, w0, w1 in zip(xb, vs0, vs1)]
        lhs45 = []
        for b_, b0, k0, k1, b1, gc in zip(bot, bt0, kt0, kt1, bt1, gcs):
            lhs45.append(cat([cat([b_[:CHUNK], zeros_q], 1), cat([zeros_q, b_[CHUNK:]], 1),
                              cat([b0 * gc, k0 * gc, k1 * gc, b1 * gc]).T.astype(BF16)]))
        o45 = [_dg(l_, g_) for l_, g_ in zip(lhs45, gst)]
        yield
        out["lhs"] = [cat([(r_ + jnp.where(lo_half, o[:CHUNK], o[CHUNK:LANES])).astype(BF16),
                           (o[LANES:] * head_diag + eye * gc).astype(BF16)])
                      for r_, o, gc in zip(rt_, o45, gcs)]
        out["y_in"] = [jnp.where(lo_half, o[CHUNK:LANES], o[:CHUNK]) for o in o45]
        out["g_c"] = [o[LANES:] * head_anti for o in o45]

    def chain_step(f, c, hs, ys):
        for q in pairs:
            u = q * cpg + c
            yh = _dg(f["lhs"][u], hs[q].astype(BF16))
            ys[q].append(yh[:CHUNK] + f["y_in"][u])
            hs[q] = yh[CHUNK:] + f["g_c"][u]

    def epilogue(g, pp, ys):
        gsl = slice(g * group, (g + 1) * group)
        for p_ in pp:
            q = p_["q"]
            y = pltpu.roll(cat(ys[q]), HEAD_DIM, 1)
            yc = y - _dg(y.astype(BF16), avg_bd)
            yv = _dg((yc * yc).astype(BF16), avg_bd)
            yn = yc * lax.rsqrt(yv + GN_EPS) * prow(q, _P_LNG) + prow(q, _P_LNB)
            bonus = _dg((p_["r"] * p_["k2"] * prow(q, _P_RK)).astype(BF16), ones_bd) * p_["v"]
            o_ref[0, gsl, psl[q]] = (yn + bonus) * p_["gate"] / (1.0 + jnp.exp(-gr_ref[0, gsl, psl[q]]))

    n_stages, n_pieces = 9, 2 + 4 * PAIRS_PER_STEP
    hs = [h_scr[q] for q in pairs]
    p_all = [[] for _ in range(n_groups)]
    f_all = [{} for _ in range(n_groups)]
    for _ in prep(0, p_all[0]):
        pass
    for g in range(n_groups):
        side = prep(g + 1, p_all[g + 1]) if g + 1 < n_groups else iter(())
        ys = [[] for _ in pairs]
        done = 0
        for i, _ in enumerate(phase1(p_all[g], f_all[g])):
            while done * n_stages < (i + 1) * n_pieces:
                next(side, None)
                done += 1
            if g > 0 and i % 2 == 1 and len(ys[0]) < cpg:
                chain_step(f_all[g - 1], len(ys[0]), hs, ys)
        for _ in side:
            pass
        if g > 0:
            while len(ys[0]) < cpg:
                chain_step(f_all[g - 1], len(ys[0]), hs, ys)
            epilogue(g - 1, p_all[g - 1], ys)
    ys = [[] for _ in pairs]
    for c in range(cpg):
        chain_step(f_all[n_groups - 1], c, hs, ys)
    epilogue(n_groups - 1, p_all[n_groups - 1], ys)
    for q in pairs:
        h_scr[q] = hs[q]
    zr_scr[...] = r_ref[0, rows - 8:rows, :]
    zk_scr[...] = k_ref[0, rows - 8:rows, :]
    zv_scr[...] = v_ref[0, rows - 8:rows, :]
    zl_scr[...] = l_ref[0, rows - 8:rows, :]

    @pl.when(t == pl.num_programs(2) - 1)
    def _():
        for q in pairs:
            s_ref[0, q] = hs[q]


def _rwkv(proj3, prev8, par, wl, g2, s0, *, rows):
    bsz, t_len, _ = proj3.shape
    masks, bfm, cum = _rwkv_consts()
    pps = PAIRS_PER_STEP
    wide = pps * LANES
    col = lambda off: (lambda b, p, t: (b, t, off // wide + p))
    pcol = lambda off: (lambda b, p, t: (0, off // wide + p))
    return pl.pallas_call(
        functools.partial(_rwkv_kernel, rows=rows, group=min(RWKV_GROUP, rows)),
        out_shape=(jax.ShapeDtypeStruct((bsz, t_len, D_MODEL), F32),
                   jax.ShapeDtypeStruct((bsz, N_PAIRS, LANES, LANES), F32)),
        grid=(bsz, N_PAIRS // pps, t_len // rows),
        in_specs=[
            pl.BlockSpec((1, rows, wide), col(OFF_R)),
            pl.BlockSpec((1, rows, wide), col(OFF_K)),
            pl.BlockSpec((1, rows, wide), col(OFF_V)),
            pl.BlockSpec((1, rows, 2 * LANES), lambda b, p, t: (b, t, OFF_L // (2 * LANES))),
            pl.BlockSpec((1, rows, wide), col(OFF_GR)),
            pl.BlockSpec((8, wide), pcol(OFF_R)),
            pl.BlockSpec((8, wide), pcol(OFF_K)),
            pl.BlockSpec((8, wide), pcol(OFF_V)),
            pl.BlockSpec((8, 2 * LANES), lambda b, p, t: (0, OFF_L // (2 * LANES))),
            pl.BlockSpec((pps, _P_ROWS, LANES), lambda b, p, t: (p, 0, 0)),
            pl.BlockSpec((pps, LANES, 2 * LANES), lambda b, p, t: (p, 0, 0)),
            pl.BlockSpec((GATE_LORA, wide), lambda b, p, t: (0, p)),
            pl.BlockSpec((pps, LANES, LANES), lambda b, p, t: (p, 0, 0)),
            _const_spec(masks.shape), _const_spec(bfm.shape), _const_spec(cum.shape),
        ],
        out_specs=(pl.BlockSpec((1, rows, wide), lambda b, p, t: (b, t, p)),
                   pl.BlockSpec((1, pps, LANES, LANES), lambda b, p, t: (b, p, 0, 0))),
        scratch_shapes=[pltpu.VMEM((pps, LANES, LANES), F32),
                        pltpu.VMEM((8, wide), F32), pltpu.VMEM((8, wide), F32),
                        pltpu.VMEM((8, wide), F32), pltpu.VMEM((8, 2 * LANES), F32)],
        compiler_params=pltpu.CompilerParams(
            dimension_semantics=("parallel", "parallel", "arbitrary"), vmem_limit_bytes=VMEM_LIMIT),
        name="rwkv",
    )(proj3, proj3, proj3, proj3, proj3, prev8, prev8, prev8, prev8, par, wl, g2, s0, masks, bfm, cum)


def _attn_kernel(sink_ref, qg_ref, kvc_ref, kvp_ref, kvm_ref, bf_ref, bg_ref, rw_ref, o_ref):
    first = pl.program_id(1) == 0
    lane = lax.broadcasted_iota(jnp.int32, (1, LANES), 1)
    lo_lane = lane < HEAD_DIM
    lo_row = lax.broadcasted_iota(jnp.int32, (LANES, 1), 0) < HEAD_DIM
    kv_w = 2 * LANES
    kc, vc = kvc_ref[0, :, :kv_w], kvc_ref[0, :, kv_w:]
    kv_prev = jnp.where(first, kvm_ref[...], kvp_ref[0])
    k_prev, v_prev = kv_prev[:, :kv_w], kv_prev[:, kv_w:]
    ppk = N_PAIRS // N_KV

    def scores(sb, kh):
        rs = slice(sb * BLOCK, (sb + 1) * BLOCK)
        kp_, vp_ = (k_prev, v_prev) if sb == 0 else (kc[rs.start - BLOCK:rs.start], vc[rs.start - BLOCK:rs.start])
        kx = jnp.concatenate([kp_, kc[rs]], axis=0)[:, kh * LANES:(kh + 1) * LANES].astype(BF16)
        vx = jnp.concatenate([vp_, vc[rs]], axis=0)[:, kh * LANES:(kh + 1) * LANES]
        lts = []
        for p in range(kh * ppk, (kh + 1) * ppk):
            qt = qg_ref[0, rs, p * LANES:(p + 1) * LANES]
            q2 = jnp.concatenate([jnp.where(lo_lane, qt, 0.0), jnp.where(lo_lane, 0.0, qt)], axis=0).astype(BF16)
            lts.append(_dg(kx, q2, _NT))
        vxt = jnp.concatenate([vx.T, jnp.ones((SUM_ROWS, 2 * BLOCK), F32)], axis=0).astype(BF16)
        return dict(sb=sb, kh=kh, rs=rs, lts=lts, vxt=vxt)

    def finish(g):
        bias_ref = bf_ref if g["sb"] == 0 else bg_ref
        pes, sinks = [], []
        for j, lt in enumerate(g["lts"]):
            for sub in range(2):
                hd = 2 * (g["kh"] * ppk + j) + sub
                lg = lt[:, sub * LANES:(sub + 1) * LANES] + bias_ref[0, hd]
                m = jnp.maximum(jnp.max(lg, axis=0, keepdims=True), sink_ref[hd])
                sinks.append(jnp.exp2(sink_ref[hd] - m))
                pes.append(jnp.exp2(lg - m).astype(BF16))
        ots = [_dg(g["vxt"], jnp.concatenate(pes[2 * j:2 * j + 2], axis=1)) for j in range(ppk)]
        for j, ot in enumerate(ots):
            cs = slice((g["kh"] * ppk + j) * LANES, (g["kh"] * ppk + j + 1) * LANES)
            inv0 = 1.0 / (ot[LANES:LANES + 1, :LANES] + sinks[2 * j])
            inv1 = 1.0 / (ot[LANES:LANES + 1, LANES:] + sinks[2 * j + 1])
            att = jnp.where(lo_row, ot[:LANES, :LANES] * inv0, ot[:LANES, LANES:] * inv1).T
            gate = 1.0 / (1.0 + jnp.exp(-qg_ref[0, g["rs"], D_MODEL + cs.start:D_MODEL + cs.stop]))
            o_ref[0, g["rs"], cs] = gate * att + rw_ref[0, g["rs"], cs]

    order = [(sb, kh) for sb in range(ATT_ROWS // BLOCK) for kh in range(N_KV)]
    cur = scores(*order[0])
    for nxt in order[1:]:
        ahead = scores(*nxt)
        finish(cur)
        cur = ahead
    finish(cur)


def _attn_merge(proj3, kvm, bias, sinks, rwg):
    bsz, t_len, _ = proj3.shape
    qg_w, kv_w = 2 * D_MODEL, 4 * LANES
    assert OFF_GA == OFF_Q + D_MODEL and OFF_VX == OFF_KX + kv_w // 2
    per = ATT_ROWS // BLOCK
    return pl.pallas_call(
        _attn_kernel,
        out_shape=jax.ShapeDtypeStruct((bsz, t_len, D_MODEL), F32),
        grid=(bsz, t_len // ATT_ROWS),
        in_specs=[
            pl.BlockSpec(memory_space=pltpu.SMEM),
            pl.BlockSpec((1, ATT_ROWS, qg_w), lambda b, n: (b, n, OFF_Q // qg_w)),
            pl.BlockSpec((1, ATT_ROWS, kv_w), lambda b, n: (b, n, OFF_KX // kv_w)),
            pl.BlockSpec((1, BLOCK, kv_w), lambda b, n: (b, jnp.maximum(n * per - 1, 0), OFF_KX // kv_w)),
            pl.BlockSpec((BLOCK, kv_w), lambda b, n: (0, OFF_KX // kv_w)),
            pl.BlockSpec((1, N_HEADS, 2 * BLOCK, BLOCK), lambda b, n: (jnp.minimum(n, 1), 0, 0, 0)),
            pl.BlockSpec((1, N_HEADS, 2 * BLOCK, BLOCK), lambda b, n: (1, 0, 0, 0)),
            pl.BlockSpec((1, ATT_ROWS, D_MODEL), lambda b, n: (b, n, 0)),
        ],
        out_specs=pl.BlockSpec((1, ATT_ROWS, D_MODEL), lambda b, n: (b, n, 0)),
        compiler_params=pltpu.CompilerParams(
            dimension_semantics=("parallel", "arbitrary"), vmem_limit_bytes=VMEM_LIMIT),
        name="attn_merge",
    )(sinks, proj3, proj3, proj3, kvm, bias, bias, rwg)


def _out_ffn_kernel(x_ref, mg_ref, ln_ref, wo_ref, w1_ref, w2_ref, o_ref, *, ff_tile):
    ln = ln_ref[...]
    h0 = _layer_norm(x_ref[...], ln[0:1], ln[1:2])
    h1 = _layer_norm(ALPHA * h0 + _dg(mg_ref[...].astype(BF16), wo_ref[...]), ln[2:3], ln[3:4])
    h1b = h1.astype(BF16)
    acc = ALPHA * h1
    for j in range(D_FF // ff_tile):
        cs = slice(j * ff_tile, (j + 1) * ff_tile)
        u = jnp.maximum(_dg(h1b, w1_ref[:, cs]), 0.0)
        acc = acc + _dg((u * u).astype(BF16), w2_ref[cs, :])
    o_ref[...] = _layer_norm(acc, ln[4:5], ln[5:6])


def _out_ffn(x2, merged2, ln_par, wo, w1, w2, *, tm, ff_tile=1024):
    m = x2.shape[0]
    return pl.pallas_call(
        functools.partial(_out_ffn_kernel, ff_tile=ff_tile),
        out_shape=jax.ShapeDtypeStruct((m, D_MODEL), F32),
        grid=(m // tm,),
        in_specs=[
            pl.BlockSpec((tm, D_MODEL), lambda i: (i, 0)),
            pl.BlockSpec((tm, D_MODEL), lambda i: (i, 0)),
            _const_spec((8, D_MODEL)),
            _const_spec((D_MODEL, D_MODEL)),
            _const_spec((D_MODEL, D_FF)),
            _const_spec((D_FF, D_MODEL)),
        ],
        out_specs=pl.BlockSpec((tm, D_MODEL), lambda i: (i, 0)),
        compiler_params=pltpu.CompilerParams(
            dimension_semantics=("parallel",), vmem_limit_bytes=VMEM_LIMIT),
        name="out_ffn",
    )(x2, merged2, ln_par, wo, w1, w2)


def _dup_heads(w):
    h0, h1 = w[:, :HEAD_DIM], w[:, HEAD_DIM:]
    return jnp.concatenate([h0, h0, h1, h1], axis=1)


def kernel(x, meta_tokens, ln0_g, ln0_b, rel_bias, w_in, shift_mu, attn_sinks, decay_w0, decay_w2, iclr_a0, iclr_a2, gate_w2, k_k, k_a, r_k, lnx_g, lnx_b, w_out, ln1_g, ln1_b, w_ff1, w_ff2, ln2_g, ln2_b):
    bsz, seq, _ = x.shape
    assert seq % RWKV_ROWS == 0 and D_MODEL == x.shape[2]
    W = D_MODEL
    wi = w_in[0]
    c0, c1, c2 = W, W + N_KV * HEAD_DIM, W + 2 * N_KV * HEAD_DIM
    zr0 = c2
    c3 = c2 + 3 * W + DECAY_LORA + ICLR_LORA + GATE_LORA
    wb = wi.astype(BF16)
    w_perm = jnp.concatenate([
        (wi[:, :c0] * (HEAD_DIM ** -0.5 * LOG2E)).astype(BF16),
        wb[:, c3:c3 + W],
        _dup_heads(wb[:, c0:c1]), _dup_heads(wb[:, c1:c2]),
        wb[:, zr0:zr0 + 3 * W],
        wb[:, zr0 + 3 * W:c3],
        wb[:, c3 + W:],
    ], axis=1)
    assert w_perm.shape[1] == P_COLS
    mu = shift_mu[0]
    rows_ = [mu[:W], mu[W:2 * W], mu[2 * W:3 * W]]
    vec = lambda a: a.reshape(N_PAIRS, 1, LANES)
    par = jnp.concatenate(
        [vec(rows_[0]), vec(rows_[1]), vec(rows_[2]),
         jnp.broadcast_to(mu[3 * W:3 * W + LANES].reshape(1, 1, LANES), (N_PAIRS, 1, LANES)),
         jnp.broadcast_to(mu[3 * W + LANES:].reshape(1, 1, LANES), (N_PAIRS, 1, LANES)),
         vec(decay_w0[0]), vec(iclr_a0[0]), vec(k_k[0]), vec(k_a[0]), vec(r_k[0].reshape(-1)),
         vec(lnx_g[0]), vec(lnx_b[0]),
         jnp.zeros((N_PAIRS, _P_ROWS - 12, LANES), F32)], axis=1).astype(F32)
    w2p = decay_w2[0].reshape(DECAY_LORA, N_PAIRS, LANES).transpose(1, 0, 2)
    a2p = iclr_a2[0].reshape(ICLR_LORA, N_PAIRS, LANES).transpose(1, 0, 2)
    z = jnp.zeros_like(w2p)
    wl = jnp.concatenate([jnp.concatenate([w2p, z], axis=2), jnp.concatenate([z, a2p], axis=2)], axis=1).astype(BF16)
    g2 = gate_w2[0].astype(BF16)
    ln_par = jnp.stack([ln0_g, ln0_b, ln1_g[0], ln1_b[0], ln2_g[0], ln2_b[0],
                        jnp.zeros_like(ln0_g), jnp.zeros_like(ln0_g)]).astype(F32)
    g0 = ln0_g.reshape(1, W).astype(F32)
    b0 = ln0_b.reshape(1, W).astype(F32)

    bias = _bias_table(rel_bias)

    meta_blk = jnp.concatenate([jnp.zeros((PAD_ROWS, W), F32), meta_tokens.astype(F32)], axis=0)
    proj_m = _ln_proj(meta_blk, g0, b0, w_perm, tm=BLOCK, n_pad=PAD_ROWS)
    zeros8 = jnp.zeros((8, P_COLS), F32)
    s_zero = jnp.zeros((N_PAIRS, LANES, LANES), F32)
    _, s_meta = _rwkv(proj_m[None], zeros8, par, wl, g2, s_zero, rows=BLOCK)
    prev8 = proj_m[BLOCK - 8:]

    x2 = x.reshape(bsz * seq, W)
    proj = _ln_proj(x2, g0, b0, w_perm, tm=512)
    proj3 = proj.reshape(bsz, seq, P_COLS)
    rwg, _ = _rwkv(proj3, prev8, par, wl, g2, s_meta[0], rows=RWKV_ROWS)
    merged = _attn_merge(proj3, proj_m, bias, attn_sinks[0].astype(F32) * LOG2E, rwg)
    out = _out_ffn(x2, merged.reshape(bsz * seq, W), ln_par, w_out[0].astype(BF16),
                   w_ff1[0].astype(BF16), w_ff2[0].astype(BF16), tm=512)
    return out.reshape(bsz, seq, W)
```

```python
import functools
import math

import numpy as np
import jax
import jax.numpy as jnp
from jax import lax
from jax.experimental import pallas as pl
from jax.experimental.pallas import tpu as pltpu

F32 = jnp.float32
BF16 = jnp.bfloat16

D_MODEL = 1024
N_META = 16
HEAD_DIM = 64
N_HEADS = D_MODEL // HEAD_DIM
N_KV = 2
BLOCK = 128
N_BUCKETS = 32
MAX_EXACT = 16
MAX_DISTANCE = 128
DECAY_LORA = 64
ICLR_LORA = 64
GATE_LORA = 128
D_FF = 4 * D_MODEL
LN_EPS = 1e-5
GN_EPS = 1e-5 * HEAD_DIM
DEPTH = 1
ALPHA = (2.0 * DEPTH) ** 0.25
LOG2E = math.log2(math.e)

LANES = 128
N_PAIRS = D_MODEL // LANES
CHUNK = 64
RWKV_ROWS = 2048
RWKV_GROUP = 256
PAIRS_PER_STEP = 2
PAD_ROWS = BLOCK - N_META
ATT_ROWS = 2 * BLOCK
SUM_ROWS = 16

OFF_Q = 0
OFF_GA = 1024
OFF_KX = 2048
OFF_VX = 2304
OFF_R = 2560
OFF_K = 3584
OFF_V = 4608
OFF_L = 5632
OFF_GR = 5888
P_COLS = 6912


IN_Q = 0
IN_K = 1024
IN_ZR = 1280
IN_GA = 4608
IN_GR = 5632
IN_COLS = 6656
_PROJ_SEGMENTS = (
    (IN_Q, D_MODEL, OFF_Q, HEAD_DIM ** -0.5 * LOG2E),
    (IN_GA, D_MODEL, OFF_GA, 1.0),
    (IN_ZR, OFF_GR - OFF_R, OFF_R, 1.0),
    (IN_GR, D_MODEL, OFF_GR, 1.0),
)

VMEM_LIMIT = 56 * 1024 * 1024

_NN = (((1,), (0,)), ((), ()))
_NT = (((1,), (1,)), ((), ()))


def _dg(a, b, dims=_NN):
    return lax.dot_general(a, b, dims, preferred_element_type=F32)


def _split(x):
    hi = x.astype(BF16)
    lo = (x - hi.astype(F32)).astype(BF16)
    return hi, lo


def _mm1(a, b, dims=_NN):
    return _dg(a.astype(BF16), b.astype(BF16), dims)


def _layer_norm(x, g, b):
    mu = jnp.mean(x, axis=-1, keepdims=True)
    xc = x - mu
    var = jnp.mean(xc * xc, axis=-1, keepdims=True)
    return xc * lax.rsqrt(var + LN_EPS) * g + b


def _const_spec(shape):
    return pl.BlockSpec(shape, lambda *_: (0,) * len(shape), pipeline_mode=pl.Buffered(1))


def _bias_kernel(bucket_ref, rel_ref, o_ref):
    first, bk = bucket_ref[0], bucket_ref[1]
    for h in range(N_HEADS):
        acc = jnp.where(bk < 0, -jnp.inf, 0.0).astype(F32)
        for b in range(N_BUCKETS):
            acc = jnp.where(bk == b, rel_ref[b, h] * LOG2E, acc)
        o_ref[1, h] = acc
        o_ref[0, h] = jnp.where(first < 0, -jnp.inf, acc)


def _bucket_table():
    q = np.arange(BLOCK)[:, None]
    s = np.arange(2 * BLOCK)[None, :]
    dist = q + BLOCK - s
    in_window = (dist >= 0) & (dist < BLOCK)
    d0 = np.maximum(dist, 0)
    d = np.maximum(d0, 1).astype(np.float32)
    large = MAX_EXACT + (np.log(d / np.float32(MAX_EXACT)) / np.float32(math.log(MAX_DISTANCE / MAX_EXACT))
                         * (N_BUCKETS - MAX_EXACT)).astype(np.int32)
    large = np.minimum(large, N_BUCKETS - 1)
    bucket = np.where(d0 < MAX_EXACT, d0, large).astype(np.int32)
    general = np.where(in_window, bucket, -1)
    first = np.where(in_window & (s >= PAD_ROWS), bucket, -1)
    return np.stack([first.T, general.T]).astype(np.int32)


def _bias_table(rel_bias):
    return pl.pallas_call(
        _bias_kernel,
        out_shape=jax.ShapeDtypeStruct((2, N_HEADS, 2 * BLOCK, BLOCK), F32),
        in_specs=[pl.BlockSpec(memory_space=pltpu.VMEM), pl.BlockSpec(memory_space=pltpu.SMEM)],
        out_specs=pl.BlockSpec(memory_space=pltpu.VMEM),
        name="bias_table",
    )(jnp.asarray(_bucket_table()), rel_bias.astype(F32))


def _ln_proj_kernel(x_ref, g_ref, b_ref, w_ref, o_ref, *, n_pad, tn):
    y = _layer_norm(x_ref[...], g_ref[...], b_ref[...])
    if n_pad:
        row = lax.broadcasted_iota(jnp.int32, y.shape, 0)
        y = jnp.where(row < n_pad, 0.0, y)
    yb = y.astype(BF16)
    lo_lane = lax.broadcasted_iota(jnp.int32, (1, LANES), 1) < HEAD_DIM

    def dup(z):
        zr = pltpu.roll(z, HEAD_DIM, 1)
        return jnp.concatenate([jnp.where(lo_lane, z, zr), jnp.where(lo_lane, zr, z)], axis=1)

    for src, width, dst, scale in _PROJ_SEGMENTS:
        for j in range(0, width, tn):
            w_ = min(tn, width - j)
            z = _dg(yb, w_ref[:, src + j:src + j + w_])
            o_ref[:, dst + j:dst + j + w_] = z * scale if scale != 1.0 else z
    kv = _dg(yb, w_ref[:, IN_K:IN_K + 2 * LANES])
    o_ref[:, OFF_KX:OFF_KX + 2 * LANES] = dup(kv[:, :LANES])
    o_ref[:, OFF_VX:OFF_VX + 2 * LANES] = dup(kv[:, LANES:])


def _ln_proj(x2, g, b, w, *, tm, tn=512, n_pad=0):
    m = x2.shape[0]
    return pl.pallas_call(
        functools.partial(_ln_proj_kernel, n_pad=n_pad, tn=tn),
        out_shape=jax.ShapeDtypeStruct((m, P_COLS), F32),
        grid=(m // tm,),
        in_specs=[
            pl.BlockSpec((tm, D_MODEL), lambda i: (i, 0)),
            _const_spec((1, D_MODEL)),
            _const_spec((1, D_MODEL)),
            _const_spec((D_MODEL, IN_COLS)),
        ],
        out_specs=pl.BlockSpec((tm, P_COLS), lambda i: (i, 0)),
        compiler_params=pltpu.CompilerParams(
            dimension_semantics=("parallel",), vmem_limit_bytes=VMEM_LIMIT),
        name="ln_proj",
    )(x2, g, b, w)


(_P_MU_R, _P_MU_K, _P_MU_V, _P_MU_WA, _P_MU_G, _P_W0, _P_A0, _P_KK, _P_KA, _P_RK, _P_LNG, _P_LNB) = range(12)
_P_ROWS = 16


def _rwkv_consts():
    r2, c2 = np.indices((LANES, LANES))
    same = (r2 < HEAD_DIM) == (c2 < HEAD_DIM)
    low = (r2 % HEAD_DIM) > (c2 % HEAD_DIM)
    masks = np.stack([low & same,
                      low & ~same,
                      (r2 % HEAD_DIM) >= (c2 % HEAD_DIM),
                      same, ~same, r2 == c2]).astype(np.float32)
    bf = np.stack([same.astype(np.float32), same.astype(np.float32) / HEAD_DIM])
    t, s = np.indices((CHUNK, CHUNK))
    return jnp.asarray(masks), jnp.asarray(bf, dtype=BF16), jnp.asarray((t >= s).astype(np.float32), dtype=BF16)


def _rwkv_kernel(r_ref, k_ref, v_ref, l_ref, gr_ref, pr_ref, pk_ref, pv_ref, plr_ref,
                 par_ref, wl_ref, g2_ref, s0_ref, msk_ref, bfm_ref, cum_ref, o_ref, s_ref,
                 h_scr, zr_scr, zk_scr, zv_scr, zl_scr, *, rows, group):
    t = pl.program_id(2)

    @pl.when(t == 0)
    def _():
        h_scr[...] = s0_ref[...]
        zr_scr[...] = pr_ref[...]
        zk_scr[...] = pk_ref[...]
        zv_scr[...] = pv_ref[...]
        zl_scr[...] = plr_ref[...]

    cat = lambda parts, axis=0: jnp.concatenate(parts, axis=axis)
    pairs = range(PAIRS_PER_STEP)
    prow = lambda q, i: par_ref[q, i:i + 1, :]
    prow2 = lambda i: cat([prow(q, i) for q in pairs], 1)
    psl = [slice(q * LANES, (q + 1) * LANES) for q in pairs]
    row_id = lax.broadcasted_iota(jnp.int32, (group, 1), 0)
    lane = lax.broadcasted_iota(jnp.int32, (1, LANES), 1)
    lo_half = lane < HEAD_DIM
    m0 = lo_half.astype(F32)
    m1 = 1.0 - m0
    mask_p, mask_a, tri_i = (msk_ref[i].astype(BF16) for i in range(3))
    head_diag, head_anti, eye = (msk_ref[i] for i in range(3, 6))
    ones_bd, avg_bd = bfm_ref[0], bfm_ref[1]
    cum = cum_ref[...]
    mu_l = cat([prow(0, _P_MU_WA), prow(0, _P_MU_G)], 1)
    cpg = group // CHUNK
    n_groups = rows // group
    csl = [slice(c * CHUNK, (c + 1) * CHUNK) for c in range(cpg)]
    zeros_q = jnp.zeros((CHUNK, LANES), BF16)

    def shifted(ref, scr, mu, g):
        z = ref[0, g * group:(g + 1) * group, :]
        prev_row = scr[7:8, :] if g == 0 else ref[0, g * group - 1:g * group, :]
        prev = jnp.where(row_id == 0, prev_row, pltpu.roll(z, 1, 0))
        return z + (prev - z) * mu

    def prep(g, out):
        r2 = shifted(r_ref, zr_scr, prow2(_P_MU_R), g)
        k2_ = shifted(k_ref, zk_scr, prow2(_P_MU_K), g)
        v2 = shifted(v_ref, zv_scr, prow2(_P_MU_V), g)
        yield
        zl = shifted(l_ref, zl_scr, mu_l, g)
        wa, gl = zl[:, :LANES], zl[:, LANES:]
        lora_in = jnp.where(lane < DECAY_LORA, jnp.tanh(wa), wa).astype(BF16)
        gate_in = (1.0 / (1.0 + jnp.exp(-gl))).astype(BF16)
        yield
        for q in pairs:
            r, k, v = r2[:, psl[q]], k2_[:, psl[q]], v2[:, psl[q]]
            lo = _dg(lora_in, wl_ref[q])
            gate = _dg(gate_in, g2_ref[:, psl[q]])
            kk = k * prow(q, _P_KK)
            n2 = _dg((kk * kk).astype(BF16), ones_bd)
            yield
            wpre = -(prow(q, _P_W0) + lo[:, :LANES])
            w = -(jnp.maximum(wpre, 0.0) + jnp.log(1.0 + jnp.exp(-jnp.abs(wpre)))) - 0.5
            logw = -jnp.exp(w)
            hi, lw = _split(logw)
            hl = cat([hi, lw], 1)
            lcs = [_dg(cum, hl[s]) for s in csl]
            yield
            a = 1.0 / (1.0 + jnp.exp(-(prow(q, _P_A0) + lo[:, LANES:])))
            kk = kk / jnp.maximum(jnp.sqrt(n2), 1e-12)
            k2 = k * (1.0 + (a - 1.0) * prow(q, _P_KA))
            lc = cat([x[:, :LANES] + x[:, LANES:] for x in lcs])
            e_in, e_neg = jnp.exp(lc), jnp.exp(-lc)
            yield
            rt_ = r * e_in
            at_ = -kk * jnp.exp(lc - logw)
            v_sw = pltpu.roll(v, HEAD_DIM, 1)
            kt_, bt_ = k2 * e_neg, kk * a * e_neg
            out.append(dict(q=q, r=r, k2=k2, v=v, gate=gate, e_in=e_in, rt=rt_, at=at_,
                            kt0=kt_ * m0, kt1=kt_ * m1, bt0=bt_ * m0, bt1=bt_ * m1,
                            vs0=(v_sw * m0).astype(BF16), vs1=(v_sw * m1).astype(BF16)))
            yield

    def phase1(pp, out):
        units = [(p_, s) for p_ in pp for s in csl]
        get = lambda n: [p_[n][s] for p_, s in units]
        at_, rt_, bt0, bt1, kt0, kt1, vs0, vs1 = (get(n) for n in ("at", "rt", "bt0", "bt1", "kt0", "kt1", "vs0", "vs1"))
        gcs = [p_["e_in"][s.stop - 1:s.stop, :] for p_, s in units]
        d_ab = [_mm1(cat([a_, r_]), cat([b0, k0, k1, b1]), _NT)
                for a_, r_, b0, k0, k1, b1 in zip(at_, rt_, bt0, kt0, kt1, bt1)]
        yield
        top = [cat([d[:CHUNK, :LANES], d[:CHUNK, LANES:]]).astype(BF16) for d in d_ab]
        bot = [cat([d[CHUNK:, :LANES], d[CHUNK:, LANES:]]).astype(BF16) * tri_i for d in d_ab]
        p = [tp * mask_p for tp in top]
        xb = [(cat([a_ * m0, a_ * m1]) + _dg(tp * mask_a, cat([w0, w1]))).astype(BF16)
              for a_, tp, w0, w1 in zip(at_, top, vs0, vs1)]
        yield
        for i in range(6):
            if i < 5:
                px = [_dg(pw, cat([x, pw], 1)) for x, pw in zip(xb, p)]
                p = [d[:, LANES:].astype(BF16) for d in px]
                xb = [x + d[:, :LANES].astype(BF16) for x, d in zip(xb, px)]
            else:
                xb = [x + _dg(pw, x).astype(BF16) for x, pw in zip(xb, p)]
            yield
        gst = [cat([x[:CHUNK], w1, w0, x[CHUNK:]]) for x, w0, w1 in zip(xb, vs0, vs1)]
        lhs45 = []
        for b_, b0, k0, k1, b1, gc in zip(bot, bt0, kt0, kt1, bt1, gcs):
            lhs45.append(cat([cat([b_[:CHUNK], zeros_q], 1), cat([zeros_q, b_[CHUNK:]], 1),
                              cat([b0 * gc, k0 * gc, k1 * gc, b1 * gc]).T.astype(BF16)]))
        o45 = [_dg(l_, g_) for l_, g_ in zip(lhs45, gst)]
        yield
        out["lhs"] = [cat([(r_ + jnp.where(lo_half, o[:CHUNK], o[CHUNK:LANES])).astype(BF16),
                           (o[LANES:] * head_diag + eye * gc).astype(BF16)])
                      for r_, o, gc in zip(rt_, o45, gcs)]
        out["y_in"] = [jnp.where(lo_half, o[CHUNK:LANES], o[:CHUNK]) for o in o45]
        out["g_c"] = [o[LANES:] * head_anti for o in o45]

    def chain_step(f, c, hs, ys):
        for q in pairs:
            u = q * cpg + c
            yh = _dg(f["lhs"][u], hs[q].astype(BF16))
            ys[q].append(yh[:CHUNK] + f["y_in"][u])
            hs[q] = yh[CHUNK:] + f["g_c"][u]

    def epilogue(g, pp, ys):
        gsl = slice(g * group, (g + 1) * group)
        for p_ in pp:
            q = p_["q"]
            y = pltpu.roll(cat(ys[q]), HEAD_DIM, 1)
            yc = y - _dg(y.astype(BF16), avg_bd)
            yv = _dg((yc * yc).astype(BF16), avg_bd)
            yn = yc * lax.rsqrt(yv + GN_EPS) * prow(q, _P_LNG) + prow(q, _P_LNB)
            bonus = _dg((p_["r"] * p_["k2"] * prow(q, _P_RK)).astype(BF16), ones_bd) * p_["v"]
            o_ref[0, gsl, psl[q]] = (yn + bonus) * p_["gate"] / (1.0 + jnp.exp(-gr_ref[0, gsl, psl[q]]))

    n_stages, n_pieces = 9, 2 + 4 * PAIRS_PER_STEP
    hs = [h_scr[q] for q in pairs]
    p_all = [[] for _ in range(n_groups)]
    f_all = [{} for _ in range(n_groups)]
    for _ in prep(0, p_all[0]):
        pass
    for g in range(n_groups):
        side = prep(g + 1, p_all[g + 1]) if g + 1 < n_groups else iter(())
        ys = [[] for _ in pairs]
        done = 0
        for i, _ in enumerate(phase1(p_all[g], f_all[g])):
            while done * n_stages < (i + 1) * n_pieces:
                next(side, None)
                done += 1
            if g > 0 and i % 2 == 1 and len(ys[0]) < cpg:
                chain_step(f_all[g - 1], len(ys[0]), hs, ys)
        for _ in side:
            pass
        if g > 0:
            while len(ys[0]) < cpg:
                chain_step(f_all[g - 1], len(ys[0]), hs, ys)
            epilogue(g - 1, p_all[g - 1], ys)
    ys = [[] for _ in pairs]
    for c in range(cpg):
        chain_step(f_all[n_groups - 1], c, hs, ys)
    epilogue(n_groups - 1, p_all[n_groups - 1], ys)
    for q in pairs:
        h_scr[q] = hs[q]
    zr_scr[...] = r_ref[0, rows - 8:rows, :]
    zk_scr[...] = k_ref[0, rows - 8:rows, :]
    zv_scr[...] = v_ref[0, rows - 8:rows, :]
    zl_scr[...] = l_ref[0, rows - 8:rows, :]

    @pl.when(t == pl.num_programs(2) - 1)
    def _():
        for q in pairs:
            s_ref[0, q] = hs[q]


def _rwkv(proj3, prev8, par, wl, g2, s0, *, rows):
    bsz, t_len, _ = proj3.shape
    masks, bfm, cum = _rwkv_consts()
    pps = PAIRS_PER_STEP
    wide = pps * LANES
    col = lambda off: (lambda b, p, t: (b, t, off // wide + p))
    pcol = lambda off: (lambda b, p, t: (0, off // wide + p))
    return pl.pallas_call(
        functools.partial(_rwkv_kernel, rows=rows, group=min(RWKV_GROUP, rows)),
        out_shape=(jax.ShapeDtypeStruct((bsz, t_len, D_MODEL), F32),
                   jax.ShapeDtypeStruct((bsz, N_PAIRS, LANES, LANES), F32)),
        grid=(bsz, N_PAIRS // pps, t_len // rows),
        in_specs=[
            pl.BlockSpec((1, rows, wide), col(OFF_R)),
            pl.BlockSpec((1, rows, wide), col(OFF_K)),
            pl.BlockSpec((1, rows, wide), col(OFF_V)),
            pl.BlockSpec((1, rows, 2 * LANES), lambda b, p, t: (b, t, OFF_L // (2 * LANES))),
            pl.BlockSpec((1, rows, wide), col(OFF_GR)),
            pl.BlockSpec((8, wide), pcol(OFF_R)),
            pl.BlockSpec((8, wide), pcol(OFF_K)),
            pl.BlockSpec((8, wide), pcol(OFF_V)),
            pl.BlockSpec((8, 2 * LANES), lambda b, p, t: (0, OFF_L // (2 * LANES))),
            pl.BlockSpec((pps, _P_ROWS, LANES), lambda b, p, t: (p, 0, 0)),
            pl.BlockSpec((pps, LANES, 2 * LANES), lambda b, p, t: (p, 0, 0)),
            pl.BlockSpec((GATE_LORA, wide), lambda b, p, t: (0, p)),
            pl.BlockSpec((pps, LANES, LANES), lambda b, p, t: (p, 0, 0)),
            _const_spec(masks.shape), _const_spec(bfm.shape), _const_spec(cum.shape),
        ],
        out_specs=(pl.BlockSpec((1, rows, wide), lambda b, p, t: (b, t, p)),
                   pl.BlockSpec((1, pps, LANES, LANES), lambda b, p, t: (b, p, 0, 0))),
        scratch_shapes=[pltpu.VMEM((pps, LANES, LANES), F32),
                        pltpu.VMEM((8, wide), F32), pltpu.VMEM((8, wide), F32),
                        pltpu.VMEM((8, wide), F32), pltpu.VMEM((8, 2 * LANES), F32)],
        compiler_params=pltpu.CompilerParams(
            dimension_semantics=("parallel", "parallel", "arbitrary"), vmem_limit_bytes=VMEM_LIMIT),
        name="rwkv",
    )(proj3, proj3, proj3, proj3, proj3, prev8, prev8, prev8, prev8, par, wl, g2, s0, masks, bfm, cum)


def _attn_kernel(sink_ref, qg_ref, kvc_ref, kvp_ref, kvm_ref, bf_ref, bg_ref, rw_ref, o_ref):
    first = pl.program_id(1) == 0
    lane = lax.broadcasted_iota(jnp.int32, (1, LANES), 1)
    lo_lane = lane < HEAD_DIM
    lo_row = lax.broadcasted_iota(jnp.int32, (LANES, 1), 0) < HEAD_DIM
    kv_w = 2 * LANES
    kc, vc = kvc_ref[0, :, :kv_w], kvc_ref[0, :, kv_w:]
    kv_prev = jnp.where(first, kvm_ref[...], kvp_ref[0])
    k_prev, v_prev = kv_prev[:, :kv_w], kv_prev[:, kv_w:]
    ppk = N_PAIRS // N_KV

    def scores(sb, kh):
        rs = slice(sb * BLOCK, (sb + 1) * BLOCK)
        kp_, vp_ = (k_prev, v_prev) if sb == 0 else (kc[rs.start - BLOCK:rs.start], vc[rs.start - BLOCK:rs.start])
        kx = jnp.concatenate([kp_, kc[rs]], axis=0)[:, kh * LANES:(kh + 1) * LANES].astype(BF16)
        vx = jnp.concatenate([vp_, vc[rs]], axis=0)[:, kh * LANES:(kh + 1) * LANES]
        lts = []
        for p in range(kh * ppk, (kh + 1) * ppk):
            qt = qg_ref[0, rs, p * LANES:(p + 1) * LANES]
            q2 = jnp.concatenate([jnp.where(lo_lane, qt, 0.0), jnp.where(lo_lane, 0.0, qt)], axis=0).astype(BF16)
            lts.append(_dg(kx, q2, _NT))
        vxt = jnp.concatenate([vx.T, jnp.ones((SUM_ROWS, 2 * BLOCK), F32)], axis=0).astype(BF16)
        return dict(sb=sb, kh=kh, rs=rs, lts=lts, vxt=vxt)

    def finish(g):
        bias_ref = bf_ref if g["sb"] == 0 else bg_ref
        pes, sinks = [], []
        for j, lt in enumerate(g["lts"]):
            for sub in range(2):
                hd = 2 * (g["kh"] * ppk + j) + sub
                lg = lt[:, sub * LANES:(sub + 1) * LANES] + bias_ref[0, hd]
                m = jnp.maximum(jnp.max(lg, axis=0, keepdims=True), sink_ref[hd])
                sinks.append(jnp.exp2(sink_ref[hd] - m))
                pes.append(jnp.exp2(lg - m).astype(BF16))
        ots = [_dg(g["vxt"], jnp.concatenate(pes[2 * j:2 * j + 2], axis=1)) for j in range(ppk)]
        for j, ot in enumerate(ots):
            cs = slice((g["kh"] * ppk + j) * LANES, (g["kh"] * ppk + j + 1) * LANES)
            inv0 = 1.0 / (ot[LANES:LANES + 1, :LANES] + sinks[2 * j])
            inv1 = 1.0 / (ot[LANES:LANES + 1, LANES:] + sinks[2 * j + 1])
            att = jnp.where(lo_row, ot[:LANES, :LANES] * inv0, ot[:LANES, LANES:] * inv1).T
            gate = 1.0 / (1.0 + jnp.exp(-qg_ref[0, g["rs"], D_MODEL + cs.start:D_MODEL + cs.stop]))
            o_ref[0, g["rs"], cs] = gate * att + rw_ref[0, g["rs"], cs]

    order = [(sb, kh) for sb in range(ATT_ROWS // BLOCK) for kh in range(N_KV)]
    cur = scores(*order[0])
    for nxt in order[1:]:
        ahead = scores(*nxt)
        finish(cur)
        cur = ahead
    finish(cur)


def _attn_merge(proj3, kvm, bias, sinks, rwg):
    bsz, t_len, _ = proj3.shape
    qg_w, kv_w = 2 * D_MODEL, 4 * LANES
    assert OFF_GA == OFF_Q + D_MODEL and OFF_VX == OFF_KX + kv_w // 2
    per = ATT_ROWS // BLOCK
    return pl.pallas_call(
        _attn_kernel,
        out_shape=jax.ShapeDtypeStruct((bsz, t_len, D_MODEL), F32),
        grid=(bsz, t_len // ATT_ROWS),
        in_specs=[
            pl.BlockSpec(memory_space=pltpu.SMEM),
            pl.BlockSpec((1, ATT_ROWS, qg_w), lambda b, n: (b, n, OFF_Q // qg_w)),
            pl.BlockSpec((1, ATT_ROWS, kv_w), lambda b, n: (b, n, OFF_KX // kv_w)),
            pl.BlockSpec((1, BLOCK, kv_w), lambda b, n: (b, jnp.maximum(n * per - 1, 0), OFF_KX // kv_w)),
            pl.BlockSpec((BLOCK, kv_w), lambda b, n: (0, OFF_KX // kv_w)),
            pl.BlockSpec((1, N_HEADS, 2 * BLOCK, BLOCK), lambda b, n: (jnp.minimum(n, 1), 0, 0, 0)),
            pl.BlockSpec((1, N_HEADS, 2 * BLOCK, BLOCK), lambda b, n: (1, 0, 0, 0)),
            pl.BlockSpec((1, ATT_ROWS, D_MODEL), lambda b, n: (b, n, 0)),
        ],
        out_specs=pl.BlockSpec((1, ATT_ROWS, D_MODEL), lambda b, n: (b, n, 0)),
        compiler_params=pltpu.CompilerParams(
            dimension_semantics=("parallel", "arbitrary"), vmem_limit_bytes=VMEM_LIMIT),
        name="attn_merge",
    )(sinks, proj3, proj3, proj3, kvm, bias, bias, rwg)


def _out_ffn_kernel(x_ref, mg_ref, ln_ref, wo_ref, w1_ref, w2_ref, o_ref, *, ff_tile, parts):
    ln = ln_ref[...]
    tm = x_ref.shape[0]
    rs = [slice(i * tm // parts, (i + 1) * tm // parts) for i in range(parts)]
    proj = [_dg(mg_ref[r, :].astype(BF16), wo_ref[...]) for r in rs]
    h1 = [_layer_norm(ALPHA * _layer_norm(x_ref[r, :], ln[0:1], ln[1:2]) + p, ln[2:3], ln[3:4]) for r, p in zip(rs, proj)]
    for r, h in zip(rs, h1):
        hb = h.astype(BF16)
        acc = ALPHA * h
        for j in range(D_FF // ff_tile):
            cs = slice(j * ff_tile, (j + 1) * ff_tile)
            u = jnp.maximum(_dg(hb, w1_ref[:, cs]), 0.0)
            acc = acc + _dg((u * u).astype(BF16), w2_ref[cs, :])
        o_ref[r, :] = _layer_norm(acc, ln[4:5], ln[5:6])


def _out_ffn(x2, merged2, ln_par, wo, w1, w2, *, tm, ff_tile=1024, parts=2):
    m = x2.shape[0]
    return pl.pallas_call(
        functools.partial(_out_ffn_kernel, ff_tile=ff_tile, parts=parts),
        out_shape=jax.ShapeDtypeStruct((m, D_MODEL), F32),
        grid=(m // tm,),
        in_specs=[
            pl.BlockSpec((tm, D_MODEL), lambda i: (i, 0)),
            pl.BlockSpec((tm, D_MODEL), lambda i: (i, 0)),
            _const_spec((8, D_MODEL)),
            _const_spec((D_MODEL, D_MODEL)),
            _const_spec((D_MODEL, D_FF)),
            _const_spec((D_FF, D_MODEL)),
        ],
        out_specs=pl.BlockSpec((tm, D_MODEL), lambda i: (i, 0)),
        compiler_params=pltpu.CompilerParams(
            dimension_semantics=("parallel",), vmem_limit_bytes=VMEM_LIMIT),
        name="out_ffn",
    )(x2, merged2, ln_par, wo, w1, w2)


def kernel(x, meta_tokens, ln0_g, ln0_b, rel_bias, w_in, shift_mu, attn_sinks, decay_w0, decay_w2, iclr_a0, iclr_a2, gate_w2, k_k, k_a, r_k, lnx_g, lnx_b, w_out, ln1_g, ln1_b, w_ff1, w_ff2, ln2_g, ln2_b):
    bsz, seq, _ = x.shape
    assert seq % RWKV_ROWS == 0 and D_MODEL == x.shape[2]
    W = D_MODEL
    wi = w_in[0]
    w_perm = wi.astype(BF16)
    assert w_perm.shape[1] == IN_COLS
    mu = shift_mu[0]
    rows_ = [mu[:W], mu[W:2 * W], mu[2 * W:3 * W]]
    vec = lambda a: a.reshape(N_PAIRS, 1, LANES)
    par = jnp.concatenate(
        [vec(rows_[0]), vec(rows_[1]), vec(rows_[2]),
         jnp.broadcast_to(mu[3 * W:3 * W + LANES].reshape(1, 1, LANES), (N_PAIRS, 1, LANES)),
         jnp.broadcast_to(mu[3 * W + LANES:].reshape(1, 1, LANES), (N_PAIRS, 1, LANES)),
         vec(decay_w0[0]), vec(iclr_a0[0]), vec(k_k[0]), vec(k_a[0]), vec(r_k[0].reshape(-1)),
         vec(lnx_g[0]), vec(lnx_b[0]),
         jnp.zeros((N_PAIRS, _P_ROWS - 12, LANES), F32)], axis=1).astype(F32)
    w2p = decay_w2[0].reshape(DECAY_LORA, N_PAIRS, LANES).transpose(1, 0, 2)
    a2p = iclr_a2[0].reshape(ICLR_LORA, N_PAIRS, LANES).transpose(1, 0, 2)
    z = jnp.zeros_like(w2p)
    wl = jnp.concatenate([jnp.concatenate([w2p, z], axis=2), jnp.concatenate([z, a2p], axis=2)], axis=1).astype(BF16)
    g2 = gate_w2[0].astype(BF16)
    ln_par = jnp.stack([ln0_g, ln0_b, ln1_g[0], ln1_b[0], ln2_g[0], ln2_b[0],
                        jnp.zeros_like(ln0_g), jnp.zeros_like(ln0_g)]).astype(F32)
    g0 = ln0_g.reshape(1, W).astype(F32)
    b0 = ln0_b.reshape(1, W).astype(F32)

    bias = _bias_table(rel_bias)

    meta_blk = jnp.concatenate([jnp.zeros((PAD_ROWS, W), F32), meta_tokens.astype(F32)], axis=0)
    proj_m = _ln_proj(meta_blk, g0, b0, w_perm, tm=BLOCK, n_pad=PAD_ROWS)
    zeros8 = jnp.zeros((8, P_COLS), F32)
    s_zero = jnp.zeros((N_PAIRS, LANES, LANES), F32)
    _, s_meta = _rwkv(proj_m[None], zeros8, par, wl, g2, s_zero, rows=BLOCK)
    prev8 = proj_m[BLOCK - 8:]

    x2 = x.reshape(bsz * seq, W)
    proj = _ln_proj(x2, g0, b0, w_perm, tm=512)
    proj3 = proj.reshape(bsz, seq, P_COLS)
    rwg, _ = _rwkv(proj3, prev8, par, wl, g2, s_meta[0], rows=RWKV_ROWS)
    merged = _attn_merge(proj3, proj_m, bias, attn_sinks[0].astype(F32) * LOG2E, rwg)
    out = _out_ffn(x2, merged.reshape(bsz * seq, W), ln_par, w_out[0].astype(BF16),
                   w_ff1[0].astype(BF16), w_ff2[0].astype(BF16), tm=512)
    return out.reshape(bsz, seq, W)
```

```python
import functools
import math

import numpy as np
import jax
import jax.numpy as jnp
from jax import lax
from jax.experimental import pallas as pl
from jax.experimental.pallas import tpu as pltpu

F32 = jnp.float32
BF16 = jnp.bfloat16

D_MODEL = 1024
N_META = 16
HEAD_DIM = 64
N_HEADS = D_MODEL // HEAD_DIM
N_KV = 2
BLOCK = 128
N_BUCKETS = 32
MAX_EXACT = 16
MAX_DISTANCE = 128
DECAY_LORA = 64
ICLR_LORA = 64
GATE_LORA = 128
D_FF = 4 * D_MODEL
LN_EPS = 1e-5
GN_EPS = 1e-5 * HEAD_DIM
DEPTH = 1
ALPHA = (2.0 * DEPTH) ** 0.25
LOG2E = math.log2(math.e)

LANES = 128
N_PAIRS = D_MODEL // LANES
CHUNK = 64
RWKV_ROWS = 2048
RWKV_GROUP = 256
PAIRS_PER_STEP = 2
PAD_ROWS = BLOCK - N_META
ATT_ROWS = 2 * BLOCK
SUM_ROWS = 16

B_Q = 0
B_KX = 1024
B_VX = 1280
B_COLS = 1536
F_GA = 0
F_R = 1024
F_K = 2048
F_V = 3072
F_L = 4096
F_GR = 4352
F_COLS = 5376

IN_Q = 0
IN_K = 1024
IN_ZR = 1280
IN_GA = 4608
IN_GR = 5632
IN_COLS = 6656
_PROJ_SEGMENTS = ((IN_GA, D_MODEL, F_GA), (IN_ZR, F_GR - F_R, F_R), (IN_GR, D_MODEL, F_GR))
Q_SCALE = HEAD_DIM ** -0.5 * LOG2E

VMEM_LIMIT = 56 * 1024 * 1024

_NN = (((1,), (0,)), ((), ()))
_NT = (((1,), (1,)), ((), ()))


def _dg(a, b, dims=_NN):
    return lax.dot_general(a, b, dims, preferred_element_type=F32)


def _split(x):
    hi = x.astype(BF16)
    lo = (x - hi.astype(F32)).astype(BF16)
    return hi, lo


def _mm1(a, b, dims=_NN):
    return _dg(a.astype(BF16), b.astype(BF16), dims)


def _layer_norm(x, g, b):
    mu = jnp.mean(x, axis=-1, keepdims=True)
    xc = x - mu
    var = jnp.mean(xc * xc, axis=-1, keepdims=True)
    return xc * lax.rsqrt(var + LN_EPS) * g + b


def _const_spec(shape):
    return pl.BlockSpec(shape, lambda *_: (0,) * len(shape), pipeline_mode=pl.Buffered(1))


def _bias_kernel(bucket_ref, rel_ref, o_ref):
    first, bk = bucket_ref[0], bucket_ref[1]
    for h in range(N_HEADS):
        acc = jnp.where(bk < 0, -jnp.inf, 0.0).astype(F32)
        for b in range(N_BUCKETS):
            acc = jnp.where(bk == b, rel_ref[b, h] * LOG2E, acc)
        o_ref[1, h] = acc
        o_ref[0, h] = jnp.where(first < 0, -jnp.inf, acc)


def _bucket_table():
    q = np.arange(BLOCK)[:, None]
    s = np.arange(2 * BLOCK)[None, :]
    dist = q + BLOCK - s
    in_window = (dist >= 0) & (dist < BLOCK)
    d0 = np.maximum(dist, 0)
    d = np.maximum(d0, 1).astype(np.float32)
    large = MAX_EXACT + (np.log(d / np.float32(MAX_EXACT)) / np.float32(math.log(MAX_DISTANCE / MAX_EXACT))
                         * (N_BUCKETS - MAX_EXACT)).astype(np.int32)
    large = np.minimum(large, N_BUCKETS - 1)
    bucket = np.where(d0 < MAX_EXACT, d0, large).astype(np.int32)
    general = np.where(in_window, bucket, -1)
    first = np.where(in_window & (s >= PAD_ROWS), bucket, -1)
    return np.stack([first.T, general.T]).astype(np.int32)


def _bias_table(rel_bias):
    return pl.pallas_call(
        _bias_kernel,
        out_shape=jax.ShapeDtypeStruct((2, N_HEADS, 2 * BLOCK, BLOCK), F32),
        in_specs=[pl.BlockSpec(memory_space=pltpu.VMEM), pl.BlockSpec(memory_space=pltpu.SMEM)],
        out_specs=pl.BlockSpec(memory_space=pltpu.VMEM),
        name="bias_table",
    )(jnp.asarray(_bucket_table()), rel_bias.astype(F32))


def _ln_proj_kernel(x_ref, g_ref, b_ref, w_ref, o16_ref, o32_ref, *, n_pad, tn):
    y = _layer_norm(x_ref[...], g_ref[...], b_ref[...])
    if n_pad:
        row = lax.broadcasted_iota(jnp.int32, y.shape, 0)
        y = jnp.where(row < n_pad, 0.0, y)
    yb = y.astype(BF16)
    lo_lane = lax.broadcasted_iota(jnp.int32, (1, LANES), 1) < HEAD_DIM

    def dup(z):
        zr = pltpu.roll(z, HEAD_DIM, 1)
        return jnp.concatenate([jnp.where(lo_lane, z, zr), jnp.where(lo_lane, zr, z)], axis=1)

    for j in range(0, D_MODEL, tn):
        o16_ref[:, B_Q + j:B_Q + j + tn] = (_dg(yb, w_ref[:, IN_Q + j:IN_Q + j + tn]) * Q_SCALE).astype(BF16)
    kv = _dg(yb, w_ref[:, IN_K:IN_K + 2 * LANES])
    o16_ref[:, B_KX:B_KX + 2 * LANES] = dup(kv[:, :LANES]).astype(BF16)
    o16_ref[:, B_VX:B_VX + 2 * LANES] = dup(kv[:, LANES:]).astype(BF16)
    for src, width, dst in _PROJ_SEGMENTS:
        for j in range(0, width, tn):
            w_ = min(tn, width - j)
            o32_ref[:, dst + j:dst + j + w_] = _dg(yb, w_ref[:, src + j:src + j + w_])


def _ln_proj(x2, g, b, w, *, tm, tn=512, n_pad=0):
    m = x2.shape[0]
    return pl.pallas_call(
        functools.partial(_ln_proj_kernel, n_pad=n_pad, tn=tn),
        out_shape=(jax.ShapeDtypeStruct((m, B_COLS), BF16), jax.ShapeDtypeStruct((m, F_COLS), F32)),
        grid=(m // tm,),
        in_specs=[
            pl.BlockSpec((tm, D_MODEL), lambda i: (i, 0)),
            _const_spec((1, D_MODEL)),
            _const_spec((1, D_MODEL)),
            _const_spec((D_MODEL, IN_COLS)),
        ],
        out_specs=(pl.BlockSpec((tm, B_COLS), lambda i: (i, 0)), pl.BlockSpec((tm, F_COLS), lambda i: (i, 0))),
        compiler_params=pltpu.CompilerParams(
            dimension_semantics=("parallel",), vmem_limit_bytes=VMEM_LIMIT),
        name="ln_proj",
    )(x2, g, b, w)


(_P_MU_R, _P_MU_K, _P_MU_V, _P_MU_WA, _P_MU_G, _P_W0, _P_A0, _P_KK, _P_KA, _P_RK, _P_LNG, _P_LNB) = range(12)
_P_ROWS = 16


def _rwkv_consts():
    r2, c2 = np.indices((LANES, LANES))
    same = (r2 < HEAD_DIM) == (c2 < HEAD_DIM)
    low = (r2 % HEAD_DIM) > (c2 % HEAD_DIM)
    masks = np.stack([low & same,
                      low & ~same,
                      (r2 % HEAD_DIM) >= (c2 % HEAD_DIM),
                      same, ~same, r2 == c2]).astype(np.float32)
    bf = np.stack([same.astype(np.float32), same.astype(np.float32) / HEAD_DIM])
    t, s = np.indices((CHUNK, CHUNK))
    return jnp.asarray(masks), jnp.asarray(bf, dtype=BF16), jnp.asarray((t >= s).astype(np.float32), dtype=BF16)


def _rwkv_kernel(r_ref, k_ref, v_ref, l_ref, gr_ref, pr_ref, pk_ref, pv_ref, plr_ref,
                 par_ref, wl_ref, g2_ref, s0_ref, msk_ref, bfm_ref, cum_ref, o_ref, s_ref,
                 h_scr, zr_scr, zk_scr, zv_scr, zl_scr, *, rows, group):
    t = pl.program_id(2)

    @pl.when(t == 0)
    def _():
        h_scr[...] = s0_ref[...]
        zr_scr[...] = pr_ref[...]
        zk_scr[...] = pk_ref[...]
        zv_scr[...] = pv_ref[...]
        zl_scr[...] = plr_ref[...]

    cat = lambda parts, axis=0: jnp.concatenate(parts, axis=axis)
    pairs = range(PAIRS_PER_STEP)
    prow = lambda q, i: par_ref[q, i:i + 1, :]
    prow2 = lambda i: cat([prow(q, i) for q in pairs], 1)
    psl = [slice(q * LANES, (q + 1) * LANES) for q in pairs]
    row_id = lax.broadcasted_iota(jnp.int32, (group, 1), 0)
    lane = lax.broadcasted_iota(jnp.int32, (1, LANES), 1)
    lo_half = lane < HEAD_DIM
    m0 = lo_half.astype(F32)
    m1 = 1.0 - m0
    mask_p, mask_a, tri_i = (msk_ref[i].astype(BF16) for i in range(3))
    head_diag, head_anti, eye = (msk_ref[i] for i in range(3, 6))
    ones_bd, avg_bd = bfm_ref[0], bfm_ref[1]
    cum = cum_ref[...]
    mu_l = cat([prow(0, _P_MU_WA), prow(0, _P_MU_G)], 1)
    cpg = group // CHUNK
    n_groups = rows // group
    csl = [slice(c * CHUNK, (c + 1) * CHUNK) for c in range(cpg)]
    zeros_q = jnp.zeros((CHUNK, LANES), BF16)

    def shifted(ref, scr, mu, g):
        z = ref[0, g * group:(g + 1) * group, :]
        prev_row = scr[7:8, :] if g == 0 else ref[0, g * group - 1:g * group, :]
        prev = jnp.where(row_id == 0, prev_row, pltpu.roll(z, 1, 0))
        return z + (prev - z) * mu

    def prep(g, out):
        r2 = shifted(r_ref, zr_scr, prow2(_P_MU_R), g)
        k2_ = shifted(k_ref, zk_scr, prow2(_P_MU_K), g)
        v2 = shifted(v_ref, zv_scr, prow2(_P_MU_V), g)
        yield
        zl = shifted(l_ref, zl_scr, mu_l, g)
        wa, gl = zl[:, :LANES], zl[:, LANES:]
        lora_in = jnp.where(lane < DECAY_LORA, jnp.tanh(wa), wa).astype(BF16)
        gate_in = (1.0 / (1.0 + jnp.exp(-gl))).astype(BF16)
        yield
        for q in pairs:
            r, k, v = r2[:, psl[q]], k2_[:, psl[q]], v2[:, psl[q]]
            lo = _dg(lora_in, wl_ref[q])
            gate = _dg(gate_in, g2_ref[:, psl[q]])
            kk = k * prow(q, _P_KK)
            n2 = _dg((kk * kk).astype(BF16), ones_bd)
            yield
            wpre = -(prow(q, _P_W0) + lo[:, :LANES])
            w = -(jnp.maximum(wpre, 0.0) + jnp.log(1.0 + jnp.exp(-jnp.abs(wpre)))) - 0.5
            logw = -jnp.exp(w)
            hi, lw = _split(logw)
            hl = cat([hi, lw], 1)
            lcs = [_dg(cum, hl[s]) for s in csl]
            yield
            a = 1.0 / (1.0 + jnp.exp(-(prow(q, _P_A0) + lo[:, LANES:])))
            kk = kk / jnp.maximum(jnp.sqrt(n2), 1e-12)
            k2 = k * (1.0 + (a - 1.0) * prow(q, _P_KA))
            lc = cat([x[:, :LANES] + x[:, LANES:] for x in lcs])
            e_in, e_neg = jnp.exp(lc), jnp.exp(-lc)
            yield
            rt_ = r * e_in
            at_ = -kk * jnp.exp(lc - logw)
            v_sw = pltpu.roll(v, HEAD_DIM, 1)
            kt_, bt_ = k2 * e_neg, kk * a * e_neg
            out.append(dict(q=q, r=r, k2=k2, v=v, gate=gate, e_in=e_in, rt=rt_, at=at_,
                            kt0=kt_ * m0, kt1=kt_ * m1, bt0=bt_ * m0, bt1=bt_ * m1,
                            vs0=(v_sw * m0).astype(BF16), vs1=(v_sw * m1).astype(BF16)))
            yield

    def phase1(pp, out):
        units = [(p_, s) for p_ in pp for s in csl]
        get = lambda n: [p_[n][s] for p_, s in units]
        at_, rt_, bt0, bt1, kt0, kt1, vs0, vs1 = (get(n) for n in ("at", "rt", "bt0", "bt1", "kt0", "kt1", "vs0", "vs1"))
        gcs = [p_["e_in"][s.stop - 1:s.stop, :] for p_, s in units]
        d_ab = [_mm1(cat([a_, r_]), cat([b0, k0, k1, b1]), _NT)
                for a_, r_, b0, k0, k1, b1 in zip(at_, rt_, bt0, kt0, kt1, bt1)]
        yield
        top = [cat([d[:CHUNK, :LANES], d[:CHUNK, LANES:]]).astype(BF16) for d in d_ab]
        bot = [cat([d[CHUNK:, :LANES], d[CHUNK:, LANES:]]).astype(BF16) * tri_i for d in d_ab]
        p = [tp * mask_p for tp in top]
        xb = [(cat([a_ * m0, a_ * m1]) + _dg(tp * mask_a, cat([w0, w1]))).astype(BF16)
              for a_, tp, w0, w1 in zip(at_, top, vs0, vs1)]
        yield
        for i in range(6):
            if i < 5:
                px = [_dg(pw, cat([x, pw], 1)) for x, pw in zip(xb, p)]
                p = [d[:, LANES:].astype(BF16) for d in px]
                xb = [x + d[:, :LANES].astype(BF16) for x, d in zip(xb, px)]
            else:
                xb = [x + _dg(pw, x).astype(BF16) for x, pw in zip(xb, p)]
            yield
        gst = [cat([x[:CHUNK], w1, w0, x[CHUNK:]]) for x, w0, w1 in zip(xb, vs0, vs1)]
        lhs45 = []
        for b_, b0, k0, k1, b1, gc in zip(bot, bt0, kt0, kt1, bt1, gcs):
            lhs45.append(cat([cat([b_[:CHUNK], zeros_q], 1), cat([zeros_q, b_[CHUNK:]], 1),
                              cat([b0 * gc, k0 * gc, k1 * gc, b1 * gc]).T.astype(BF16)]))
        o45 = [_dg(l_, g_) for l_, g_ in zip(lhs45, gst)]
        yield
        out["lhs"] = [cat([(r_ + jnp.where(lo_half, o[:CHUNK], o[CHUNK:LANES])).astype(BF16),
                           (o[LANES:] * head_diag + eye * gc).astype(BF16)])
                      for r_, o, gc in zip(rt_, o45, gcs)]
        out["y_in"] = [jnp.where(lo_half, o[CHUNK:LANES], o[:CHUNK]) for o in o45]
        out["g_c"] = [o[LANES:] * head_anti for o in o45]

    def chain_step(f, c, hs, ys):
        for q in pairs:
            u = q * cpg + c
            yh = _dg(f["lhs"][u], hs[q].astype(BF16))
            ys[q].append(yh[:CHUNK] + f["y_in"][u])
            hs[q] = yh[CHUNK:] + f["g_c"][u]

    def epilogue(g, pp, ys):
        gsl = slice(g * group, (g + 1) * group)
        for p_ in pp:
            q = p_["q"]
            y = pltpu.roll(cat(ys[q]), HEAD_DIM, 1)
            yc = y - _dg(y.astype(BF16), avg_bd)
            yv = _dg((yc * yc).astype(BF16), avg_bd)
            yn = yc * lax.rsqrt(yv + GN_EPS) * prow(q, _P_LNG) + prow(q, _P_LNB)
            bonus = _dg((p_["r"] * p_["k2"] * prow(q, _P_RK)).astype(BF16), ones_bd) * p_["v"]
            o_ref[0, gsl, psl[q]] = (yn + bonus) * p_["gate"] / (1.0 + jnp.exp(-gr_ref[0, gsl, psl[q]]))

    n_stages, n_pieces = 9, 2 + 4 * PAIRS_PER_STEP
    hs = [h_scr[q] for q in pairs]
    p_all = [[] for _ in range(n_groups)]
    f_all = [{} for _ in range(n_groups)]
    for _ in prep(0, p_all[0]):
        pass
    for g in range(n_groups):
        side = prep(g + 1, p_all[g + 1]) if g + 1 < n_groups else iter(())
        ys = [[] for _ in pairs]
        done = 0
        for i, _ in enumerate(phase1(p_all[g], f_all[g])):
            while done * n_stages < (i + 1) * n_pieces:
                next(side, None)
                done += 1
            if g > 0 and i % 2 == 1 and len(ys[0]) < cpg:
                chain_step(f_all[g - 1], len(ys[0]), hs, ys)
        for _ in side:
            pass
        if g > 0:
            while len(ys[0]) < cpg:
                chain_step(f_all[g - 1], len(ys[0]), hs, ys)
            epilogue(g - 1, p_all[g - 1], ys)
    ys = [[] for _ in pairs]
    for c in range(cpg):
        chain_step(f_all[n_groups - 1], c, hs, ys)
    epilogue(n_groups - 1, p_all[n_groups - 1], ys)
    for q in pairs:
        h_scr[q] = hs[q]
    zr_scr[...] = r_ref[0, rows - 8:rows, :]
    zk_scr[...] = k_ref[0, rows - 8:rows, :]
    zv_scr[...] = v_ref[0, rows - 8:rows, :]
    zl_scr[...] = l_ref[0, rows - 8:rows, :]

    @pl.when(t == pl.num_programs(2) - 1)
    def _():
        for q in pairs:
            s_ref[0, q] = hs[q]


def _rwkv(proj3, prev8, par, wl, g2, s0, *, rows):
    bsz, t_len, _ = proj3.shape
    masks, bfm, cum = _rwkv_consts()
    pps = PAIRS_PER_STEP
    wide = pps * LANES
    col = lambda off: (lambda b, p, t: (b, t, off // wide + p))
    pcol = lambda off: (lambda b, p, t: (0, off // wide + p))
    return pl.pallas_call(
        functools.partial(_rwkv_kernel, rows=rows, group=min(RWKV_GROUP, rows)),
        out_shape=(jax.ShapeDtypeStruct((bsz, t_len, D_MODEL), F32),
                   jax.ShapeDtypeStruct((bsz, N_PAIRS, LANES, LANES), F32)),
        grid=(bsz, N_PAIRS // pps, t_len // rows),
        in_specs=[
            pl.BlockSpec((1, rows, wide), col(F_R)),
            pl.BlockSpec((1, rows, wide), col(F_K)),
            pl.BlockSpec((1, rows, wide), col(F_V)),
            pl.BlockSpec((1, rows, 2 * LANES), lambda b, p, t: (b, t, F_L // (2 * LANES))),
            pl.BlockSpec((1, rows, wide), col(F_GR)),
            pl.BlockSpec((8, wide), pcol(F_R)),
            pl.BlockSpec((8, wide), pcol(F_K)),
            pl.BlockSpec((8, wide), pcol(F_V)),
            pl.BlockSpec((8, 2 * LANES), lambda b, p, t: (0, F_L // (2 * LANES))),
            pl.BlockSpec((pps, _P_ROWS, LANES), lambda b, p, t: (p, 0, 0)),
            pl.BlockSpec((pps, LANES, 2 * LANES), lambda b, p, t: (p, 0, 0)),
            pl.BlockSpec((GATE_LORA, wide), lambda b, p, t: (0, p)),
            pl.BlockSpec((pps, LANES, LANES), lambda b, p, t: (p, 0, 0)),
            _const_spec(masks.shape), _const_spec(bfm.shape), _const_spec(cum.shape),
        ],
        out_specs=(pl.BlockSpec((1, rows, wide), lambda b, p, t: (b, t, p)),
                   pl.BlockSpec((1, pps, LANES, LANES), lambda b, p, t: (b, p, 0, 0))),
        scratch_shapes=[pltpu.VMEM((pps, LANES, LANES), F32),
                        pltpu.VMEM((8, wide), F32), pltpu.VMEM((8, wide), F32),
                        pltpu.VMEM((8, wide), F32), pltpu.VMEM((8, 2 * LANES), F32)],
        compiler_params=pltpu.CompilerParams(
            dimension_semantics=("parallel", "parallel", "arbitrary"), vmem_limit_bytes=VMEM_LIMIT),
        name="rwkv",
    )(proj3, proj3, proj3, proj3, proj3, prev8, prev8, prev8, prev8, par, wl, g2, s0, masks, bfm, cum)


def _attn_kernel(sink_ref, q_ref, ga_ref, kvc_ref, kvp_ref, kvm_ref, bf_ref, bg_ref, rw_ref, o_ref):
    first = pl.program_id(1) == 0
    lane = lax.broadcasted_iota(jnp.int32, (1, LANES), 1)
    lo_lane = lane < HEAD_DIM
    lo_row = lax.broadcasted_iota(jnp.int32, (LANES, 1), 0) < HEAD_DIM
    kv_w = 2 * LANES
    kc, vc = kvc_ref[0, :, :kv_w], kvc_ref[0, :, kv_w:]
    kv_prev = jnp.where(first, kvm_ref[...], kvp_ref[0])
    k_prev, v_prev = kv_prev[:, :kv_w], kv_prev[:, kv_w:]
    ppk = N_PAIRS // N_KV

    def scores(sb, kh):
        rs = slice(sb * BLOCK, (sb + 1) * BLOCK)
        kp_, vp_ = (k_prev, v_prev) if sb == 0 else (kc[rs.start - BLOCK:rs.start], vc[rs.start - BLOCK:rs.start])
        kx = jnp.concatenate([kp_, kc[rs]], axis=0)[:, kh * LANES:(kh + 1) * LANES]
        vx = jnp.concatenate([vp_, vc[rs]], axis=0)[:, kh * LANES:(kh + 1) * LANES].astype(F32)
        lts = []
        for p in range(kh * ppk, (kh + 1) * ppk):
            qt = q_ref[0, rs, p * LANES:(p + 1) * LANES]
            zq = jnp.zeros_like(qt)
            q2 = jnp.concatenate([jnp.where(lo_lane, qt, zq), jnp.where(lo_lane, zq, qt)], axis=0)
            lts.append(_dg(kx, q2, _NT))
        vxt = jnp.concatenate([vx.T, jnp.ones((SUM_ROWS, 2 * BLOCK), F32)], axis=0).astype(BF16)
        return dict(sb=sb, kh=kh, rs=rs, lts=lts, vxt=vxt)

    def finish(g):
        bias_ref = bf_ref if g["sb"] == 0 else bg_ref
        pes, sinks = [], []
        for j, lt in enumerate(g["lts"]):
            for sub in range(2):
                hd = 2 * (g["kh"] * ppk + j) + sub
                lg = lt[:, sub * LANES:(sub + 1) * LANES] + bias_ref[0, hd]
                m = jnp.maximum(jnp.max(lg, axis=0, keepdims=True), sink_ref[hd])
                sinks.append(jnp.exp2(sink_ref[hd] - m))
                pes.append(jnp.exp2(lg - m).astype(BF16))
        ots = [_dg(g["vxt"], jnp.concatenate(pes[2 * j:2 * j + 2], axis=1)) for j in range(ppk)]
        for j, ot in enumerate(ots):
            cs = slice((g["kh"] * ppk + j) * LANES, (g["kh"] * ppk + j + 1) * LANES)
            inv0 = 1.0 / (ot[LANES:LANES + 1, :LANES] + sinks[2 * j])
            inv1 = 1.0 / (ot[LANES:LANES + 1, LANES:] + sinks[2 * j + 1])
            att = jnp.where(lo_row, ot[:LANES, :LANES] * inv0, ot[:LANES, LANES:] * inv1).T
            gate = 1.0 / (1.0 + jnp.exp(-ga_ref[0, g["rs"], cs]))
            o_ref[0, g["rs"], cs] = (gate * att + rw_ref[0, g["rs"], cs]).astype(BF16)

    order = [(sb, kh) for sb in range(ATT_ROWS // BLOCK) for kh in range(N_KV)]
    cur = scores(*order[0])
    for nxt in order[1:]:
        ahead = scores(*nxt)
        finish(cur)
        cur = ahead
    finish(cur)


def _attn_merge(proj16, proj32, kvm, bias, sinks, rwg):
    bsz, t_len, _ = proj16.shape
    kv_w = 4 * LANES
    assert B_VX == B_KX + kv_w // 2 and B_KX % kv_w == 0
    per = ATT_ROWS // BLOCK
    rows = lambda w, col: pl.BlockSpec((1, ATT_ROWS, w), lambda b, n: (b, n, col // w))
    return pl.pallas_call(
        _attn_kernel,
        out_shape=jax.ShapeDtypeStruct((bsz, t_len, D_MODEL), BF16),
        grid=(bsz, t_len // ATT_ROWS),
        in_specs=[
            pl.BlockSpec(memory_space=pltpu.SMEM),
            rows(D_MODEL, B_Q), rows(D_MODEL, F_GA), rows(kv_w, B_KX),
            pl.BlockSpec((1, BLOCK, kv_w), lambda b, n: (b, jnp.maximum(n * per - 1, 0), B_KX // kv_w)),
            pl.BlockSpec((BLOCK, kv_w), lambda b, n: (0, B_KX // kv_w)),
            pl.BlockSpec((1, N_HEADS, 2 * BLOCK, BLOCK), lambda b, n: (jnp.minimum(n, 1), 0, 0, 0)),
            pl.BlockSpec((1, N_HEADS, 2 * BLOCK, BLOCK), lambda b, n: (1, 0, 0, 0)),
            rows(D_MODEL, 0),
        ],
        out_specs=rows(D_MODEL, 0),
        compiler_params=pltpu.CompilerParams(
            dimension_semantics=("parallel", "arbitrary"), vmem_limit_bytes=VMEM_LIMIT),
        name="attn_merge",
    )(sinks, proj16, proj32, proj16, proj16, kvm, bias, bias, rwg)


def _out_ffn_kernel(x_ref, mg_ref, ln_ref, wo_ref, w1_ref, w2_ref, o_ref, *, ff_tile, parts):
    ln = ln_ref[...]
    tm = x_ref.shape[0]
    rs = [slice(i * tm // parts, (i + 1) * tm // parts) for i in range(parts)]
    proj = [_dg(mg_ref[r, :], wo_ref[...]) for r in rs]
    h1 = [_layer_norm(ALPHA * _layer_norm(x_ref[r, :], ln[0:1], ln[1:2]) + p, ln[2:3], ln[3:4]) for r, p in zip(rs, proj)]
    for r, h in zip(rs, h1):
        hb = h.astype(BF16)
        acc = ALPHA * h
        for j in range(D_FF // ff_tile):
            cs = slice(j * ff_tile, (j + 1) * ff_tile)
            u = jnp.maximum(_dg(hb, w1_ref[:, cs]), 0.0)
            acc = acc + _dg((u * u).astype(BF16), w2_ref[cs, :])
        o_ref[r, :] = _layer_norm(acc, ln[4:5], ln[5:6])


def _out_ffn(x2, merged2, ln_par, wo, w1, w2, *, tm, ff_tile=1024, parts=2):
    m = x2.shape[0]
    return pl.pallas_call(
        functools.partial(_out_ffn_kernel, ff_tile=ff_tile, parts=parts),
        out_shape=jax.ShapeDtypeStruct((m, D_MODEL), F32),
        grid=(m // tm,),
        in_specs=[
            pl.BlockSpec((tm, D_MODEL), lambda i: (i, 0)),
            pl.BlockSpec((tm, D_MODEL), lambda i: (i, 0)),
            _const_spec((8, D_MODEL)),
            _const_spec((D_MODEL, D_MODEL)),
            _const_spec((D_MODEL, D_FF)),
            _const_spec((D_FF, D_MODEL)),
        ],
        out_specs=pl.BlockSpec((tm, D_MODEL), lambda i: (i, 0)),
        compiler_params=pltpu.CompilerParams(
            dimension_semantics=("parallel",), vmem_limit_bytes=VMEM_LIMIT),
        name="out_ffn",
    )(x2, merged2, ln_par, wo, w1, w2)


def kernel(x, meta_tokens, ln0_g, ln0_b, rel_bias, w_in, shift_mu, attn_sinks, decay_w0, decay_w2, iclr_a0, iclr_a2, gate_w2, k_k, k_a, r_k, lnx_g, lnx_b, w_out, ln1_g, ln1_b, w_ff1, w_ff2, ln2_g, ln2_b):
    bsz, seq, _ = x.shape
    assert seq % RWKV_ROWS == 0 and D_MODEL == x.shape[2]
    W = D_MODEL
    wi = w_in[0]
    w_perm = wi.astype(BF16)
    assert w_perm.shape[1] == IN_COLS
    mu = shift_mu[0]
    rows_ = [mu[:W], mu[W:2 * W], mu[2 * W:3 * W]]
    vec = lambda a: a.reshape(N_PAIRS, 1, LANES)
    par = jnp.concatenate(
        [vec(rows_[0]), vec(rows_[1]), vec(rows_[2]),
         jnp.broadcast_to(mu[3 * W:3 * W + LANES].reshape(1, 1, LANES), (N_PAIRS, 1, LANES)),
         jnp.broadcast_to(mu[3 * W + LANES:].reshape(1, 1, LANES), (N_PAIRS, 1, LANES)),
         vec(decay_w0[0]), vec(iclr_a0[0]), vec(k_k[0]), vec(k_a[0]), vec(r_k[0].reshape(-1)),
         vec(lnx_g[0]), vec(lnx_b[0]),
         jnp.zeros((N_PAIRS, _P_ROWS - 12, LANES), F32)], axis=1).astype(F32)
    w2p = decay_w2[0].reshape(DECAY_LORA, N_PAIRS, LANES).transpose(1, 0, 2)
    a2p = iclr_a2[0].reshape(ICLR_LORA, N_PAIRS, LANES).transpose(1, 0, 2)
    z = jnp.zeros_like(w2p)
    wl = jnp.concatenate([jnp.concatenate([w2p, z], axis=2), jnp.concatenate([z, a2p], axis=2)], axis=1).astype(BF16)
    g2 = gate_w2[0].astype(BF16)
    ln_par = jnp.stack([ln0_g, ln0_b, ln1_g[0], ln1_b[0], ln2_g[0], ln2_b[0],
                        jnp.zeros_like(ln0_g), jnp.zeros_like(ln0_g)]).astype(F32)
    g0 = ln0_g.reshape(1, W).astype(F32)
    b0 = ln0_b.reshape(1, W).astype(F32)

    bias = _bias_table(rel_bias)

    meta_blk = jnp.concatenate([jnp.zeros((PAD_ROWS, W), F32), meta_tokens.astype(F32)], axis=0)
    proj_m16, proj_m32 = _ln_proj(meta_blk, g0, b0, w_perm, tm=BLOCK, n_pad=PAD_ROWS)
    zeros8 = jnp.zeros((8, F_COLS), F32)
    s_zero = jnp.zeros((N_PAIRS, LANES, LANES), F32)
    _, s_meta = _rwkv(proj_m32[None], zeros8, par, wl, g2, s_zero, rows=BLOCK)
    prev8 = proj_m32[BLOCK - 8:]

    x2 = x.reshape(bsz * seq, W)
    proj16, proj32 = _ln_proj(x2, g0, b0, w_perm, tm=512)
    proj32 = proj32.reshape(bsz, seq, F_COLS)
    rwg, _ = _rwkv(proj32, prev8, par, wl, g2, s_meta[0], rows=RWKV_ROWS)
    merged = _attn_merge(proj16.reshape(bsz, seq, B_COLS), proj32, proj_m16, bias,
                         attn_sinks[0].astype(F32) * LOG2E, rwg)
    out = _out_ffn(x2, merged.reshape(bsz * seq, W), ln_par, w_out[0].astype(BF16),
                   w_ff1[0].astype(BF16), w_ff2[0].astype(BF16), tm=512)
    return out.reshape(bsz, seq, W)
```

```python
import functools
import math

import numpy as np
import jax
import jax.numpy as jnp
from jax import lax
from jax.experimental import pallas as pl
from jax.experimental.pallas import tpu as pltpu

F32 = jnp.float32
BF16 = jnp.bfloat16

D_MODEL = 1024
N_META = 16
HEAD_DIM = 64
N_HEADS = D_MODEL // HEAD_DIM
N_KV = 2
BLOCK = 128
N_BUCKETS = 32
MAX_EXACT = 16
MAX_DISTANCE = 128
DECAY_LORA = 64
ICLR_LORA = 64
GATE_LORA = 128
D_FF = 4 * D_MODEL
LN_EPS = 1e-5
GN_EPS = 1e-5 * HEAD_DIM
DEPTH = 1
ALPHA = (2.0 * DEPTH) ** 0.25
LOG2E = math.log2(math.e)

LANES = 128
SUBLANES = 8
N_PAIRS = D_MODEL // LANES
CHUNK = 64
RWKV_ROWS = 2048
RWKV_GROUP = 256
PAIRS_PER_STEP = 2
PROJ_ROWS = 512
FFN_ROWS = 512
PAD_ROWS = BLOCK - N_META
ATT_ROWS = 4 * BLOCK
SUM_ROWS = 16

B_Q = 0
B_KX = 1024
B_VX = 1280
B_COLS = 1536
F_GA = 0
F_R = 1024
F_K = 2048
F_V = 3072
F_L = 4096
F_GR = 4352
F_COLS = 5376

IN_Q = 0
IN_K = 1024
IN_ZR = 1280
IN_GA = 4608
IN_GR = 5632
IN_COLS = 6656
_PROJ_SEGMENTS = ((IN_GA, D_MODEL, F_GA), (IN_ZR, F_GR - F_R, F_R), (IN_GR, D_MODEL, F_GR))
Q_SCALE = HEAD_DIM ** -0.5 * LOG2E

VMEM_LIMIT = 56 * 1024 * 1024

_NN = (((1,), (0,)), ((), ()))
_NT = (((1,), (1,)), ((), ()))


def _dg(a, b, dims=_NN):
    return lax.dot_general(a, b, dims, preferred_element_type=F32)


def _split(x):
    hi = x.astype(BF16)
    lo = (x - hi.astype(F32)).astype(BF16)
    return hi, lo


def _mm1(a, b, dims=_NN):
    return _dg(a.astype(BF16), b.astype(BF16), dims)


def _layer_norm(x, g, b):
    mu = jnp.mean(x, axis=-1, keepdims=True)
    xc = x - mu
    var = jnp.mean(xc * xc, axis=-1, keepdims=True)
    return xc * lax.rsqrt(var + LN_EPS) * g + b


def _const_spec(shape):
    return pl.BlockSpec(shape, lambda *_: (0,) * len(shape), pipeline_mode=pl.Buffered(1))


def _bias_kernel(bucket_ref, rel_ref, o_ref):
    first, bk = bucket_ref[0], bucket_ref[1]
    for h in range(N_HEADS):
        acc = jnp.where(bk < 0, -jnp.inf, 0.0).astype(F32)
        for b in range(N_BUCKETS):
            acc = jnp.where(bk == b, rel_ref[b, h] * LOG2E, acc)
        o_ref[1, h] = acc
        o_ref[0, h] = jnp.where(first < 0, -jnp.inf, acc)


def _bucket_table():
    q = np.arange(BLOCK)[:, None]
    s = np.arange(2 * BLOCK)[None, :]
    dist = q + BLOCK - s
    in_window = (dist >= 0) & (dist < BLOCK)
    d0 = np.maximum(dist, 0)
    d = np.maximum(d0, 1).astype(np.float32)
    large = MAX_EXACT + (np.log(d / np.float32(MAX_EXACT)) / np.float32(math.log(MAX_DISTANCE / MAX_EXACT))
                         * (N_BUCKETS - MAX_EXACT)).astype(np.int32)
    large = np.minimum(large, N_BUCKETS - 1)
    bucket = np.where(d0 < MAX_EXACT, d0, large).astype(np.int32)
    general = np.where(in_window, bucket, -1)
    first = np.where(in_window & (s >= PAD_ROWS), bucket, -1)
    return np.stack([first.T, general.T]).astype(np.int32)


def _bias_table(rel_bias):
    return pl.pallas_call(
        _bias_kernel,
        out_shape=jax.ShapeDtypeStruct((2, N_HEADS, 2 * BLOCK, BLOCK), F32),
        in_specs=[pl.BlockSpec(memory_space=pltpu.VMEM), pl.BlockSpec(memory_space=pltpu.SMEM)],
        out_specs=pl.BlockSpec(memory_space=pltpu.VMEM),
        name="bias_table",
    )(jnp.asarray(_bucket_table()), rel_bias.astype(F32))


def _ln_proj_kernel(x_ref, g_ref, b_ref, w_ref, o16_ref, o32_ref, *, n_pad, tn):
    y = _layer_norm(x_ref[...], g_ref[...], b_ref[...])
    if n_pad:
        row = lax.broadcasted_iota(jnp.int32, y.shape, 0)
        y = jnp.where(row < n_pad, 0.0, y)
    yb = y.astype(BF16)
    lo_lane = lax.broadcasted_iota(jnp.int32, (1, LANES), 1) < HEAD_DIM

    def dup(z):
        zr = pltpu.roll(z, HEAD_DIM, 1)
        return jnp.concatenate([jnp.where(lo_lane, z, zr), jnp.where(lo_lane, zr, z)], axis=1)

    for j in range(0, D_MODEL, tn):
        o16_ref[:, B_Q + j:B_Q + j + tn] = (_dg(yb, w_ref[:, IN_Q + j:IN_Q + j + tn]) * Q_SCALE).astype(BF16)
    kv = _dg(yb, w_ref[:, IN_K:IN_K + 2 * LANES])
    o16_ref[:, B_KX:B_KX + 2 * LANES] = dup(kv[:, :LANES]).astype(BF16)
    o16_ref[:, B_VX:B_VX + 2 * LANES] = dup(kv[:, LANES:]).astype(BF16)
    for src, width, dst in _PROJ_SEGMENTS:
        for j in range(0, width, tn):
            w_ = min(tn, width - j)
            o32_ref[:, dst + j:dst + j + w_] = _dg(yb, w_ref[:, src + j:src + j + w_])


def _ln_proj(x2, g, b, w, *, tm, tn=512, n_pad=0):
    m = x2.shape[0]
    return pl.pallas_call(
        functools.partial(_ln_proj_kernel, n_pad=n_pad, tn=tn),
        out_shape=(jax.ShapeDtypeStruct((m, B_COLS), BF16), jax.ShapeDtypeStruct((m, F_COLS), F32)),
        grid=(m // tm,),
        in_specs=[
            pl.BlockSpec((tm, D_MODEL), lambda i: (i, 0)),
            _const_spec((1, D_MODEL)),
            _const_spec((1, D_MODEL)),
            _const_spec((D_MODEL, IN_COLS)),
        ],
        out_specs=(pl.BlockSpec((tm, B_COLS), lambda i: (i, 0)), pl.BlockSpec((tm, F_COLS), lambda i: (i, 0))),
        compiler_params=pltpu.CompilerParams(
            dimension_semantics=("parallel",), vmem_limit_bytes=VMEM_LIMIT),
        name="ln_proj",
    )(x2, g, b, w)


(_P_MU_R, _P_MU_K, _P_MU_V, _P_MU_WA, _P_MU_G, _P_W0, _P_A0, _P_KK, _P_KA, _P_RK, _P_LNG, _P_LNB) = range(12)
_P_ROWS = 16


def _rwkv_consts():
    r2, c2 = np.indices((LANES, LANES))
    same = (r2 < HEAD_DIM) == (c2 < HEAD_DIM)
    low = (r2 % HEAD_DIM) > (c2 % HEAD_DIM)
    masks = np.stack([low & same,
                      low & ~same,
                      (r2 % HEAD_DIM) >= (c2 % HEAD_DIM),
                      same, ~same, r2 == c2]).astype(np.float32)
    bf = np.stack([same.astype(np.float32), same.astype(np.float32) / HEAD_DIM])
    t, s = np.indices((CHUNK, CHUNK))
    return jnp.asarray(masks), jnp.asarray(bf, dtype=BF16), jnp.asarray((t >= s).astype(np.float32), dtype=BF16)


def _rwkv_kernel(r_ref, k_ref, v_ref, l_ref, gr_ref, pr_ref, pk_ref, pv_ref, plr_ref,
                 par_ref, wl_ref, g2_ref, s0_ref, msk_ref, bfm_ref, cum_ref, o_ref, s_ref,
                 h_scr, zr_scr, zk_scr, zv_scr, zl_scr, *, rows, group):
    t = pl.program_id(2)

    @pl.when(t == 0)
    def _():
        h_scr[...] = s0_ref[...]
        zr_scr[...] = pr_ref[...]
        zk_scr[...] = pk_ref[...]
        zv_scr[...] = pv_ref[...]
        zl_scr[...] = plr_ref[...]

    cat = lambda parts, axis=0: jnp.concatenate(parts, axis=axis)
    pairs = range(PAIRS_PER_STEP)
    prow = lambda q, i: par_ref[q, i:i + 1, :]
    prow2 = lambda i: cat([prow(q, i) for q in pairs], 1)
    psl = [slice(q * LANES, (q + 1) * LANES) for q in pairs]
    row_id = lax.broadcasted_iota(jnp.int32, (group, 1), 0)
    lane = lax.broadcasted_iota(jnp.int32, (1, LANES), 1)
    lo_half = lane < HEAD_DIM
    m0 = lo_half.astype(F32)
    m1 = 1.0 - m0
    mask_p, mask_a, tri_i = (msk_ref[i].astype(BF16) for i in range(3))
    head_diag, head_anti, eye = (msk_ref[i] for i in range(3, 6))
    ones_bd, avg_bd = bfm_ref[0], bfm_ref[1]
    cum = cum_ref[...]
    mu_l = cat([prow(0, _P_MU_WA), prow(0, _P_MU_G)], 1)
    cpg = group // CHUNK
    n_groups = rows // group
    csl = [slice(c * CHUNK, (c + 1) * CHUNK) for c in range(cpg)]
    zeros_q = jnp.zeros((CHUNK, LANES), BF16)

    def shifted(ref, scr, mu, g):
        z = ref[0, g * group:(g + 1) * group, :]
        prev_row = scr[SUBLANES - 1:SUBLANES, :] if g == 0 else ref[0, g * group - 1:g * group, :]
        prev = jnp.where(row_id == 0, prev_row, pltpu.roll(z, 1, 0))
        return z + (prev - z) * mu

    def prep(g, out):
        r2 = shifted(r_ref, zr_scr, prow2(_P_MU_R), g)
        k2_ = shifted(k_ref, zk_scr, prow2(_P_MU_K), g)
        v2 = shifted(v_ref, zv_scr, prow2(_P_MU_V), g)
        yield
        zl = shifted(l_ref, zl_scr, mu_l, g)
        wa, gl = zl[:, :LANES], zl[:, LANES:]
        lora_in = jnp.where(lane < DECAY_LORA, jnp.tanh(wa), wa).astype(BF16)
        gate_all = _dg((1.0 / (1.0 + jnp.exp(-gl))).astype(BF16), g2_ref[...])
        yield
        for q in pairs:
            r, k, v = r2[:, psl[q]], k2_[:, psl[q]], v2[:, psl[q]]
            lo = _dg(lora_in, wl_ref[q])
            gate = gate_all[:, psl[q]]
            kk = k * prow(q, _P_KK)
            n2 = _dg((kk * kk).astype(BF16), ones_bd)
            yield
            wpre = -(prow(q, _P_W0) + lo[:, :LANES])
            w = -(jnp.maximum(wpre, 0.0) + jnp.log(1.0 + jnp.exp(-jnp.abs(wpre)))) - 0.5
            logw = -jnp.exp(w)
            hi, lw = _split(logw)
            hl = cat([hi, lw], 1)
            lcs = [_dg(cum, hl[s]) for s in csl]
            yield
            a = 1.0 / (1.0 + jnp.exp(-(prow(q, _P_A0) + lo[:, LANES:])))
            kk = kk / jnp.maximum(jnp.sqrt(n2), 1e-12)
            k2 = k * (1.0 + (a - 1.0) * prow(q, _P_KA))
            lc = cat([x[:, :LANES] + x[:, LANES:] for x in lcs])
            e_in, e_neg = jnp.exp(lc), jnp.exp(-lc)
            yield
            rt_ = r * e_in
            at_ = -kk * jnp.exp(lc - logw)
            v_sw = pltpu.roll(v, HEAD_DIM, 1)
            kt_, bt_ = k2 * e_neg, kk * a * e_neg
            out.append(dict(q=q, r=r, k2=k2, v=v, gate=gate, e_in=e_in, rt=rt_, at=at_,
                            kt0=kt_ * m0, kt1=kt_ * m1, bt0=bt_ * m0, bt1=bt_ * m1,
                            vs0=(v_sw * m0).astype(BF16), vs1=(v_sw * m1).astype(BF16)))
            yield

    def phase1(pp, out):
        units = [(p_, s) for p_ in pp for s in csl]
        get = lambda n: [p_[n][s] for p_, s in units]
        at_, rt_, bt0, bt1, kt0, kt1, vs0, vs1 = (get(n) for n in ("at", "rt", "bt0", "bt1", "kt0", "kt1", "vs0", "vs1"))
        gcs = [p_["e_in"][s.stop - 1:s.stop, :] for p_, s in units]
        d_ab = [_mm1(cat([a_, r_]), cat([b0, k0, k1, b1]), _NT)
                for a_, r_, b0, k0, k1, b1 in zip(at_, rt_, bt0, kt0, kt1, bt1)]
        yield
        top = [cat([d[:CHUNK, :LANES], d[:CHUNK, LANES:]]).astype(BF16) for d in d_ab]
        bot = [cat([d[CHUNK:, :LANES], d[CHUNK:, LANES:]]).astype(BF16) * tri_i for d in d_ab]
        p = [tp * mask_p for tp in top]
        xb = [(cat([a_ * m0, a_ * m1]) + _dg(tp * mask_a, cat([w0, w1]))).astype(BF16)
              for a_, tp, w0, w1 in zip(at_, top, vs0, vs1)]
        yield
        for i in range(6):
            if i < 5:
                px = [_dg(pw, cat([x, pw], 1)) for x, pw in zip(xb, p)]
                p = [d[:, LANES:].astype(BF16) for d in px]
                xb = [x + d[:, :LANES].astype(BF16) for x, d in zip(xb, px)]
            else:
                xb = [x + _dg(pw, x).astype(BF16) for x, pw in zip(xb, p)]
            yield
        gst = [cat([x[:CHUNK], w1, w0, x[CHUNK:]]) for x, w0, w1 in zip(xb, vs0, vs1)]
        lhs45 = []
        for b_, b0, k0, k1, b1, gc in zip(bot, bt0, kt0, kt1, bt1, gcs):
            lhs45.append(cat([cat([b_[:CHUNK], zeros_q], 1), cat([zeros_q, b_[CHUNK:]], 1),
                              cat([b0 * gc, k0 * gc, k1 * gc, b1 * gc]).T.astype(BF16)]))
        o45 = [_dg(l_, g_) for l_, g_ in zip(lhs45, gst)]
        yield
        out["lhs"] = [cat([(r_ + jnp.where(lo_half, o[:CHUNK], o[CHUNK:LANES])).astype(BF16),
                           (o[LANES:] * head_diag + eye * gc).astype(BF16)])
                      for r_, o, gc in zip(rt_, o45, gcs)]
        out["y_in"] = [jnp.where(lo_half, o[CHUNK:LANES], o[:CHUNK]) for o in o45]
        out["g_c"] = [o[LANES:] * head_anti for o in o45]

    def chain_step(f, c, hs, ys):
        for q in pairs:
            u = q * cpg + c
            yh = _dg(f["lhs"][u], hs[q].astype(BF16))
            ys[q].append(yh[:CHUNK] + f["y_in"][u])
            hs[q] = yh[CHUNK:] + f["g_c"][u]

    def epilogue(g, pp, ys):
        gsl = slice(g * group, (g + 1) * group)
        for p_ in pp:
            q = p_["q"]
            y = pltpu.roll(cat(ys[q]), HEAD_DIM, 1)
            yc = y - _dg(y.astype(BF16), avg_bd)
            yv = _dg((yc * yc).astype(BF16), avg_bd)
            yn = yc * lax.rsqrt(yv + GN_EPS) * prow(q, _P_LNG) + prow(q, _P_LNB)
            bonus = _dg((p_["r"] * p_["k2"] * prow(q, _P_RK)).astype(BF16), ones_bd) * p_["v"]
            o_ref[0, gsl, psl[q]] = (yn + bonus) * p_["gate"] / (1.0 + jnp.exp(-gr_ref[0, gsl, psl[q]]))

    n_stages, n_pieces = 9, 2 + 4 * PAIRS_PER_STEP
    hs = [h_scr[q] for q in pairs]
    p_all = [[] for _ in range(n_groups)]
    f_all = [{} for _ in range(n_groups)]
    for _ in prep(0, p_all[0]):
        pass
    for g in range(n_groups):
        side = prep(g + 1, p_all[g + 1]) if g + 1 < n_groups else iter(())
        ys = [[] for _ in pairs]
        done = 0
        for i, _ in enumerate(phase1(p_all[g], f_all[g])):
            while done * n_stages < (i + 1) * n_pieces:
                next(side, None)
                done += 1
            if g > 0 and i % 2 == 1 and len(ys[0]) < cpg:
                chain_step(f_all[g - 1], len(ys[0]), hs, ys)
        for _ in side:
            pass
        if g > 0:
            while len(ys[0]) < cpg:
                chain_step(f_all[g - 1], len(ys[0]), hs, ys)
            epilogue(g - 1, p_all[g - 1], ys)
    ys = [[] for _ in pairs]
    for c in range(cpg):
        chain_step(f_all[n_groups - 1], c, hs, ys)
    epilogue(n_groups - 1, p_all[n_groups - 1], ys)
    for q in pairs:
        h_scr[q] = hs[q]
    zr_scr[...] = r_ref[0, rows - SUBLANES:rows, :]
    zk_scr[...] = k_ref[0, rows - SUBLANES:rows, :]
    zv_scr[...] = v_ref[0, rows - SUBLANES:rows, :]
    zl_scr[...] = l_ref[0, rows - SUBLANES:rows, :]

    @pl.when(t == pl.num_programs(2) - 1)
    def _():
        for q in pairs:
            s_ref[0, q] = hs[q]


def _rwkv(proj3, prev8, par, wl, g2, s0, *, rows):
    bsz, t_len, _ = proj3.shape
    masks, bfm, cum = _rwkv_consts()
    pps = PAIRS_PER_STEP
    wide = pps * LANES
    col = lambda off: (lambda b, p, t: (b, t, off // wide + p))
    pcol = lambda off: (lambda b, p, t: (0, off // wide + p))
    return pl.pallas_call(
        functools.partial(_rwkv_kernel, rows=rows, group=min(RWKV_GROUP, rows)),
        out_shape=(jax.ShapeDtypeStruct((bsz, t_len, D_MODEL), F32),
                   jax.ShapeDtypeStruct((bsz, N_PAIRS, LANES, LANES), F32)),
        grid=(bsz, N_PAIRS // pps, t_len // rows),
        in_specs=[
            pl.BlockSpec((1, rows, wide), col(F_R)),
            pl.BlockSpec((1, rows, wide), col(F_K)),
            pl.BlockSpec((1, rows, wide), col(F_V)),
            pl.BlockSpec((1, rows, 2 * LANES), lambda b, p, t: (b, t, F_L // (2 * LANES))),
            pl.BlockSpec((1, rows, wide), col(F_GR)),
            pl.BlockSpec((SUBLANES, wide), pcol(F_R)),
            pl.BlockSpec((SUBLANES, wide), pcol(F_K)),
            pl.BlockSpec((SUBLANES, wide), pcol(F_V)),
            pl.BlockSpec((SUBLANES, 2 * LANES), lambda b, p, t: (0, F_L // (2 * LANES))),
            pl.BlockSpec((pps, _P_ROWS, LANES), lambda b, p, t: (p, 0, 0)),
            pl.BlockSpec((pps, LANES, 2 * LANES), lambda b, p, t: (p, 0, 0)),
            pl.BlockSpec((GATE_LORA, wide), lambda b, p, t: (0, p)),
            pl.BlockSpec((pps, LANES, LANES), lambda b, p, t: (p, 0, 0)),
            _const_spec(masks.shape), _const_spec(bfm.shape), _const_spec(cum.shape),
        ],
        out_specs=(pl.BlockSpec((1, rows, wide), lambda b, p, t: (b, t, p)),
                   pl.BlockSpec((1, pps, LANES, LANES), lambda b, p, t: (b, p, 0, 0))),
        scratch_shapes=[pltpu.VMEM((pps, LANES, LANES), F32),
                        pltpu.VMEM((SUBLANES, wide), F32), pltpu.VMEM((SUBLANES, wide), F32),
                        pltpu.VMEM((SUBLANES, wide), F32), pltpu.VMEM((SUBLANES, 2 * LANES), F32)],
        compiler_params=pltpu.CompilerParams(
            dimension_semantics=("parallel", "parallel", "arbitrary"), vmem_limit_bytes=VMEM_LIMIT),
        name="rwkv",
    )(proj3, proj3, proj3, proj3, proj3, prev8, prev8, prev8, prev8, par, wl, g2, s0, masks, bfm, cum)


def _attn_kernel(sink_ref, q_ref, ga_ref, kvc_ref, kvp_ref, kvm_ref, bf_ref, bg_ref, rw_ref, o_ref):
    first = pl.program_id(1) == 0
    lane = lax.broadcasted_iota(jnp.int32, (1, LANES), 1)
    lo_lane = lane < HEAD_DIM
    lo_row = lax.broadcasted_iota(jnp.int32, (LANES, 1), 0) < HEAD_DIM
    kv_w = 2 * LANES
    kc, vc = kvc_ref[0, :, :kv_w], kvc_ref[0, :, kv_w:]
    kv_prev = jnp.where(first, kvm_ref[...], kvp_ref[0])
    k_prev, v_prev = kv_prev[:, :kv_w], kv_prev[:, kv_w:]
    ppk = N_PAIRS // N_KV

    def scores(sb, kh):
        rs = slice(sb * BLOCK, (sb + 1) * BLOCK)
        kp_, vp_ = (k_prev, v_prev) if sb == 0 else (kc[rs.start - BLOCK:rs.start], vc[rs.start - BLOCK:rs.start])
        kx = jnp.concatenate([kp_, kc[rs]], axis=0)[:, kh * LANES:(kh + 1) * LANES]
        vx = jnp.concatenate([vp_, vc[rs]], axis=0)[:, kh * LANES:(kh + 1) * LANES].astype(F32)
        lts = []
        for p in range(kh * ppk, (kh + 1) * ppk):
            qt = q_ref[0, rs, p * LANES:(p + 1) * LANES]
            zq = jnp.zeros_like(qt)
            q2 = jnp.concatenate([jnp.where(lo_lane, qt, zq), jnp.where(lo_lane, zq, qt)], axis=0)
            lts.append(_dg(kx, q2, _NT))
        vxt = jnp.concatenate([vx.T, jnp.ones((SUM_ROWS, 2 * BLOCK), F32)], axis=0).astype(BF16)
        return dict(sb=sb, kh=kh, rs=rs, lts=lts, vxt=vxt)

    def finish(g):
        bias_ref = bf_ref if g["sb"] == 0 else bg_ref
        pes, sinks = [], []
        for j, lt in enumerate(g["lts"]):
            for sub in range(2):
                hd = 2 * (g["kh"] * ppk + j) + sub
                lg = lt[:, sub * LANES:(sub + 1) * LANES] + bias_ref[0, hd]
                m = jnp.maximum(jnp.max(lg, axis=0, keepdims=True), sink_ref[hd])
                sinks.append(jnp.exp2(sink_ref[hd] - m))
                pes.append(jnp.exp2(lg - m).astype(BF16))
        ots = [_dg(g["vxt"], jnp.concatenate(pes[2 * j:2 * j + 2], axis=1)) for j in range(ppk)]
        for j, ot in enumerate(ots):
            cs = slice((g["kh"] * ppk + j) * LANES, (g["kh"] * ppk + j + 1) * LANES)
            inv0 = 1.0 / (ot[LANES:LANES + 1, :LANES] + sinks[2 * j])
            inv1 = 1.0 / (ot[LANES:LANES + 1, LANES:] + sinks[2 * j + 1])
            att = jnp.where(lo_row, ot[:LANES, :LANES] * inv0, ot[:LANES, LANES:] * inv1).T
            gate = 1.0 / (1.0 + jnp.exp(-ga_ref[0, g["rs"], cs]))
            o_ref[0, g["rs"], cs] = (gate * att + rw_ref[0, g["rs"], cs]).astype(BF16)

    order = [(sb, kh) for sb in range(ATT_ROWS // BLOCK) for kh in range(N_KV)]
    cur = scores(*order[0])
    for nxt in order[1:]:
        ahead = scores(*nxt)
        finish(cur)
        cur = ahead
    finish(cur)


def _attn_merge(proj16, proj32, kvm, bias, sinks, rwg):
    bsz, t_len, _ = proj16.shape
    kv_w = 4 * LANES
    assert B_VX == B_KX + kv_w // 2 and B_KX % kv_w == 0
    per = ATT_ROWS // BLOCK
    rows = lambda w, col: pl.BlockSpec((1, ATT_ROWS, w), lambda b, n: (b, n, col // w))
    return pl.pallas_call(
        _attn_kernel,
        out_shape=jax.ShapeDtypeStruct((bsz, t_len, D_MODEL), BF16),
        grid=(bsz, t_len // ATT_ROWS),
        in_specs=[
            pl.BlockSpec(memory_space=pltpu.SMEM),
            rows(D_MODEL, B_Q), rows(D_MODEL, F_GA), rows(kv_w, B_KX),
            pl.BlockSpec((1, BLOCK, kv_w), lambda b, n: (b, jnp.maximum(n * per - 1, 0), B_KX // kv_w)),
            pl.BlockSpec((BLOCK, kv_w), lambda b, n: (0, B_KX // kv_w)),
            pl.BlockSpec((1, N_HEADS, 2 * BLOCK, BLOCK), lambda b, n: (jnp.minimum(n, 1), 0, 0, 0)),
            pl.BlockSpec((1, N_HEADS, 2 * BLOCK, BLOCK), lambda b, n: (1, 0, 0, 0)),
            rows(D_MODEL, 0),
        ],
        out_specs=rows(D_MODEL, 0),
        compiler_params=pltpu.CompilerParams(
            dimension_semantics=("parallel", "arbitrary"), vmem_limit_bytes=VMEM_LIMIT),
        name="attn_merge",
    )(sinks, proj16, proj32, proj16, proj16, kvm, bias, bias, rwg)


def _out_ffn_kernel(x_ref, mg_ref, ln_ref, wo_ref, w1_ref, w2_ref, o_ref, *, ff_tile, parts):
    ln = ln_ref[...]
    tm = x_ref.shape[0]
    rs = [slice(i * tm // parts, (i + 1) * tm // parts) for i in range(parts)]
    proj = [_dg(mg_ref[r, :], wo_ref[...]) for r in rs]
    h1 = [_layer_norm(ALPHA * _layer_norm(x_ref[r, :], ln[0:1], ln[1:2]) + p, ln[2:3], ln[3:4]) for r, p in zip(rs, proj)]
    for r, h in zip(rs, h1):
        hb = h.astype(BF16)
        acc = ALPHA * h
        for j in range(D_FF // ff_tile):
            cs = slice(j * ff_tile, (j + 1) * ff_tile)
            u = jnp.maximum(_dg(hb, w1_ref[:, cs]), 0.0)
            acc = acc + _dg((u * u).astype(BF16), w2_ref[cs, :])
        o_ref[r, :] = _layer_norm(acc, ln[4:5], ln[5:6])


def _out_ffn(x2, merged2, ln_par, wo, w1, w2, *, tm, ff_tile=1024, parts=2):
    m = x2.shape[0]
    return pl.pallas_call(
        functools.partial(_out_ffn_kernel, ff_tile=ff_tile, parts=parts),
        out_shape=jax.ShapeDtypeStruct((m, D_MODEL), F32),
        grid=(m // tm,),
        in_specs=[
            pl.BlockSpec((tm, D_MODEL), lambda i: (i, 0)),
            pl.BlockSpec((tm, D_MODEL), lambda i: (i, 0)),
            _const_spec((SUBLANES, D_MODEL)),
            _const_spec((D_MODEL, D_MODEL)),
            _const_spec((D_MODEL, D_FF)),
            _const_spec((D_FF, D_MODEL)),
        ],
        out_specs=pl.BlockSpec((tm, D_MODEL), lambda i: (i, 0)),
        compiler_params=pltpu.CompilerParams(
            dimension_semantics=("parallel",), vmem_limit_bytes=VMEM_LIMIT),
        name="out_ffn",
    )(x2, merged2, ln_par, wo, w1, w2)


def kernel(x, meta_tokens, ln0_g, ln0_b, rel_bias, w_in, shift_mu, attn_sinks, decay_w0, decay_w2, iclr_a0, iclr_a2, gate_w2, k_k, k_a, r_k, lnx_g, lnx_b, w_out, ln1_g, ln1_b, w_ff1, w_ff2, ln2_g, ln2_b):
    bsz, seq, _ = x.shape
    assert D_MODEL == x.shape[2] and seq % RWKV_ROWS == 0 and seq % ATT_ROWS == 0
    assert (bsz * seq) % PROJ_ROWS == 0 and (bsz * seq) % FFN_ROWS == 0
    W = D_MODEL
    wi = w_in[0]
    w_perm = wi.astype(BF16)
    assert w_perm.shape[1] == IN_COLS
    mu = shift_mu[0]
    rows_ = [mu[:W], mu[W:2 * W], mu[2 * W:3 * W]]
    vec = lambda a: a.reshape(N_PAIRS, 1, LANES)
    par = jnp.concatenate(
        [vec(rows_[0]), vec(rows_[1]), vec(rows_[2]),
         jnp.broadcast_to(mu[3 * W:3 * W + LANES].reshape(1, 1, LANES), (N_PAIRS, 1, LANES)),
         jnp.broadcast_to(mu[3 * W + LANES:].reshape(1, 1, LANES), (N_PAIRS, 1, LANES)),
         vec(decay_w0[0]), vec(iclr_a0[0]), vec(k_k[0]), vec(k_a[0]), vec(r_k[0].reshape(-1)),
         vec(lnx_g[0]), vec(lnx_b[0]),
         jnp.zeros((N_PAIRS, _P_ROWS - 12, LANES), F32)], axis=1).astype(F32)
    w2p = decay_w2[0].reshape(DECAY_LORA, N_PAIRS, LANES).transpose(1, 0, 2)
    a2p = iclr_a2[0].reshape(ICLR_LORA, N_PAIRS, LANES).transpose(1, 0, 2)
    z = jnp.zeros_like(w2p)
    wl = jnp.concatenate([jnp.concatenate([w2p, z], axis=2), jnp.concatenate([z, a2p], axis=2)], axis=1).astype(BF16)
    g2 = gate_w2[0].astype(BF16)
    ln_par = jnp.stack([ln0_g, ln0_b, ln1_g[0], ln1_b[0], ln2_g[0], ln2_b[0],
                        jnp.zeros_like(ln0_g), jnp.zeros_like(ln0_g)]).astype(F32)
    g0 = ln0_g.reshape(1, W).astype(F32)
    b0 = ln0_b.reshape(1, W).astype(F32)

    bias = _bias_table(rel_bias)

    meta_blk = jnp.concatenate([jnp.zeros((PAD_ROWS, W), F32), meta_tokens.astype(F32)], axis=0)
    proj_m16, proj_m32 = _ln_proj(meta_blk, g0, b0, w_perm, tm=BLOCK, n_pad=PAD_ROWS)
    zeros8 = jnp.zeros((SUBLANES, F_COLS), F32)
    s_zero = jnp.zeros((N_PAIRS, LANES, LANES), F32)
    _, s_meta = _rwkv(proj_m32[None], zeros8, par, wl, g2, s_zero, rows=BLOCK)
    prev8 = proj_m32[BLOCK - SUBLANES:]

    x2 = x.reshape(bsz * seq, W)
    proj16, proj32 = _ln_proj(x2, g0, b0, w_perm, tm=PROJ_ROWS)
    proj32 = proj32.reshape(bsz, seq, F_COLS)
    rwg, _ = _rwkv(proj32, prev8, par, wl, g2, s_meta[0], rows=RWKV_ROWS)
    merged = _attn_merge(proj16.reshape(bsz, seq, B_COLS), proj32, proj_m16, bias,
                         attn_sinks[0].astype(F32) * LOG2E, rwg)
    out = _out_ffn(x2, merged.reshape(bsz * seq, W), ln_par, w_out[0].astype(BF16),
                   w_ff1[0].astype(BF16), w_ff2[0].astype(BF16), tm=FFN_ROWS)
    return out.reshape(bsz, seq, W)
```

```python
import functools
import math

import numpy as np
import jax
import jax.numpy as jnp
from jax import lax
from jax.experimental import pallas as pl
from jax.experimental.pallas import tpu as pltpu

F32 = jnp.float32
BF16 = jnp.bfloat16

D_MODEL = 1024
N_META = 16
HEAD_DIM = 64
N_HEADS = D_MODEL // HEAD_DIM
N_KV = 2
BLOCK = 128
N_BUCKETS = 32
MAX_EXACT = 16
MAX_DISTANCE = 128
DECAY_LORA = 64
ICLR_LORA = 64
GATE_LORA = 128
D_FF = 4 * D_MODEL
LN_EPS = 1e-5
GN_EPS = 1e-5 * HEAD_DIM
DEPTH = 1
ALPHA = (2.0 * DEPTH) ** 0.25
LOG2E = math.log2(math.e)

LANES = 128
SUBLANES = 8
N_PAIRS = D_MODEL // LANES
CHUNK = 64
RWKV_ROWS = 2048
RWKV_GROUP = 256
PAIRS_PER_STEP = 2
PROJ_ROWS = 512
FFN_ROWS = 512
PAD_ROWS = BLOCK - N_META
ATT_ROWS = 4 * BLOCK
SUM_ROWS = 16

B_Q = 0
B_KX = 1024
B_VX = 1280
B_COLS = 1536
F_GA = 0
F_R = 1024
F_K = 2048
F_V = 3072
F_L = 4096
F_GR = 4352
F_COLS = 5376

IN_Q = 0
IN_K = 1024
IN_ZR = 1280
IN_GA = 4608
IN_GR = 5632
IN_COLS = 6656
_PROJ_SEGMENTS = ((IN_GA, D_MODEL, F_GA), (IN_ZR, F_GR - F_R, F_R), (IN_GR, D_MODEL, F_GR))
Q_SCALE = HEAD_DIM ** -0.5 * LOG2E

VMEM_LIMIT = 56 * 1024 * 1024

_NN = (((1,), (0,)), ((), ()))
_NT = (((1,), (1,)), ((), ()))


def _dg(a, b, dims=_NN):
    return lax.dot_general(a, b, dims, preferred_element_type=F32)


def _split(x):
    hi = x.astype(BF16)
    lo = (x - hi.astype(F32)).astype(BF16)
    return hi, lo


def _mm1(a, b, dims=_NN):
    return _dg(a.astype(BF16), b.astype(BF16), dims)


def _layer_norm(x, g, b):
    mu = jnp.mean(x, axis=-1, keepdims=True)
    xc = x - mu
    var = jnp.mean(xc * xc, axis=-1, keepdims=True)
    return xc * lax.rsqrt(var + LN_EPS) * g + b


def _const_spec(shape):
    return pl.BlockSpec(shape, lambda *_: (0,) * len(shape), pipeline_mode=pl.Buffered(1))


def _bias_kernel(bucket_ref, rel_ref, o_ref):
    first, bk = bucket_ref[0], bucket_ref[1]
    for h in range(N_HEADS):
        acc = jnp.where(bk < 0, -jnp.inf, 0.0).astype(F32)
        for b in range(N_BUCKETS):
            acc = jnp.where(bk == b, rel_ref[b, h] * LOG2E, acc)
        o_ref[1, h] = acc
        o_ref[0, h] = jnp.where(first < 0, -jnp.inf, acc)


def _bucket_table():
    q = np.arange(BLOCK)[:, None]
    s = np.arange(2 * BLOCK)[None, :]
    dist = q + BLOCK - s
    in_window = (dist >= 0) & (dist < BLOCK)
    d0 = np.maximum(dist, 0)
    d = np.maximum(d0, 1).astype(np.float32)
    large = MAX_EXACT + (np.log(d / np.float32(MAX_EXACT)) / np.float32(math.log(MAX_DISTANCE / MAX_EXACT))
                         * (N_BUCKETS - MAX_EXACT)).astype(np.int32)
    large = np.minimum(large, N_BUCKETS - 1)
    bucket = np.where(d0 < MAX_EXACT, d0, large).astype(np.int32)
    general = np.where(in_window, bucket, -1)
    first = np.where(in_window & (s >= PAD_ROWS), bucket, -1)
    return np.stack([first.T, general.T]).astype(np.int32)


def _bias_table(rel_bias):
    return pl.pallas_call(
        _bias_kernel,
        out_shape=jax.ShapeDtypeStruct((2, N_HEADS, 2 * BLOCK, BLOCK), F32),
        in_specs=[pl.BlockSpec(memory_space=pltpu.VMEM), pl.BlockSpec(memory_space=pltpu.SMEM)],
        out_specs=pl.BlockSpec(memory_space=pltpu.VMEM),
        name="bias_table",
    )(jnp.asarray(_bucket_table()), rel_bias.astype(F32))


def _ln_proj_kernel(x_ref, g_ref, b_ref, w_ref, o16_ref, o32_ref, *, n_pad, tn):
    y = _layer_norm(x_ref[...], g_ref[...], b_ref[...])
    if n_pad:
        row = lax.broadcasted_iota(jnp.int32, y.shape, 0)
        y = jnp.where(row < n_pad, 0.0, y)
    yb = y.astype(BF16)
    lo_lane = lax.broadcasted_iota(jnp.int32, (1, LANES), 1) < HEAD_DIM

    def dup(z):
        zr = pltpu.roll(z, HEAD_DIM, 1)
        return jnp.concatenate([jnp.where(lo_lane, z, zr), jnp.where(lo_lane, zr, z)], axis=1)

    for j in range(0, D_MODEL, tn):
        o16_ref[:, B_Q + j:B_Q + j + tn] = (_dg(yb, w_ref[:, IN_Q + j:IN_Q + j + tn]) * Q_SCALE).astype(BF16)
    kv = _dg(yb, w_ref[:, IN_K:IN_K + 2 * LANES])
    o16_ref[:, B_KX:B_KX + 2 * LANES] = dup(kv[:, :LANES]).astype(BF16)
    o16_ref[:, B_VX:B_VX + 2 * LANES] = dup(kv[:, LANES:]).astype(BF16)
    for src, width, dst in _PROJ_SEGMENTS:
        for j in range(0, width, tn):
            w_ = min(tn, width - j)
            o32_ref[:, dst + j:dst + j + w_] = _dg(yb, w_ref[:, src + j:src + j + w_])


def _ln_proj(x2, g, b, w, *, tm, tn=512, n_pad=0):
    m = x2.shape[0]
    return pl.pallas_call(
        functools.partial(_ln_proj_kernel, n_pad=n_pad, tn=tn),
        out_shape=(jax.ShapeDtypeStruct((m, B_COLS), BF16), jax.ShapeDtypeStruct((m, F_COLS), F32)),
        grid=(m // tm,),
        in_specs=[
            pl.BlockSpec((tm, D_MODEL), lambda i: (i, 0)),
            _const_spec((1, D_MODEL)),
            _const_spec((1, D_MODEL)),
            _const_spec((D_MODEL, IN_COLS)),
        ],
        out_specs=(pl.BlockSpec((tm, B_COLS), lambda i: (i, 0)), pl.BlockSpec((tm, F_COLS), lambda i: (i, 0))),
        compiler_params=pltpu.CompilerParams(
            dimension_semantics=("parallel",), vmem_limit_bytes=VMEM_LIMIT),
        name="ln_proj",
    )(x2, g, b, w)


(_P_MU_R, _P_MU_K, _P_MU_V, _P_MU_WA, _P_MU_G, _P_W0, _P_A0, _P_KK, _P_KA, _P_RK, _P_LNG, _P_LNB) = range(12)
_P_ROWS = 16


def _rwkv_consts():
    r2, c2 = np.indices((LANES, LANES))
    same = (r2 < HEAD_DIM) == (c2 < HEAD_DIM)
    low = (r2 % HEAD_DIM) > (c2 % HEAD_DIM)
    masks = np.stack([low & same,
                      low & ~same,
                      (r2 % HEAD_DIM) >= (c2 % HEAD_DIM),
                      same, ~same, r2 == c2]).astype(np.float32)
    bf = np.stack([same.astype(np.float32), same.astype(np.float32) / HEAD_DIM])
    ones2 = np.kron(np.eye(2, dtype=np.float32), same.astype(np.float32))
    t, s = np.indices((CHUNK, CHUNK))
    return (jnp.asarray(masks), jnp.asarray(bf, dtype=BF16), jnp.asarray(ones2, dtype=BF16),
            jnp.asarray((t >= s).astype(np.float32), dtype=BF16))


def _rwkv_kernel(r_ref, k_ref, v_ref, l_ref, gr_ref, pr_ref, pk_ref, pv_ref, plr_ref,
                 par_ref, wl_ref, g2_ref, s0_ref, msk_ref, bfm_ref, ones2_ref, cum_ref, o_ref, s_ref,
                 h_scr, zr_scr, zk_scr, zv_scr, zl_scr, *, rows, group, pps):
    t = pl.program_id(2)

    @pl.when(t == 0)
    def _():
        h_scr[...] = s0_ref[...]
        zr_scr[...] = pr_ref[...]
        zk_scr[...] = pk_ref[...]
        zv_scr[...] = pv_ref[...]
        zl_scr[...] = plr_ref[...]

    cat = lambda parts, axis=0: jnp.concatenate(parts, axis=axis)
    pairs = range(pps)
    prow = lambda q, i: par_ref[q, i:i + 1, :]
    prow2 = lambda i: cat([prow(q, i) for q in pairs], 1)
    psl = [slice(q * LANES, (q + 1) * LANES) for q in pairs]
    row_id = lax.broadcasted_iota(jnp.int32, (group, 1), 0)
    lane = lax.broadcasted_iota(jnp.int32, (1, LANES), 1)
    lo_half = lane < HEAD_DIM
    m0 = lo_half.astype(F32)
    m1 = 1.0 - m0
    mask_p, mask_a, tri_i = (msk_ref[i].astype(BF16) for i in range(3))
    head_diag, head_anti, eye = (msk_ref[i] for i in range(3, 6))
    avg_bd = bfm_ref[1]
    ones2 = ones2_ref[...]
    cum = cum_ref[...]
    mu_l = cat([prow(0, _P_MU_WA), prow(0, _P_MU_G)], 1)
    cpg = group // CHUNK
    n_groups = rows // group
    csl = [slice(c * CHUNK, (c + 1) * CHUNK) for c in range(cpg)]
    zeros_q = jnp.zeros((CHUNK, LANES), BF16)

    def shifted(ref, scr, mu, g):
        z = ref[0, g * group:(g + 1) * group, :]
        prev_row = scr[SUBLANES - 1:SUBLANES, :] if g == 0 else ref[0, g * group - 1:g * group, :]
        prev = jnp.where(row_id == 0, prev_row, pltpu.roll(z, 1, 0))
        return z + (prev - z) * mu

    def prep(g, out):
        r2 = shifted(r_ref, zr_scr, prow2(_P_MU_R), g)
        k2_ = shifted(k_ref, zk_scr, prow2(_P_MU_K), g)
        v2 = shifted(v_ref, zv_scr, prow2(_P_MU_V), g)
        yield
        zl = shifted(l_ref, zl_scr, mu_l, g)
        wa, gl = zl[:, :LANES], zl[:, LANES:]
        lora_in = jnp.where(lane < DECAY_LORA, jnp.tanh(wa), wa).astype(BF16)
        gate_all = _dg((1.0 / (1.0 + jnp.exp(-gl))).astype(BF16), g2_ref[...])
        yield
        for q in pairs:
            r, k, v = r2[:, psl[q]], k2_[:, psl[q]], v2[:, psl[q]]
            lo = _dg(lora_in, wl_ref[q])
            gate = gate_all[:, psl[q]]
            kk = k * prow(q, _P_KK)
            yield
            wpre = -(prow(q, _P_W0) + lo[:, :LANES])
            w = -(jnp.maximum(wpre, 0.0) + jnp.log(1.0 + jnp.exp(-jnp.abs(wpre)))) - 0.5
            logw = -jnp.exp(w)
            hi, lw = _split(logw)
            hl = cat([hi, lw], 1)
            lcs = [_dg(cum, hl[s]) for s in csl]
            a = 1.0 / (1.0 + jnp.exp(-(prow(q, _P_A0) + lo[:, LANES:])))
            k2 = k * (1.0 + (a - 1.0) * prow(q, _P_KA))
            sums = _dg(cat([kk * kk, r * k2 * prow(q, _P_RK)], 1).astype(BF16), ones2)
            yield
            kk = kk / jnp.maximum(jnp.sqrt(sums[:, :LANES]), 1e-12)
            lc = cat([x[:, :LANES] + x[:, LANES:] for x in lcs])
            e_in, e_neg = jnp.exp(lc), jnp.exp(-lc)
            yield
            rt_ = r * e_in
            at_ = -kk * jnp.exp(lc - logw)
            v_sw = pltpu.roll(v, HEAD_DIM, 1)
            kt_, bt_ = k2 * e_neg, kk * a * e_neg
            out.append(dict(q=q, v=v, gate=gate, bonus=sums[:, LANES:] * v, e_in=e_in, rt=rt_, at=at_,
                            kt0=kt_ * m0, kt1=kt_ * m1, bt0=bt_ * m0, bt1=bt_ * m1,
                            vs0=(v_sw * m0).astype(BF16), vs1=(v_sw * m1).astype(BF16)))
            yield

    def phase1(pp, out):
        units = [(p_, s) for p_ in pp for s in csl]
        get = lambda n: [p_[n][s] for p_, s in units]
        at_, rt_, bt0, bt1, kt0, kt1, vs0, vs1 = (get(n) for n in ("at", "rt", "bt0", "bt1", "kt0", "kt1", "vs0", "vs1"))
        gcs = [p_["e_in"][s.stop - 1:s.stop, :] for p_, s in units]
        d_ab = [_mm1(cat([a_, r_]), cat([b0, k0, k1, b1]), _NT)
                for a_, r_, b0, k0, k1, b1 in zip(at_, rt_, bt0, kt0, kt1, bt1)]
        yield
        top = [cat([d[:CHUNK, :LANES], d[:CHUNK, LANES:]]).astype(BF16) for d in d_ab]
        bot = [cat([d[CHUNK:, :LANES], d[CHUNK:, LANES:]]).astype(BF16) * tri_i for d in d_ab]
        p = [tp * mask_p for tp in top]
        xb = [(cat([a_ * m0, a_ * m1]) + _dg(tp * mask_a, cat([w0, w1]))).astype(BF16)
              for a_, tp, w0, w1 in zip(at_, top, vs0, vs1)]
        yield
        for i in range(6):
            if i < 5:
                px = [_dg(pw, cat([x, pw], 1)) for x, pw in zip(xb, p)]
                p = [d[:, LANES:].astype(BF16) for d in px]
                xb = [x + d[:, :LANES].astype(BF16) for x, d in zip(xb, px)]
            else:
                xb = [x + _dg(pw, x).astype(BF16) for x, pw in zip(xb, p)]
            yield
        gst = [cat([x[:CHUNK], w1, w0, x[CHUNK:]]) for x, w0, w1 in zip(xb, vs0, vs1)]
        lhs45 = []
        for b_, b0, k0, k1, b1, gc in zip(bot, bt0, kt0, kt1, bt1, gcs):
            lhs45.append(cat([cat([b_[:CHUNK], zeros_q], 1), cat([zeros_q, b_[CHUNK:]], 1),
                              cat([b0 * gc, k0 * gc, k1 * gc, b1 * gc]).T.astype(BF16)]))
        o45 = [_dg(l_, g_) for l_, g_ in zip(lhs45, gst)]
        yield
        out["lhs"] = [cat([(r_ + jnp.where(lo_half, o[:CHUNK], o[CHUNK:LANES])).astype(BF16),
                           (o[LANES:] * head_diag + eye * gc).astype(BF16)])
                      for r_, o, gc in zip(rt_, o45, gcs)]
        out["y_in"] = [jnp.where(lo_half, o[CHUNK:LANES], o[:CHUNK]) for o in o45]
        out["g_c"] = [o[LANES:] * head_anti for o in o45]

    def chain_step(f, c, hs, ys):
        for q in pairs:
            u = q * cpg + c
            yh = _dg(f["lhs"][u], hs[q].astype(BF16))
            ys[q].append(yh[:CHUNK] + f["y_in"][u])
            hs[q] = yh[CHUNK:] + f["g_c"][u]

    def epilogue(g, pp, ys):
        gsl = slice(g * group, (g + 1) * group)
        for p_ in pp:
            q = p_["q"]
            y = pltpu.roll(cat(ys[q]), HEAD_DIM, 1)
            yc = y - _dg(y.astype(BF16), avg_bd)
            yv = _dg((yc * yc).astype(BF16), avg_bd)
            yn = yc * lax.rsqrt(yv + GN_EPS) * prow(q, _P_LNG) + prow(q, _P_LNB)
            o_ref[0, gsl, psl[q]] = (yn + p_["bonus"]) * p_["gate"] / (1.0 + jnp.exp(-gr_ref[0, gsl, psl[q]]))

    n_stages, n_pieces = 9, 2 + 4 * pps
    hs = [h_scr[q] for q in pairs]
    p_all = [[] for _ in range(n_groups)]
    f_all = [{} for _ in range(n_groups)]
    for _ in prep(0, p_all[0]):
        pass
    for g in range(n_groups):
        side = prep(g + 1, p_all[g + 1]) if g + 1 < n_groups else iter(())
        ys = [[] for _ in pairs]
        done = 0
        for i, _ in enumerate(phase1(p_all[g], f_all[g])):
            while done * n_stages < (i + 1) * n_pieces:
                next(side, None)
                done += 1
            if g > 0 and i % 2 == 1 and len(ys[0]) < cpg:
                chain_step(f_all[g - 1], len(ys[0]), hs, ys)
        for _ in side:
            pass
        if g > 0:
            while len(ys[0]) < cpg:
                chain_step(f_all[g - 1], len(ys[0]), hs, ys)
            epilogue(g - 1, p_all[g - 1], ys)
    ys = [[] for _ in pairs]
    for c in range(cpg):
        chain_step(f_all[n_groups - 1], c, hs, ys)
    epilogue(n_groups - 1, p_all[n_groups - 1], ys)
    for q in pairs:
        h_scr[q] = hs[q]
    zr_scr[...] = r_ref[0, rows - SUBLANES:rows, :]
    zk_scr[...] = k_ref[0, rows - SUBLANES:rows, :]
    zv_scr[...] = v_ref[0, rows - SUBLANES:rows, :]
    zl_scr[...] = l_ref[0, rows - SUBLANES:rows, :]

    @pl.when(t == pl.num_programs(2) - 1)
    def _():
        for q in pairs:
            s_ref[0, q] = hs[q]


def _rwkv(proj3, prev8, par, wl, g2, s0, *, rows, pps):
    bsz, t_len, _ = proj3.shape
    masks, bfm, ones2, cum = _rwkv_consts()
    wide = pps * LANES
    col = lambda off: (lambda b, p, t: (b, t, off // wide + p))
    pcol = lambda off: (lambda b, p, t: (0, off // wide + p))
    return pl.pallas_call(
        functools.partial(_rwkv_kernel, rows=rows, group=min(RWKV_GROUP, rows), pps=pps),
        out_shape=(jax.ShapeDtypeStruct((bsz, t_len, D_MODEL), F32),
                   jax.ShapeDtypeStruct((bsz, N_PAIRS, LANES, LANES), F32)),
        grid=(bsz, N_PAIRS // pps, t_len // rows),
        in_specs=[
            pl.BlockSpec((1, rows, wide), col(F_R)),
            pl.BlockSpec((1, rows, wide), col(F_K)),
            pl.BlockSpec((1, rows, wide), col(F_V)),
            pl.BlockSpec((1, rows, 2 * LANES), lambda b, p, t: (b, t, F_L // (2 * LANES))),
            pl.BlockSpec((1, rows, wide), col(F_GR)),
            pl.BlockSpec((SUBLANES, wide), pcol(F_R)),
            pl.BlockSpec((SUBLANES, wide), pcol(F_K)),
            pl.BlockSpec((SUBLANES, wide), pcol(F_V)),
            pl.BlockSpec((SUBLANES, 2 * LANES), lambda b, p, t: (0, F_L // (2 * LANES))),
            pl.BlockSpec((pps, _P_ROWS, LANES), lambda b, p, t: (p, 0, 0)),
            pl.BlockSpec((pps, LANES, 2 * LANES), lambda b, p, t: (p, 0, 0)),
            pl.BlockSpec((GATE_LORA, wide), lambda b, p, t: (0, p)),
            pl.BlockSpec((pps, LANES, LANES), lambda b, p, t: (p, 0, 0)),
            _const_spec(masks.shape), _const_spec(bfm.shape), _const_spec(ones2.shape), _const_spec(cum.shape),
        ],
        out_specs=(pl.BlockSpec((1, rows, wide), lambda b, p, t: (b, t, p)),
                   pl.BlockSpec((1, pps, LANES, LANES), lambda b, p, t: (b, p, 0, 0))),
        scratch_shapes=[pltpu.VMEM((pps, LANES, LANES), F32),
                        pltpu.VMEM((SUBLANES, wide), F32), pltpu.VMEM((SUBLANES, wide), F32),
                        pltpu.VMEM((SUBLANES, wide), F32), pltpu.VMEM((SUBLANES, 2 * LANES), F32)],
        compiler_params=pltpu.CompilerParams(
            dimension_semantics=("parallel", "parallel", "arbitrary"), vmem_limit_bytes=VMEM_LIMIT),
        name="rwkv",
    )(proj3, proj3, proj3, proj3, proj3, prev8, prev8, prev8, prev8, par, wl, g2, s0, masks, bfm, ones2, cum)


def _attn_kernel(sink_ref, q_ref, ga_ref, kvc_ref, kvp_ref, kvm_ref, bf_ref, bg_ref, rw_ref, o_ref):
    first = pl.program_id(1) == 0
    lane = lax.broadcasted_iota(jnp.int32, (1, LANES), 1)
    lo_lane = lane < HEAD_DIM
    lo_row = lax.broadcasted_iota(jnp.int32, (LANES, 1), 0) < HEAD_DIM
    kv_w = 2 * LANES
    kc, vc = kvc_ref[0, :, :kv_w], kvc_ref[0, :, kv_w:]
    kv_prev = jnp.where(first, kvm_ref[...], kvp_ref[0])
    k_prev, v_prev = kv_prev[:, :kv_w], kv_prev[:, kv_w:]
    ppk = N_PAIRS // N_KV

    def scores(sb, kh):
        rs = slice(sb * BLOCK, (sb + 1) * BLOCK)
        kp_, vp_ = (k_prev, v_prev) if sb == 0 else (kc[rs.start - BLOCK:rs.start], vc[rs.start - BLOCK:rs.start])
        kx = jnp.concatenate([kp_, kc[rs]], axis=0)[:, kh * LANES:(kh + 1) * LANES]
        vx = jnp.concatenate([vp_, vc[rs]], axis=0)[:, kh * LANES:(kh + 1) * LANES].astype(F32)
        lts = []
        for p in range(kh * ppk, (kh + 1) * ppk):
            qt = q_ref[0, rs, p * LANES:(p + 1) * LANES]
            zq = jnp.zeros_like(qt)
            q2 = jnp.concatenate([jnp.where(lo_lane, qt, zq), jnp.where(lo_lane, zq, qt)], axis=0)
            lts.append(_dg(kx, q2, _NT))
        vxt = jnp.concatenate([vx.T, jnp.ones((SUM_ROWS, 2 * BLOCK), F32)], axis=0).astype(BF16)
        return dict(sb=sb, kh=kh, rs=rs, lts=lts, vxt=vxt)

    def finish(g):
        bias_ref = bf_ref if g["sb"] == 0 else bg_ref
        pes, sinks = [], []
        for j, lt in enumerate(g["lts"]):
            for sub in range(2):
                hd = 2 * (g["kh"] * ppk + j) + sub
                lg = lt[:, sub * LANES:(sub + 1) * LANES] + bias_ref[0, hd]
                m = jnp.maximum(jnp.max(lg, axis=0, keepdims=True), sink_ref[hd])
                sinks.append(jnp.exp2(sink_ref[hd] - m))
                pes.append(jnp.exp2(lg - m).astype(BF16))
        ots = [_dg(g["vxt"], jnp.concatenate(pes[2 * j:2 * j + 2], axis=1)) for j in range(ppk)]
        for j, ot in enumerate(ots):
            cs = slice((g["kh"] * ppk + j) * LANES, (g["kh"] * ppk + j + 1) * LANES)
            inv0 = 1.0 / (ot[LANES:LANES + 1, :LANES] + sinks[2 * j])
            inv1 = 1.0 / (ot[LANES:LANES + 1, LANES:] + sinks[2 * j + 1])
            att = jnp.where(lo_row, ot[:LANES, :LANES] * inv0, ot[:LANES, LANES:] * inv1).T
            gate = 1.0 / (1.0 + jnp.exp(-ga_ref[0, g["rs"], cs]))
            o_ref[0, g["rs"], cs] = (gate * att + rw_ref[0, g["rs"], cs]).astype(BF16)

    order = [(sb, kh) for sb in range(ATT_ROWS // BLOCK) for kh in range(N_KV)]
    cur = scores(*order[0])
    for nxt in order[1:]:
        ahead = scores(*nxt)
        finish(cur)
        cur = ahead
    finish(cur)


def _attn_merge(proj16, proj32, kvm, bias, sinks, rwg):
    bsz, t_len, _ = proj16.shape
    kv_w = 4 * LANES
    assert B_VX == B_KX + kv_w // 2 and B_KX % kv_w == 0
    per = ATT_ROWS // BLOCK
    rows = lambda w, col: pl.BlockSpec((1, ATT_ROWS, w), lambda b, n: (b, n, col // w))
    return pl.pallas_call(
        _attn_kernel,
        out_shape=jax.ShapeDtypeStruct((bsz, t_len, D_MODEL), BF16),
        grid=(bsz, t_len // ATT_ROWS),
        in_specs=[
            pl.BlockSpec(memory_space=pltpu.SMEM),
            rows(D_MODEL, B_Q), rows(D_MODEL, F_GA), rows(kv_w, B_KX),
            pl.BlockSpec((1, BLOCK, kv_w), lambda b, n: (b, jnp.maximum(n * per - 1, 0), B_KX // kv_w)),
            pl.BlockSpec((BLOCK, kv_w), lambda b, n: (0, B_KX // kv_w)),
            pl.BlockSpec((1, N_HEADS, 2 * BLOCK, BLOCK), lambda b, n: (jnp.minimum(n, 1), 0, 0, 0)),
            pl.BlockSpec((1, N_HEADS, 2 * BLOCK, BLOCK), lambda b, n: (1, 0, 0, 0)),
            rows(D_MODEL, 0),
        ],
        out_specs=rows(D_MODEL, 0),
        compiler_params=pltpu.CompilerParams(
            dimension_semantics=("parallel", "arbitrary"), vmem_limit_bytes=VMEM_LIMIT),
        name="attn_merge",
    )(sinks, proj16, proj32, proj16, proj16, kvm, bias, bias, rwg)


def _out_ffn_kernel(x_ref, mg_ref, ln_ref, wo_ref, w1_ref, w2_ref, o_ref, *, ff_tile, parts):
    ln = ln_ref[...]
    tm = x_ref.shape[0]
    rs = [slice(i * tm // parts, (i + 1) * tm // parts) for i in range(parts)]
    proj = [_dg(mg_ref[r, :], wo_ref[...]) for r in rs]
    h1 = [_layer_norm(ALPHA * _layer_norm(x_ref[r, :], ln[0:1], ln[1:2]) + p, ln[2:3], ln[3:4]) for r, p in zip(rs, proj)]
    for r, h in zip(rs, h1):
        hb = h.astype(BF16)
        acc = ALPHA * h
        for j in range(D_FF // ff_tile):
            cs = slice(j * ff_tile, (j + 1) * ff_tile)
            u = jnp.maximum(_dg(hb, w1_ref[:, cs]), 0.0)
            acc = acc + _dg((u * u).astype(BF16), w2_ref[cs, :])
        o_ref[r, :] = _layer_norm(acc, ln[4:5], ln[5:6])


def _out_ffn(x2, merged2, ln_par, wo, w1, w2, *, tm, ff_tile=1024, parts=2):
    m = x2.shape[0]
    return pl.pallas_call(
        functools.partial(_out_ffn_kernel, ff_tile=ff_tile, parts=parts),
        out_shape=jax.ShapeDtypeStruct((m, D_MODEL), F32),
        grid=(m // tm,),
        in_specs=[
            pl.BlockSpec((tm, D_MODEL), lambda i: (i, 0)),
            pl.BlockSpec((tm, D_MODEL), lambda i: (i, 0)),
            _const_spec((SUBLANES, D_MODEL)),
            _const_spec((D_MODEL, D_MODEL)),
            _const_spec((D_MODEL, D_FF)),
            _const_spec((D_FF, D_MODEL)),
        ],
        out_specs=pl.BlockSpec((tm, D_MODEL), lambda i: (i, 0)),
        compiler_params=pltpu.CompilerParams(
            dimension_semantics=("parallel",), vmem_limit_bytes=VMEM_LIMIT),
        name="out_ffn",
    )(x2, merged2, ln_par, wo, w1, w2)


def kernel(x, meta_tokens, ln0_g, ln0_b, rel_bias, w_in, shift_mu, attn_sinks, decay_w0, decay_w2, iclr_a0, iclr_a2, gate_w2, k_k, k_a, r_k, lnx_g, lnx_b, w_out, ln1_g, ln1_b, w_ff1, w_ff2, ln2_g, ln2_b):
    bsz, seq, _ = x.shape
    assert D_MODEL == x.shape[2] and seq % RWKV_ROWS == 0 and seq % ATT_ROWS == 0
    assert (bsz * seq) % PROJ_ROWS == 0 and (bsz * seq) % FFN_ROWS == 0
    W = D_MODEL
    wi = w_in[0]
    w_perm = wi.astype(BF16)
    assert w_perm.shape[1] == IN_COLS
    mu = shift_mu[0]
    rows_ = [mu[:W], mu[W:2 * W], mu[2 * W:3 * W]]
    vec = lambda a: a.reshape(N_PAIRS, 1, LANES)
    par = jnp.concatenate(
        [vec(rows_[0]), vec(rows_[1]), vec(rows_[2]),
         jnp.broadcast_to(mu[3 * W:3 * W + LANES].reshape(1, 1, LANES), (N_PAIRS, 1, LANES)),
         jnp.broadcast_to(mu[3 * W + LANES:].reshape(1, 1, LANES), (N_PAIRS, 1, LANES)),
         vec(decay_w0[0]), vec(iclr_a0[0]), vec(k_k[0]), vec(k_a[0]), vec(r_k[0].reshape(-1)),
         vec(lnx_g[0]), vec(lnx_b[0]),
         jnp.zeros((N_PAIRS, _P_ROWS - 12, LANES), F32)], axis=1).astype(F32)
    w2p = decay_w2[0].reshape(DECAY_LORA, N_PAIRS, LANES).transpose(1, 0, 2)
    a2p = iclr_a2[0].reshape(ICLR_LORA, N_PAIRS, LANES).transpose(1, 0, 2)
    z = jnp.zeros_like(w2p)
    wl = jnp.concatenate([jnp.concatenate([w2p, z], axis=2), jnp.concatenate([z, a2p], axis=2)], axis=1).astype(BF16)
    g2 = gate_w2[0].astype(BF16)
    ln_par = jnp.stack([ln0_g, ln0_b, ln1_g[0], ln1_b[0], ln2_g[0], ln2_b[0],
                        jnp.zeros_like(ln0_g), jnp.zeros_like(ln0_g)]).astype(F32)
    g0 = ln0_g.reshape(1, W).astype(F32)
    b0 = ln0_b.reshape(1, W).astype(F32)

    bias = _bias_table(rel_bias)

    meta_blk = jnp.concatenate([jnp.zeros((PAD_ROWS, W), F32), meta_tokens.astype(F32)], axis=0)
    proj_m16, proj_m32 = _ln_proj(meta_blk, g0, b0, w_perm, tm=BLOCK, n_pad=PAD_ROWS)
    zeros8 = jnp.zeros((SUBLANES, F_COLS), F32)
    s_zero = jnp.zeros((N_PAIRS, LANES, LANES), F32)
    _, s_meta = _rwkv(proj_m32[None], zeros8, par, wl, g2, s_zero, rows=BLOCK, pps=N_PAIRS)
    prev8 = proj_m32[BLOCK - SUBLANES:]

    x2 = x.reshape(bsz * seq, W)
    proj16, proj32 = _ln_proj(x2, g0, b0, w_perm, tm=PROJ_ROWS)
    proj32 = proj32.reshape(bsz, seq, F_COLS)
    rwg, _ = _rwkv(proj32, prev8, par, wl, g2, s_meta[0], rows=RWKV_ROWS, pps=PAIRS_PER_STEP)
    merged = _attn_merge(proj16.reshape(bsz, seq, B_COLS), proj32, proj_m16, bias,
                         attn_sinks[0].astype(F32) * LOG2E, rwg)
    out = _out_ffn(x2, merged.reshape(bsz * seq, W), ln_par, w_out[0].astype(BF16),
                   w_ff1[0].astype(BF16), w_ff2[0].astype(BF16), tm=FFN_ROWS)
    return out.reshape(bsz, seq, W)
```

```python
import functools
import math

import numpy as np
import jax
import jax.numpy as jnp
from jax import lax
from jax.experimental import pallas as pl
from jax.experimental.pallas import tpu as pltpu

F32 = jnp.float32
BF16 = jnp.bfloat16

D_MODEL = 1024
N_META = 16
HEAD_DIM = 64
N_HEADS = D_MODEL // HEAD_DIM
N_KV = 2
BLOCK = 128
N_BUCKETS = 32
MAX_EXACT = 16
MAX_DISTANCE = 128
DECAY_LORA = 64
ICLR_LORA = 64
GATE_LORA = 128
D_FF = 4 * D_MODEL
LN_EPS = 1e-5
GN_EPS = 1e-5 * HEAD_DIM
DEPTH = 1
ALPHA = (2.0 * DEPTH) ** 0.25
LOG2E = math.log2(math.e)

LANES = 128
SUBLANES = 8
N_PAIRS = D_MODEL // LANES
CHUNK = 64
RWKV_ROWS = 2048
RWKV_GROUP = 256
PAIRS_PER_STEP = 2
PROJ_ROWS = 512
FFN_ROWS = 512
PAD_ROWS = BLOCK - N_META
ATT_ROWS = 8 * BLOCK
SUM_ROWS = 16

B_Q = 0
B_KX = 1024
B_VX = 1280
B_COLS = 1536
F_GA = 0
F_R = 1024
F_K = 2048
F_V = 3072
F_L = 4096
F_GR = 4352
F_COLS = 5376

IN_Q = 0
IN_K = 1024
IN_ZR = 1280
IN_GA = 4608
IN_GR = 5632
IN_COLS = 6656
_PROJ_SEGMENTS = ((IN_GA, D_MODEL, F_GA), (IN_ZR, F_GR - F_R, F_R), (IN_GR, D_MODEL, F_GR))
Q_SCALE = HEAD_DIM ** -0.5 * LOG2E

VMEM_LIMIT = 56 * 1024 * 1024

_NN = (((1,), (0,)), ((), ()))
_NT = (((1,), (1,)), ((), ()))


def _dg(a, b, dims=_NN):
    return lax.dot_general(a, b, dims, preferred_element_type=F32)


def _split(x):
    hi = x.astype(BF16)
    lo = (x - hi.astype(F32)).astype(BF16)
    return hi, lo


def _mm1(a, b, dims=_NN):
    return _dg(a.astype(BF16), b.astype(BF16), dims)


def _layer_norm(x, g, b):
    mu = jnp.mean(x, axis=-1, keepdims=True)
    xc = x - mu
    var = jnp.mean(xc * xc, axis=-1, keepdims=True)
    return xc * lax.rsqrt(var + LN_EPS) * g + b


def _const_spec(shape):
    return pl.BlockSpec(shape, lambda *_: (0,) * len(shape), pipeline_mode=pl.Buffered(1))


def _bias_kernel(bucket_ref, rel_ref, o_ref):
    first, bk = bucket_ref[0], bucket_ref[1]
    for h in range(N_HEADS):
        acc = jnp.where(bk < 0, -jnp.inf, 0.0).astype(F32)
        for b in range(N_BUCKETS):
            acc = jnp.where(bk == b, rel_ref[b, h] * LOG2E, acc)
        o_ref[1, h] = acc
        o_ref[0, h] = jnp.where(first < 0, -jnp.inf, acc)


def _bucket_table():
    q = np.arange(BLOCK)[:, None]
    s = np.arange(2 * BLOCK)[None, :]
    dist = q + BLOCK - s
    in_window = (dist >= 0) & (dist < BLOCK)
    d0 = np.maximum(dist, 0)
    d = np.maximum(d0, 1).astype(np.float32)
    large = MAX_EXACT + (np.log(d / np.float32(MAX_EXACT)) / np.float32(math.log(MAX_DISTANCE / MAX_EXACT))
                         * (N_BUCKETS - MAX_EXACT)).astype(np.int32)
    large = np.minimum(large, N_BUCKETS - 1)
    bucket = np.where(d0 < MAX_EXACT, d0, large).astype(np.int32)
    general = np.where(in_window, bucket, -1)
    first = np.where(in_window & (s >= PAD_ROWS), bucket, -1)
    return np.stack([first.T, general.T]).astype(np.int32)


def _bias_table(rel_bias):
    return pl.pallas_call(
        _bias_kernel,
        out_shape=jax.ShapeDtypeStruct((2, N_HEADS, 2 * BLOCK, BLOCK), F32),
        in_specs=[pl.BlockSpec(memory_space=pltpu.VMEM), pl.BlockSpec(memory_space=pltpu.SMEM)],
        out_specs=pl.BlockSpec(memory_space=pltpu.VMEM),
        name="bias_table",
    )(jnp.asarray(_bucket_table()), rel_bias.astype(F32))


def _ln_proj_kernel(x_ref, g_ref, b_ref, w_ref, o16_ref, o32_ref, *, n_pad, tn):
    tm = x_ref.shape[0]
    halves = [slice(0, tm // 2), slice(tm // 2, tm)]
    ybs = []
    for h_ in halves:
        y = _layer_norm(x_ref[h_, :], g_ref[...], b_ref[...])
        if n_pad:
            row = lax.broadcasted_iota(jnp.int32, y.shape, 0) + h_.start
            y = jnp.where(row < n_pad, 0.0, y)
        ybs.append(y.astype(BF16))
    lo_lane = lax.broadcasted_iota(jnp.int32, (1, LANES), 1) < HEAD_DIM

    def dup(z):
        zr = pltpu.roll(z, HEAD_DIM, 1)
        return jnp.concatenate([jnp.where(lo_lane, z, zr), jnp.where(lo_lane, zr, z)], axis=1)

    for j in range(0, D_MODEL, tn):
        for h_, yh in zip(halves, ybs):
            o16_ref[h_, B_Q + j:B_Q + j + tn] = (_dg(yh, w_ref[:, IN_Q + j:IN_Q + j + tn]) * Q_SCALE).astype(BF16)
    yb = jnp.concatenate(ybs, axis=0)
    kv = _dg(yb, w_ref[:, IN_K:IN_K + 2 * LANES])
    o16_ref[:, B_KX:B_KX + 2 * LANES] = dup(kv[:, :LANES]).astype(BF16)
    o16_ref[:, B_VX:B_VX + 2 * LANES] = dup(kv[:, LANES:]).astype(BF16)
    for src, width, dst in _PROJ_SEGMENTS:
        for j in range(0, width, tn):
            w_ = min(tn, width - j)
            o32_ref[:, dst + j:dst + j + w_] = _dg(yb, w_ref[:, src + j:src + j + w_])


def _ln_proj(x2, g, b, w, *, tm, tn=512, n_pad=0):
    m = x2.shape[0]
    return pl.pallas_call(
        functools.partial(_ln_proj_kernel, n_pad=n_pad, tn=tn),
        out_shape=(jax.ShapeDtypeStruct((m, B_COLS), BF16), jax.ShapeDtypeStruct((m, F_COLS), F32)),
        grid=(m // tm,),
        in_specs=[
            pl.BlockSpec((tm, D_MODEL), lambda i: (i, 0)),
            _const_spec((1, D_MODEL)),
            _const_spec((1, D_MODEL)),
            _const_spec((D_MODEL, IN_COLS)),
        ],
        out_specs=(pl.BlockSpec((tm, B_COLS), lambda i: (i, 0)), pl.BlockSpec((tm, F_COLS), lambda i: (i, 0))),
        compiler_params=pltpu.CompilerParams(
            dimension_semantics=("parallel",), vmem_limit_bytes=VMEM_LIMIT),
        name="ln_proj",
    )(x2, g, b, w)


(_P_MU_R, _P_MU_K, _P_MU_V, _P_MU_WA, _P_MU_G, _P_W0, _P_A0, _P_KK, _P_KA, _P_RK, _P_LNG, _P_LNB) = range(12)
_P_ROWS = 16


def _rwkv_consts():
    r2, c2 = np.indices((LANES, LANES))
    same = (r2 < HEAD_DIM) == (c2 < HEAD_DIM)
    low = (r2 % HEAD_DIM) > (c2 % HEAD_DIM)
    masks = np.stack([low & same,
                      low & ~same,
                      (r2 % HEAD_DIM) >= (c2 % HEAD_DIM),
                      same, ~same, r2 == c2]).astype(np.float32)
    bf = np.stack([same.astype(np.float32), same.astype(np.float32) / HEAD_DIM])
    ones2 = np.kron(np.eye(2, dtype=np.float32), same.astype(np.float32))
    t, s = np.indices((CHUNK, CHUNK))
    return (jnp.asarray(masks), jnp.asarray(bf, dtype=BF16), jnp.asarray(ones2, dtype=BF16),
            jnp.asarray((t >= s).astype(np.float32), dtype=BF16))


def _rwkv_kernel(r_ref, k_ref, v_ref, l_ref, gr_ref, pr_ref, pk_ref, pv_ref, plr_ref,
                 par_ref, wl_ref, g2_ref, s0_ref, msk_ref, bfm_ref, ones2_ref, cum_ref, o_ref, s_ref,
                 h_scr, zr_scr, zk_scr, zv_scr, zl_scr, *, rows, group, pps):
    t = pl.program_id(2)

    @pl.when(t == 0)
    def _():
        h_scr[...] = s0_ref[...]
        zr_scr[...] = pr_ref[...]
        zk_scr[...] = pk_ref[...]
        zv_scr[...] = pv_ref[...]
        zl_scr[...] = plr_ref[...]

    cat = lambda parts, axis=0: jnp.concatenate(parts, axis=axis)
    pairs = range(pps)
    prow = lambda q, i: par_ref[q, i:i + 1, :]
    prow2 = lambda i: cat([prow(q, i) for q in pairs], 1)
    psl = [slice(q * LANES, (q + 1) * LANES) for q in pairs]
    row_id = lax.broadcasted_iota(jnp.int32, (group, 1), 0)
    lane = lax.broadcasted_iota(jnp.int32, (1, LANES), 1)
    lo_half = lane < HEAD_DIM
    m0 = lo_half.astype(F32)
    m1 = 1.0 - m0
    mask_p, mask_a, tri_i = (msk_ref[i].astype(BF16) for i in range(3))
    head_diag, head_anti, eye = (msk_ref[i] for i in range(3, 6))
    avg_bd = bfm_ref[1]
    ones2 = ones2_ref[...]
    cum = cum_ref[...]
    mu_l = cat([prow(0, _P_MU_WA), prow(0, _P_MU_G)], 1)
    cpg = group // CHUNK
    n_groups = rows // group
    csl = [slice(c * CHUNK, (c + 1) * CHUNK) for c in range(cpg)]
    zeros_q = jnp.zeros((CHUNK, LANES), BF16)

    def shifted(ref, scr, mu, g):
        z = ref[0, g * group:(g + 1) * group, :]
        prev_row = scr[SUBLANES - 1:SUBLANES, :] if g == 0 else ref[0, g * group - 1:g * group, :]
        prev = jnp.where(row_id == 0, prev_row, pltpu.roll(z, 1, 0))
        return z + (prev - z) * mu

    def prep(g, out):
        r2 = shifted(r_ref, zr_scr, prow2(_P_MU_R), g)
        k2_ = shifted(k_ref, zk_scr, prow2(_P_MU_K), g)
        v2 = shifted(v_ref, zv_scr, prow2(_P_MU_V), g)
        yield
        zl = shifted(l_ref, zl_scr, mu_l, g)
        wa, gl = zl[:, :LANES], zl[:, LANES:]
        lora_in = jnp.where(lane < DECAY_LORA, jnp.tanh(wa), wa).astype(BF16)
        gate_all = _dg((1.0 / (1.0 + jnp.exp(-gl))).astype(BF16), g2_ref[...])
        yield
        for q in pairs:
            r, k, v = r2[:, psl[q]], k2_[:, psl[q]], v2[:, psl[q]]
            lo = _dg(lora_in, wl_ref[q])
            gate = gate_all[:, psl[q]]
            kk = k * prow(q, _P_KK)
            yield
            wpre = -(prow(q, _P_W0) + lo[:, :LANES])
            w = -(jnp.maximum(wpre, 0.0) + jnp.log(1.0 + jnp.exp(-jnp.abs(wpre)))) - 0.5
            logw = -jnp.exp(w)
            hi, lw = _split(logw)
            hl = cat([hi, lw], 1)
            lcs = [_dg(cum, hl[s]) for s in csl]
            a = 1.0 / (1.0 + jnp.exp(-(prow(q, _P_A0) + lo[:, LANES:])))
            k2 = k * (1.0 + (a - 1.0) * prow(q, _P_KA))
            sums = _dg(cat([kk * kk, r * k2 * prow(q, _P_RK)], 1).astype(BF16), ones2)
            yield
            kk = kk / jnp.maximum(jnp.sqrt(sums[:, :LANES]), 1e-12)
            lc = cat([x[:, :LANES] + x[:, LANES:] for x in lcs])
            e_in, e_neg = jnp.exp(lc), jnp.exp(-lc)
            yield
            rt_ = r * e_in
            at_ = -kk * jnp.exp(lc - logw)
            v_sw = pltpu.roll(v, HEAD_DIM, 1)
            kt_, bt_ = k2 * e_neg, kk * a * e_neg
            gate = gate / (1.0 + jnp.exp(-gr_ref[0, g * group:(g + 1) * group, psl[q]]))
            out.append(dict(q=q, v=v, gate=gate, bonus=sums[:, LANES:] * v, e_in=e_in, rt=rt_, at=at_,
                            kt0=kt_ * m0, kt1=kt_ * m1, bt0=bt_ * m0, bt1=bt_ * m1,
                            vs0=(v_sw * m0).astype(BF16), vs1=(v_sw * m1).astype(BF16)))
            yield

    def phase1(pp, out):
        units = [(p_, s) for p_ in pp for s in csl]
        get = lambda n: [p_[n][s] for p_, s in units]
        at_, rt_, bt0, bt1, kt0, kt1, vs0, vs1 = (get(n) for n in ("at", "rt", "bt0", "bt1", "kt0", "kt1", "vs0", "vs1"))
        gcs = [p_["e_in"][s.stop - 1:s.stop, :] for p_, s in units]
        d_ab = [_mm1(cat([a_, r_]), cat([b0, k0, k1, b1]), _NT)
                for a_, r_, b0, k0, k1, b1 in zip(at_, rt_, bt0, kt0, kt1, bt1)]
        yield
        top = [cat([d[:CHUNK, :LANES], d[:CHUNK, LANES:]]).astype(BF16) for d in d_ab]
        bot = [cat([d[CHUNK:, :LANES], d[CHUNK:, LANES:]]).astype(BF16) * tri_i for d in d_ab]
        p = [tp * mask_p for tp in top]
        xb = [(cat([a_ * m0, a_ * m1]) + _dg(tp * mask_a, cat([w0, w1]))).astype(BF16)
              for a_, tp, w0, w1 in zip(at_, top, vs0, vs1)]
        yield
        for i in range(6):
            if i < 5:
                px = [_dg(pw, cat([x, pw], 1)) for x, pw in zip(xb, p)]
                p = [d[:, LANES:].astype(BF16) for d in px]
                xb = [x + d[:, :LANES].astype(BF16) for x, d in zip(xb, px)]
            else:
                xb = [x + _dg(pw, x).astype(BF16) for x, pw in zip(xb, p)]
            yield
        gst = [cat([x[:CHUNK], w1, w0, x[CHUNK:]]) for x, w0, w1 in zip(xb, vs0, vs1)]
        lhs45 = []
        for b_, b0, k0, k1, b1, gc in zip(bot, bt0, kt0, kt1, bt1, gcs):
            lhs45.append(cat([cat([b_[:CHUNK], zeros_q], 1), cat([zeros_q, b_[CHUNK:]], 1),
                              cat([b0 * gc, k0 * gc, k1 * gc, b1 * gc]).T.astype(BF16)]))
        o45 = [_dg(l_, g_) for l_, g_ in zip(lhs45, gst)]
        yield
        out["lhs"] = [cat([(r_ + jnp.where(lo_half, o[:CHUNK], o[CHUNK:LANES])).astype(BF16),
                           (o[LANES:] * head_diag + eye * gc).astype(BF16)])
                      for r_, o, gc in zip(rt_, o45, gcs)]
        out["y_in"] = [jnp.where(lo_half, o[CHUNK:LANES], o[:CHUNK]) for o in o45]
        out["g_c"] = [o[LANES:] * head_anti for o in o45]

    def chain_step(f, c, hs, ys):
        for q in pairs:
            u = q * cpg + c
            yh = _dg(f["lhs"][u], hs[q].astype(BF16))
            ys[q].append(yh[:CHUNK] + f["y_in"][u])
            hs[q] = yh[CHUNK:] + f["g_c"][u]

    def epilogue(g, pp, ys):
        gsl = slice(g * group, (g + 1) * group)
        for p_ in pp:
            q = p_["q"]
            y = pltpu.roll(cat(ys[q]), HEAD_DIM, 1)
            yc = y - _dg(y.astype(BF16), avg_bd)
            yv = _dg((yc * yc).astype(BF16), avg_bd)
            yn = yc * lax.rsqrt(yv + GN_EPS) * prow(q, _P_LNG) + prow(q, _P_LNB)
            o_ref[0, gsl, psl[q]] = (yn + p_["bonus"]) * p_["gate"]

    n_stages, n_pieces = 9, 2 + 4 * pps
    hs = [h_scr[q] for q in pairs]
    p_all = [[] for _ in range(n_groups)]
    f_all = [{} for _ in range(n_groups)]
    for _ in prep(0, p_all[0]):
        pass
    for g in range(n_groups):
        side = prep(g + 1, p_all[g + 1]) if g + 1 < n_groups else iter(())
        ys = [[] for _ in pairs]
        done = 0
        for i, _ in enumerate(phase1(p_all[g], f_all[g])):
            while done * n_stages < (i + 1) * n_pieces:
                next(side, None)
                done += 1
            if g > 0 and i % 2 == 1 and len(ys[0]) < cpg:
                chain_step(f_all[g - 1], len(ys[0]), hs, ys)
        for _ in side:
            pass
        if g > 0:
            while len(ys[0]) < cpg:
                chain_step(f_all[g - 1], len(ys[0]), hs, ys)
            epilogue(g - 1, p_all[g - 1], ys)
    ys = [[] for _ in pairs]
    for c in range(cpg):
        chain_step(f_all[n_groups - 1], c, hs, ys)
    epilogue(n_groups - 1, p_all[n_groups - 1], ys)
    for q in pairs:
        h_scr[q] = hs[q]
    zr_scr[...] = r_ref[0, rows - SUBLANES:rows, :]
    zk_scr[...] = k_ref[0, rows - SUBLANES:rows, :]
    zv_scr[...] = v_ref[0, rows - SUBLANES:rows, :]
    zl_scr[...] = l_ref[0, rows - SUBLANES:rows, :]

    @pl.when(t == pl.num_programs(2) - 1)
    def _():
        for q in pairs:
            s_ref[0, q] = hs[q]


def _rwkv(proj3, prev8, par, wl, g2, s0, *, rows, pps):
    bsz, t_len, _ = proj3.shape
    masks, bfm, ones2, cum = _rwkv_consts()
    wide = pps * LANES
    col = lambda off: (lambda b, p, t: (b, t, off // wide + p))
    pcol = lambda off: (lambda b, p, t: (0, off // wide + p))
    return pl.pallas_call(
        functools.partial(_rwkv_kernel, rows=rows, group=min(RWKV_GROUP, rows), pps=pps),
        out_shape=(jax.ShapeDtypeStruct((bsz, t_len, D_MODEL), F32),
                   jax.ShapeDtypeStruct((bsz, N_PAIRS, LANES, LANES), F32)),
        grid=(bsz, N_PAIRS // pps, t_len // rows),
        in_specs=[
            pl.BlockSpec((1, rows, wide), col(F_R)),
            pl.BlockSpec((1, rows, wide), col(F_K)),
            pl.BlockSpec((1, rows, wide), col(F_V)),
            pl.BlockSpec((1, rows, 2 * LANES), lambda b, p, t: (b, t, F_L // (2 * LANES))),
            pl.BlockSpec((1, rows, wide), col(F_GR)),
            pl.BlockSpec((SUBLANES, wide), pcol(F_R)),
            pl.BlockSpec((SUBLANES, wide), pcol(F_K)),
            pl.BlockSpec((SUBLANES, wide), pcol(F_V)),
            pl.BlockSpec((SUBLANES, 2 * LANES), lambda b, p, t: (0, F_L // (2 * LANES))),
            pl.BlockSpec((pps, _P_ROWS, LANES), lambda b, p, t: (p, 0, 0)),
            pl.BlockSpec((pps, LANES, 2 * LANES), lambda b, p, t: (p, 0, 0)),
            pl.BlockSpec((GATE_LORA, wide), lambda b, p, t: (0, p)),
            pl.BlockSpec((pps, LANES, LANES), lambda b, p, t: (p, 0, 0)),
            _const_spec(masks.shape), _const_spec(bfm.shape), _const_spec(ones2.shape), _const_spec(cum.shape),
        ],
        out_specs=(pl.BlockSpec((1, rows, wide), lambda b, p, t: (b, t, p)),
                   pl.BlockSpec((1, pps, LANES, LANES), lambda b, p, t: (b, p, 0, 0))),
        scratch_shapes=[pltpu.VMEM((pps, LANES, LANES), F32),
                        pltpu.VMEM((SUBLANES, wide), F32), pltpu.VMEM((SUBLANES, wide), F32),
                        pltpu.VMEM((SUBLANES, wide), F32), pltpu.VMEM((SUBLANES, 2 * LANES), F32)],
        compiler_params=pltpu.CompilerParams(
            dimension_semantics=("parallel", "parallel", "arbitrary"), vmem_limit_bytes=VMEM_LIMIT),
        name="rwkv",
    )(proj3, proj3, proj3, proj3, proj3, prev8, prev8, prev8, prev8, par, wl, g2, s0, masks, bfm, ones2, cum)


def _attn_kernel(sink_ref, q_ref, ga_ref, kvc_ref, kvp_ref, kvm_ref, bf_ref, bg_ref, rw_ref, o_ref):
    first = pl.program_id(1) == 0
    lane = lax.broadcasted_iota(jnp.int32, (1, LANES), 1)
    lo_lane = lane < HEAD_DIM
    lo_row = lax.broadcasted_iota(jnp.int32, (LANES, 1), 0) < HEAD_DIM
    kv_w = 2 * LANES
    kc, vc = kvc_ref[0, :, :kv_w], kvc_ref[0, :, kv_w:]
    kv_prev = jnp.where(first, kvm_ref[...], kvp_ref[0])
    k_prev, v_prev = kv_prev[:, :kv_w], kv_prev[:, kv_w:]
    ppk = N_PAIRS // N_KV

    def scores(sb, kh):
        rs = slice(sb * BLOCK, (sb + 1) * BLOCK)
        kp_, vp_ = (k_prev, v_prev) if sb == 0 else (kc[rs.start - BLOCK:rs.start], vc[rs.start - BLOCK:rs.start])
        kx = jnp.concatenate([kp_, kc[rs]], axis=0)[:, kh * LANES:(kh + 1) * LANES]
        vx = jnp.concatenate([vp_, vc[rs]], axis=0)[:, kh * LANES:(kh + 1) * LANES].astype(F32)
        lts = []
        for p in range(kh * ppk, (kh + 1) * ppk):
            qt = q_ref[0, rs, p * LANES:(p + 1) * LANES]
            zq = jnp.zeros_like(qt)
            q2 = jnp.concatenate([jnp.where(lo_lane, qt, zq), jnp.where(lo_lane, zq, qt)], axis=0)
            lts.append(_dg(kx, q2, _NT))
        vxt = jnp.concatenate([vx.T, jnp.ones((SUM_ROWS, 2 * BLOCK), F32)], axis=0).astype(BF16)
        return dict(sb=sb, kh=kh, rs=rs, lts=lts, vxt=vxt)

    def finish(g):
        bias_ref = bf_ref if g["sb"] == 0 else bg_ref
        pes, sinks = [], []
        for j, lt in enumerate(g["lts"]):
            for sub in range(2):
                hd = 2 * (g["kh"] * ppk + j) + sub
                lg = lt[:, sub * LANES:(sub + 1) * LANES] + bias_ref[0, hd]
                m = jnp.maximum(jnp.max(lg, axis=0, keepdims=True), sink_ref[hd])
                sinks.append(jnp.exp2(sink_ref[hd] - m))
                pes.append(jnp.exp2(lg - m).astype(BF16))
        ots = [_dg(g["vxt"], jnp.concatenate(pes[2 * j:2 * j + 2], axis=1)) for j in range(ppk)]
        for j, ot in enumerate(ots):
            cs = slice((g["kh"] * ppk + j) * LANES, (g["kh"] * ppk + j + 1) * LANES)
            inv0 = 1.0 / (ot[LANES:LANES + 1, :LANES] + sinks[2 * j])
            inv1 = 1.0 / (ot[LANES:LANES + 1, LANES:] + sinks[2 * j + 1])
            att = jnp.where(lo_row, ot[:LANES, :LANES] * inv0, ot[:LANES, LANES:] * inv1).T
            gate = 1.0 / (1.0 + jnp.exp(-ga_ref[0, g["rs"], cs]))
            o_ref[0, g["rs"], cs] = (gate * att + rw_ref[0, g["rs"], cs]).astype(BF16)

    order = [(sb, kh) for sb in range(ATT_ROWS // BLOCK) for kh in range(N_KV)]
    cur = scores(*order[0])
    for nxt in order[1:]:
        ahead = scores(*nxt)
        finish(cur)
        cur = ahead
    finish(cur)


def _attn_merge(proj16, proj32, kvm, bias, sinks, rwg):
    bsz, t_len, _ = proj16.shape
    kv_w = 4 * LANES
    assert B_VX == B_KX + kv_w // 2 and B_KX % kv_w == 0
    per = ATT_ROWS // BLOCK
    rows = lambda w, col: pl.BlockSpec((1, ATT_ROWS, w), lambda b, n: (b, n, col // w))
    return pl.pallas_call(
        _attn_kernel,
        out_shape=jax.ShapeDtypeStruct((bsz, t_len, D_MODEL), BF16),
        grid=(bsz, t_len // ATT_ROWS),
        in_specs=[
            pl.BlockSpec(memory_space=pltpu.SMEM),
            rows(D_MODEL, B_Q), rows(D_MODEL, F_GA), rows(kv_w, B_KX),
            pl.BlockSpec((1, BLOCK, kv_w), lambda b, n: (b, jnp.maximum(n * per - 1, 0), B_KX // kv_w)),
            pl.BlockSpec((BLOCK, kv_w), lambda b, n: (0, B_KX // kv_w)),
            pl.BlockSpec((1, N_HEADS, 2 * BLOCK, BLOCK), lambda b, n: (jnp.minimum(n, 1), 0, 0, 0)),
            pl.BlockSpec((1, N_HEADS, 2 * BLOCK, BLOCK), lambda b, n: (1, 0, 0, 0)),
            rows(D_MODEL, 0),
        ],
        out_specs=rows(D_MODEL, 0),
        compiler_params=pltpu.CompilerParams(
            dimension_semantics=("parallel", "arbitrary"), vmem_limit_bytes=VMEM_LIMIT),
        name="attn_merge",
    )(sinks, proj16, proj32, proj16, proj16, kvm, bias, bias, rwg)


def _out_ffn_kernel(x_ref, mg_ref, ln_ref, wo_ref, w1_ref, w2_ref, o_ref, *, ff_tile, parts):
    ln = ln_ref[...]
    tm = x_ref.shape[0]
    rs = [slice(i * tm // parts, (i + 1) * tm // parts) for i in range(parts)]
    proj = [_dg(mg_ref[r, :], wo_ref[...]) for r in rs]
    h1 = [_layer_norm(ALPHA * _layer_norm(x_ref[r, :], ln[0:1], ln[1:2]) + p, ln[2:3], ln[3:4]) for r, p in zip(rs, proj)]
    for r, h in zip(rs, h1):
        hb = h.astype(BF16)
        acc = ALPHA * h
        for j in range(D_FF // ff_tile):
            cs = slice(j * ff_tile, (j + 1) * ff_tile)
            u = jnp.maximum(_dg(hb, w1_ref[:, cs]), 0.0)
            acc = acc + _dg((u * u).astype(BF16), w2_ref[cs, :])
        o_ref[r, :] = _layer_norm(acc, ln[4:5], ln[5:6])


def _out_ffn(x2, merged2, ln_par, wo, w1, w2, *, tm, ff_tile=1024, parts=2):
    m = x2.shape[0]
    return pl.pallas_call(
        functools.partial(_out_ffn_kernel, ff_tile=ff_tile, parts=parts),
        out_shape=jax.ShapeDtypeStruct((m, D_MODEL), F32),
        grid=(m // tm,),
        in_specs=[
            pl.BlockSpec((tm, D_MODEL), lambda i: (i, 0)),
            pl.BlockSpec((tm, D_MODEL), lambda i: (i, 0)),
            _const_spec((SUBLANES, D_MODEL)),
            _const_spec((D_MODEL, D_MODEL)),
            _const_spec((D_MODEL, D_FF)),
            _const_spec((D_FF, D_MODEL)),
        ],
        out_specs=pl.BlockSpec((tm, D_MODEL), lambda i: (i, 0)),
        compiler_params=pltpu.CompilerParams(
            dimension_semantics=("parallel",), vmem_limit_bytes=VMEM_LIMIT),
        name="out_ffn",
    )(x2, merged2, ln_par, wo, w1, w2)


def kernel(x, meta_tokens, ln0_g, ln0_b, rel_bias, w_in, shift_mu, attn_sinks, decay_w0, decay_w2, iclr_a0, iclr_a2, gate_w2, k_k, k_a, r_k, lnx_g, lnx_b, w_out, ln1_g, ln1_b, w_ff1, w_ff2, ln2_g, ln2_b):
    bsz, seq, _ = x.shape
    assert D_MODEL == x.shape[2] and seq % RWKV_ROWS == 0 and seq % ATT_ROWS == 0
    assert (bsz * seq) % PROJ_ROWS == 0 and (bsz * seq) % FFN_ROWS == 0
    W = D_MODEL
    wi = w_in[0]
    w_perm = wi.astype(BF16)
    assert w_perm.shape[1] == IN_COLS
    mu = shift_mu[0]
    rows_ = [mu[:W], mu[W:2 * W], mu[2 * W:3 * W]]
    vec = lambda a: a.reshape(N_PAIRS, 1, LANES)
    par = jnp.concatenate(
        [vec(rows_[0]), vec(rows_[1]), vec(rows_[2]),
         jnp.broadcast_to(mu[3 * W:3 * W + LANES].reshape(1, 1, LANES), (N_PAIRS, 1, LANES)),
         jnp.broadcast_to(mu[3 * W + LANES:].reshape(1, 1, LANES), (N_PAIRS, 1, LANES)),
         vec(decay_w0[0]), vec(iclr_a0[0]), vec(k_k[0]), vec(k_a[0]), vec(r_k[0].reshape(-1)),
         vec(lnx_g[0]), vec(lnx_b[0]),
         jnp.zeros((N_PAIRS, _P_ROWS - 12, LANES), F32)], axis=1).astype(F32)
    w2p = decay_w2[0].reshape(DECAY_LORA, N_PAIRS, LANES).transpose(1, 0, 2)
    a2p = iclr_a2[0].reshape(ICLR_LORA, N_PAIRS, LANES).transpose(1, 0, 2)
    z = jnp.zeros_like(w2p)
    wl = jnp.concatenate([jnp.concatenate([w2p, z], axis=2), jnp.concatenate([z, a2p], axis=2)], axis=1).astype(BF16)
    g2 = gate_w2[0].astype(BF16)
    ln_par = jnp.stack([ln0_g, ln0_b, ln1_g[0], ln1_b[0], ln2_g[0], ln2_b[0],
                        jnp.zeros_like(ln0_g), jnp.zeros_like(ln0_g)]).astype(F32)
    g0 = ln0_g.reshape(1, W).astype(F32)
    b0 = ln0_b.reshape(1, W).astype(F32)

    bias = _bias_table(rel_bias)

    meta_blk = jnp.concatenate([jnp.zeros((PAD_ROWS, W), F32), meta_tokens.astype(F32)], axis=0)
    proj_m16, proj_m32 = _ln_proj(meta_blk, g0, b0, w_perm, tm=BLOCK, n_pad=PAD_ROWS)
    zeros8 = jnp.zeros((SUBLANES, F_COLS), F32)
    s_zero = jnp.zeros((N_PAIRS, LANES, LANES), F32)
    _, s_meta = _rwkv(proj_m32[None], zeros8, par, wl, g2, s_zero, rows=BLOCK, pps=N_PAIRS)
    prev8 = proj_m32[BLOCK - SUBLANES:]

    x2 = x.reshape(bsz * seq, W)
    proj16, proj32 = _ln_proj(x2, g0, b0, w_perm, tm=PROJ_ROWS)
    proj32 = proj32.reshape(bsz, seq, F_COLS)
    rwg, _ = _rwkv(proj32, prev8, par, wl, g2, s_meta[0], rows=RWKV_ROWS, pps=PAIRS_PER_STEP)
    merged = _attn_merge(proj16.reshape(bsz, seq, B_COLS), proj32, proj_m16, bias,
                         attn_sinks[0].astype(F32) * LOG2E, rwg)
    out = _out_ffn(x2, merged.reshape(bsz * seq, W), ln_par, w_out[0].astype(BF16),
                   w_ff1[0].astype(BF16), w_ff2[0].astype(BF16), tm=FFN_ROWS)
    return out.reshape(bsz, seq, W)
```

```python
import functools
import math

import numpy as np
import jax
import jax.numpy as jnp
from jax import lax
from jax.experimental import pallas as pl
from jax.experimental.pallas import tpu as pltpu

F32 = jnp.float32
BF16 = jnp.bfloat16

D_MODEL = 1024
N_META = 16
HEAD_DIM = 64
N_HEADS = D_MODEL // HEAD_DIM
N_KV = 2
BLOCK = 128
N_BUCKETS = 32
MAX_EXACT = 16
MAX_DISTANCE = 128
DECAY_LORA = 64
ICLR_LORA = 64
GATE_LORA = 128
D_FF = 4 * D_MODEL
LN_EPS = 1e-5
GN_EPS = 1e-5 * HEAD_DIM
DEPTH = 1
ALPHA = (2.0 * DEPTH) ** 0.25
LOG2E = math.log2(math.e)

LANES = 128
SUBLANES = 8
N_PAIRS = D_MODEL // LANES
CHUNK = 64
RWKV_ROWS = 2048
RWKV_GROUP = 256
PAIRS_PER_STEP = 2
PROJ_ROWS = 512
FFN_ROWS = 512
PAD_ROWS = BLOCK - N_META
ATT_ROWS = 8 * BLOCK
SUM_ROWS = 16

B_Q = 0
B_GA = 1024
B_KX = 2048
B_VX = 2304
B_COLS = 2560
F_R = 0
F_K = 1024
F_V = 2048
F_L = 3072
F_GR = 3328
F_COLS = 4352

IN_Q = 0
IN_K = 1024
IN_ZR = 1280
IN_GA = 4608
IN_GR = 5632
IN_COLS = 6656
_PROJ_SEGMENTS = ((IN_ZR, F_GR - F_R, F_R), (IN_GR, D_MODEL, F_GR))
Q_SCALE = HEAD_DIM ** -0.5 * LOG2E

VMEM_LIMIT = 56 * 1024 * 1024

_NN = (((1,), (0,)), ((), ()))
_NT = (((1,), (1,)), ((), ()))


def _dg(a, b, dims=_NN):
    return lax.dot_general(a, b, dims, preferred_element_type=F32)


def _split(x):
    hi = x.astype(BF16)
    lo = (x - hi.astype(F32)).astype(BF16)
    return hi, lo


def _mm1(a, b, dims=_NN):
    return _dg(a.astype(BF16), b.astype(BF16), dims)


def _layer_norm(x, g, b):
    mu = jnp.mean(x, axis=-1, keepdims=True)
    xc = x - mu
    var = jnp.mean(xc * xc, axis=-1, keepdims=True)
    return xc * lax.rsqrt(var + LN_EPS) * g + b


def _const_spec(shape):
    return pl.BlockSpec(shape, lambda *_: (0,) * len(shape), pipeline_mode=pl.Buffered(1))


def _bias_kernel(bucket_ref, rel_ref, o_ref):
    first, bk = bucket_ref[0], bucket_ref[1]
    for h in range(N_HEADS):
        acc = jnp.where(bk < 0, -jnp.inf, 0.0).astype(F32)
        for b in range(N_BUCKETS):
            acc = jnp.where(bk == b, rel_ref[b, h] * LOG2E, acc)
        o_ref[1, h] = acc
        o_ref[0, h] = jnp.where(first < 0, -jnp.inf, acc)


def _bucket_table():
    q = np.arange(BLOCK)[:, None]
    s = np.arange(2 * BLOCK)[None, :]
    dist = q + BLOCK - s
    in_window = (dist >= 0) & (dist < BLOCK)
    d0 = np.maximum(dist, 0)
    d = np.maximum(d0, 1).astype(np.float32)
    large = MAX_EXACT + (np.log(d / np.float32(MAX_EXACT)) / np.float32(math.log(MAX_DISTANCE / MAX_EXACT))
                         * (N_BUCKETS - MAX_EXACT)).astype(np.int32)
    large = np.minimum(large, N_BUCKETS - 1)
    bucket = np.where(d0 < MAX_EXACT, d0, large).astype(np.int32)
    general = np.where(in_window, bucket, -1)
    first = np.where(in_window & (s >= PAD_ROWS), bucket, -1)
    return np.stack([first.T, general.T]).astype(np.int32)


def _bias_table(rel_bias):
    return pl.pallas_call(
        _bias_kernel,
        out_shape=jax.ShapeDtypeStruct((2, N_HEADS, 2 * BLOCK, BLOCK), F32),
        in_specs=[pl.BlockSpec(memory_space=pltpu.VMEM), pl.BlockSpec(memory_space=pltpu.SMEM)],
        out_specs=pl.BlockSpec(memory_space=pltpu.VMEM),
        name="bias_table",
    )(jnp.asarray(_bucket_table()), rel_bias.astype(F32))


def _ln_proj_kernel(x_ref, g_ref, b_ref, w_ref, o16_ref, o32_ref, *, n_pad, tn):
    tm = x_ref.shape[0]
    halves = [slice(0, tm // 2), slice(tm // 2, tm)]
    ybs = []
    for h_ in halves:
        y = _layer_norm(x_ref[h_, :], g_ref[...], b_ref[...])
        if n_pad:
            row = lax.broadcasted_iota(jnp.int32, y.shape, 0) + h_.start
            y = jnp.where(row < n_pad, 0.0, y)
        ybs.append(y.astype(BF16))
    lo_lane = lax.broadcasted_iota(jnp.int32, (1, LANES), 1) < HEAD_DIM

    def dup(z):
        zr = pltpu.roll(z, HEAD_DIM, 1)
        return jnp.concatenate([jnp.where(lo_lane, z, zr), jnp.where(lo_lane, zr, z)], axis=1)

    for j in range(0, D_MODEL, tn):
        for h_, yh in zip(halves, ybs):
            o16_ref[h_, B_Q + j:B_Q + j + tn] = (_dg(yh, w_ref[:, IN_Q + j:IN_Q + j + tn]) * Q_SCALE).astype(BF16)
    yb = jnp.concatenate(ybs, axis=0)
    for j in range(0, D_MODEL, tn):
        o16_ref[:, B_GA + j:B_GA + j + tn] = _dg(yb, w_ref[:, IN_GA + j:IN_GA + j + tn]).astype(BF16)
    kv = _dg(yb, w_ref[:, IN_K:IN_K + 2 * LANES])
    o16_ref[:, B_KX:B_KX + 2 * LANES] = dup(kv[:, :LANES]).astype(BF16)
    o16_ref[:, B_VX:B_VX + 2 * LANES] = dup(kv[:, LANES:]).astype(BF16)
    for src, width, dst in _PROJ_SEGMENTS:
        for j in range(0, width, tn):
            w_ = min(tn, width - j)
            o32_ref[:, dst + j:dst + j + w_] = _dg(yb, w_ref[:, src + j:src + j + w_])


def _ln_proj(x2, g, b, w, *, tm, tn=512, n_pad=0):
    m = x2.shape[0]
    return pl.pallas_call(
        functools.partial(_ln_proj_kernel, n_pad=n_pad, tn=tn),
        out_shape=(jax.ShapeDtypeStruct((m, B_COLS), BF16), jax.ShapeDtypeStruct((m, F_COLS), F32)),
        grid=(m // tm,),
        in_specs=[
            pl.BlockSpec((tm, D_MODEL), lambda i: (i, 0)),
            _const_spec((1, D_MODEL)),
            _const_spec((1, D_MODEL)),
            _const_spec((D_MODEL, IN_COLS)),
        ],
        out_specs=(pl.BlockSpec((tm, B_COLS), lambda i: (i, 0)), pl.BlockSpec((tm, F_COLS), lambda i: (i, 0))),
        compiler_params=pltpu.CompilerParams(
            dimension_semantics=("parallel",), vmem_limit_bytes=VMEM_LIMIT),
        name="ln_proj",
    )(x2, g, b, w)


(_P_MU_R, _P_MU_K, _P_MU_V, _P_MU_WA, _P_MU_G, _P_W0, _P_A0, _P_KK, _P_KA, _P_RK, _P_LNG, _P_LNB) = range(12)
_P_ROWS = 16


def _rwkv_consts():
    r2, c2 = np.indices((LANES, LANES))
    same = (r2 < HEAD_DIM) == (c2 < HEAD_DIM)
    low = (r2 % HEAD_DIM) > (c2 % HEAD_DIM)
    masks = np.stack([low & same,
                      low & ~same,
                      (r2 % HEAD_DIM) >= (c2 % HEAD_DIM),
                      same, ~same, r2 == c2]).astype(np.float32)
    bf = np.stack([same.astype(np.float32), same.astype(np.float32) / HEAD_DIM])
    ones2 = np.kron(np.eye(2, dtype=np.float32), same.astype(np.float32))
    t, s = np.indices((CHUNK, CHUNK))
    return (jnp.asarray(masks), jnp.asarray(bf, dtype=BF16), jnp.asarray(ones2, dtype=BF16),
            jnp.asarray((t >= s).astype(np.float32), dtype=BF16))


def _rwkv_kernel(r_ref, k_ref, v_ref, l_ref, gr_ref, pr_ref, pk_ref, pv_ref, plr_ref,
                 par_ref, wl_ref, g2_ref, s0_ref, msk_ref, bfm_ref, ones2_ref, cum_ref, o_ref, s_ref,
                 h_scr, zr_scr, zk_scr, zv_scr, zl_scr, *, rows, group, pps):
    t = pl.program_id(2)

    @pl.when(t == 0)
    def _():
        h_scr[...] = s0_ref[...]
        zr_scr[...] = pr_ref[...]
        zk_scr[...] = pk_ref[...]
        zv_scr[...] = pv_ref[...]
        zl_scr[...] = plr_ref[...]

    cat = lambda parts, axis=0: jnp.concatenate(parts, axis=axis)
    pairs = range(pps)
    prow = lambda q, i: par_ref[q, i:i + 1, :]
    prow2 = lambda i: cat([prow(q, i) for q in pairs], 1)
    psl = [slice(q * LANES, (q + 1) * LANES) for q in pairs]
    row_id = lax.broadcasted_iota(jnp.int32, (group, 1), 0)
    lane = lax.broadcasted_iota(jnp.int32, (1, LANES), 1)
    lo_half = lane < HEAD_DIM
    m0 = lo_half.astype(F32)
    m1 = 1.0 - m0
    mask_p, mask_a, tri_i = (msk_ref[i].astype(BF16) for i in range(3))
    head_diag, head_anti, eye = (msk_ref[i] for i in range(3, 6))
    avg_bd = bfm_ref[1]
    ones2 = ones2_ref[...]
    cum = cum_ref[...]
    mu_l = cat([prow(0, _P_MU_WA), prow(0, _P_MU_G)], 1)
    cpg = group // CHUNK
    n_groups = rows // group
    csl = [slice(c * CHUNK, (c + 1) * CHUNK) for c in range(cpg)]
    zeros_q = jnp.zeros((CHUNK, LANES), BF16)

    def shifted(ref, scr, mu, g):
        z = ref[0, g * group:(g + 1) * group, :]
        prev_row = scr[SUBLANES - 1:SUBLANES, :] if g == 0 else ref[0, g * group - 1:g * group, :]
        prev = jnp.where(row_id == 0, prev_row, pltpu.roll(z, 1, 0))
        return z + (prev - z) * mu

    def prep(g, out):
        r2 = shifted(r_ref, zr_scr, prow2(_P_MU_R), g)
        k2_ = shifted(k_ref, zk_scr, prow2(_P_MU_K), g)
        v2 = shifted(v_ref, zv_scr, prow2(_P_MU_V), g)
        yield
        zl = shifted(l_ref, zl_scr, mu_l, g)
        wa, gl = zl[:, :LANES], zl[:, LANES:]
        lora_in = jnp.where(lane < DECAY_LORA, jnp.tanh(wa), wa).astype(BF16)
        gate_all = _dg((1.0 / (1.0 + jnp.exp(-gl))).astype(BF16), g2_ref[...])
        yield
        for q in pairs:
            r, k, v = r2[:, psl[q]], k2_[:, psl[q]], v2[:, psl[q]]
            lo = _dg(lora_in, wl_ref[q])
            gate = gate_all[:, psl[q]]
            kk = k * prow(q, _P_KK)
            yield
            wpre = -(prow(q, _P_W0) + lo[:, :LANES])
            w = -(jnp.maximum(wpre, 0.0) + jnp.log(1.0 + jnp.exp(-jnp.abs(wpre)))) - 0.5
            logw = -jnp.exp(w)
            hi, lw = _split(logw)
            hl = cat([hi, lw], 1)
            lcs = [_dg(cum, hl[s]) for s in csl]
            a = 1.0 / (1.0 + jnp.exp(-(prow(q, _P_A0) + lo[:, LANES:])))
            k2 = k * (1.0 + (a - 1.0) * prow(q, _P_KA))
            sums = _dg(cat([kk * kk, r * k2 * prow(q, _P_RK)], 1).astype(BF16), ones2)
            yield
            kk = kk / jnp.maximum(jnp.sqrt(sums[:, :LANES]), 1e-12)
            lc = cat([x[:, :LANES] + x[:, LANES:] for x in lcs])
            e_in, e_neg = jnp.exp(lc), jnp.exp(-lc)
            yield
            rt_ = r * e_in
            at_ = -kk * jnp.exp(lc - logw)
            v_sw = pltpu.roll(v, HEAD_DIM, 1)
            kt_, bt_ = k2 * e_neg, kk * a * e_neg
            gate = gate / (1.0 + jnp.exp(-gr_ref[0, g * group:(g + 1) * group, psl[q]]))
            out.append(dict(q=q, v=v, gate=gate, bonus=sums[:, LANES:] * v, e_in=e_in, rt=rt_, at=at_,
                            kt0=kt_ * m0, kt1=kt_ * m1, bt0=bt_ * m0, bt1=bt_ * m1,
                            vs0=(v_sw * m0).astype(BF16), vs1=(v_sw * m1).astype(BF16)))
            yield

    def phase1(pp, out):
        units = [(p_, s) for p_ in pp for s in csl]
        get = lambda n: [p_[n][s] for p_, s in units]
        at_, rt_, bt0, bt1, kt0, kt1, vs0, vs1 = (get(n) for n in ("at", "rt", "bt0", "bt1", "kt0", "kt1", "vs0", "vs1"))
        gcs = [p_["e_in"][s.stop - 1:s.stop, :] for p_, s in units]
        d_ab = [_mm1(cat([a_, r_]), cat([b0, k0, k1, b1]), _NT)
                for a_, r_, b0, k0, k1, b1 in zip(at_, rt_, bt0, kt0, kt1, bt1)]
        yield
        top = [cat([d[:CHUNK, :LANES], d[:CHUNK, LANES:]]).astype(BF16) for d in d_ab]
        bot = [cat([d[CHUNK:, :LANES], d[CHUNK:, LANES:]]).astype(BF16) * tri_i for d in d_ab]
        p = [tp * mask_p for tp in top]
        xb = [(cat([a_ * m0, a_ * m1]) + _dg(tp * mask_a, cat([w0, w1]))).astype(BF16)
              for a_, tp, w0, w1 in zip(at_, top, vs0, vs1)]
        yield
        for i in range(6):
            if i < 5:
                px = [_dg(pw, cat([x, pw], 1)) for x, pw in zip(xb, p)]
                p = [d[:, LANES:].astype(BF16) for d in px]
                xb = [x + d[:, :LANES].astype(BF16) for x, d in zip(xb, px)]
            else:
                xb = [x + _dg(pw, x).astype(BF16) for x, pw in zip(xb, p)]
            yield
        gst = [cat([x[:CHUNK], w1, w0, x[CHUNK:]]) for x, w0, w1 in zip(xb, vs0, vs1)]
        lhs45 = []
        for b_, b0, k0, k1, b1, gc in zip(bot, bt0, kt0, kt1, bt1, gcs):
            lhs45.append(cat([cat([b_[:CHUNK], zeros_q], 1), cat([zeros_q, b_[CHUNK:]], 1),
                              cat([b0 * gc, k0 * gc, k1 * gc, b1 * gc]).T.astype(BF16)]))
        o45 = [_dg(l_, g_) for l_, g_ in zip(lhs45, gst)]
        yield
        out["lhs"] = [cat([(r_ + jnp.where(lo_half, o[:CHUNK], o[CHUNK:LANES])).astype(BF16),
                           (o[LANES:] * head_diag + eye * gc).astype(BF16)])
                      for r_, o, gc in zip(rt_, o45, gcs)]
        out["y_in"] = [jnp.where(lo_half, o[CHUNK:LANES], o[:CHUNK]) for o in o45]
        out["g_c"] = [o[LANES:] * head_anti for o in o45]

    def chain_step(f, c, hs, ys):
        for q in pairs:
            u = q * cpg + c
            yh = _dg(f["lhs"][u], hs[q].astype(BF16))
            ys[q].append(yh[:CHUNK] + f["y_in"][u])
            hs[q] = yh[CHUNK:] + f["g_c"][u]

    def epilogue(g, pp, ys):
        gsl = slice(g * group, (g + 1) * group)
        for p_ in pp:
            q = p_["q"]
            y = pltpu.roll(cat(ys[q]), HEAD_DIM, 1)
            yc = y - _dg(y.astype(BF16), avg_bd)
            yv = _dg((yc * yc).astype(BF16), avg_bd)
            yn = yc * lax.rsqrt(yv + GN_EPS) * prow(q, _P_LNG) + prow(q, _P_LNB)
            o_ref[0, gsl, psl[q]] = ((yn + p_["bonus"]) * p_["gate"]).astype(BF16)

    n_stages, n_pieces = 9, 2 + 4 * pps
    hs = [h_scr[q] for q in pairs]
    p_all = [[] for _ in range(n_groups)]
    f_all = [{} for _ in range(n_groups)]
    for _ in prep(0, p_all[0]):
        pass
    for g in range(n_groups):
        side = prep(g + 1, p_all[g + 1]) if g + 1 < n_groups else iter(())
        ys = [[] for _ in pairs]
        done = 0
        for i, _ in enumerate(phase1(p_all[g], f_all[g])):
            while done * n_stages < (i + 1) * n_pieces:
                next(side, None)
                done += 1
            if g > 0 and i % 2 == 1 and len(ys[0]) < cpg:
                chain_step(f_all[g - 1], len(ys[0]), hs, ys)
        for _ in side:
            pass
        if g > 0:
            while len(ys[0]) < cpg:
                chain_step(f_all[g - 1], len(ys[0]), hs, ys)
            epilogue(g - 1, p_all[g - 1], ys)
    ys = [[] for _ in pairs]
    for c in range(cpg):
        chain_step(f_all[n_groups - 1], c, hs, ys)
    epilogue(n_groups - 1, p_all[n_groups - 1], ys)
    for q in pairs:
        h_scr[q] = hs[q]
    zr_scr[...] = r_ref[0, rows - SUBLANES:rows, :]
    zk_scr[...] = k_ref[0, rows - SUBLANES:rows, :]
    zv_scr[...] = v_ref[0, rows - SUBLANES:rows, :]
    zl_scr[...] = l_ref[0, rows - SUBLANES:rows, :]

    @pl.when(t == pl.num_programs(2) - 1)
    def _():
        for q in pairs:
            s_ref[0, q] = hs[q]


def _rwkv(proj3, prev8, par, wl, g2, s0, *, rows, pps):
    bsz, t_len, _ = proj3.shape
    masks, bfm, ones2, cum = _rwkv_consts()
    wide = pps * LANES
    col = lambda off: (lambda b, p, t: (b, t, off // wide + p))
    pcol = lambda off: (lambda b, p, t: (0, off // wide + p))
    return pl.pallas_call(
        functools.partial(_rwkv_kernel, rows=rows, group=min(RWKV_GROUP, rows), pps=pps),
        out_shape=(jax.ShapeDtypeStruct((bsz, t_len, D_MODEL), BF16),
                   jax.ShapeDtypeStruct((bsz, N_PAIRS, LANES, LANES), F32)),
        grid=(bsz, N_PAIRS // pps, t_len // rows),
        in_specs=[
            pl.BlockSpec((1, rows, wide), col(F_R)),
            pl.BlockSpec((1, rows, wide), col(F_K)),
            pl.BlockSpec((1, rows, wide), col(F_V)),
            pl.BlockSpec((1, rows, 2 * LANES), lambda b, p, t: (b, t, F_L // (2 * LANES))),
            pl.BlockSpec((1, rows, wide), col(F_GR)),
            pl.BlockSpec((SUBLANES, wide), pcol(F_R)),
            pl.BlockSpec((SUBLANES, wide), pcol(F_K)),
            pl.BlockSpec((SUBLANES, wide), pcol(F_V)),
            pl.BlockSpec((SUBLANES, 2 * LANES), lambda b, p, t: (0, F_L // (2 * LANES))),
            pl.BlockSpec((pps, _P_ROWS, LANES), lambda b, p, t: (p, 0, 0)),
            pl.BlockSpec((pps, LANES, 2 * LANES), lambda b, p, t: (p, 0, 0)),
            pl.BlockSpec((GATE_LORA, wide), lambda b, p, t: (0, p)),
            pl.BlockSpec((pps, LANES, LANES), lambda b, p, t: (p, 0, 0)),
            _const_spec(masks.shape), _const_spec(bfm.shape), _const_spec(ones2.shape), _const_spec(cum.shape),
        ],
        out_specs=(pl.BlockSpec((1, rows, wide), lambda b, p, t: (b, t, p)),
                   pl.BlockSpec((1, pps, LANES, LANES), lambda b, p, t: (b, p, 0, 0))),
        scratch_shapes=[pltpu.VMEM((pps, LANES, LANES), F32),
                        pltpu.VMEM((SUBLANES, wide), F32), pltpu.VMEM((SUBLANES, wide), F32),
                        pltpu.VMEM((SUBLANES, wide), F32), pltpu.VMEM((SUBLANES, 2 * LANES), F32)],
        compiler_params=pltpu.CompilerParams(
            dimension_semantics=("parallel", "parallel", "arbitrary"), vmem_limit_bytes=VMEM_LIMIT),
        name="rwkv",
    )(proj3, proj3, proj3, proj3, proj3, prev8, prev8, prev8, prev8, par, wl, g2, s0, masks, bfm, ones2, cum)


def _attn_kernel(sink_ref, q_ref, ga_ref, kvc_ref, kvp_ref, kvm_ref, bf_ref, bg_ref, rw_ref, o_ref):
    first = pl.program_id(1) == 0
    lane = lax.broadcasted_iota(jnp.int32, (1, LANES), 1)
    lo_lane = lane < HEAD_DIM
    lo_row = lax.broadcasted_iota(jnp.int32, (LANES, 1), 0) < HEAD_DIM
    kv_w = 2 * LANES
    kc, vc = kvc_ref[0, :, :kv_w], kvc_ref[0, :, kv_w:]
    kv_prev = jnp.where(first, kvm_ref[...], kvp_ref[0])
    k_prev, v_prev = kv_prev[:, :kv_w], kv_prev[:, kv_w:]
    ppk = N_PAIRS // N_KV

    def scores(sb, kh):
        rs = slice(sb * BLOCK, (sb + 1) * BLOCK)
        kp_, vp_ = (k_prev, v_prev) if sb == 0 else (kc[rs.start - BLOCK:rs.start], vc[rs.start - BLOCK:rs.start])
        kx = jnp.concatenate([kp_, kc[rs]], axis=0)[:, kh * LANES:(kh + 1) * LANES]
        vx = jnp.concatenate([vp_, vc[rs]], axis=0)[:, kh * LANES:(kh + 1) * LANES].astype(F32)
        lts = []
        for p in range(kh * ppk, (kh + 1) * ppk):
            qt = q_ref[0, rs, p * LANES:(p + 1) * LANES]
            zq = jnp.zeros_like(qt)
            q2 = jnp.concatenate([jnp.where(lo_lane, qt, zq), jnp.where(lo_lane, zq, qt)], axis=0)
            lts.append(_dg(kx, q2, _NT))
        vxt = jnp.concatenate([vx.T, jnp.ones((SUM_ROWS, 2 * BLOCK), F32)], axis=0).astype(BF16)
        return dict(sb=sb, kh=kh, rs=rs, lts=lts, vxt=vxt)

    def finish(g):
        bias_ref = bf_ref if g["sb"] == 0 else bg_ref
        pes, sinks = [], []
        for j, lt in enumerate(g["lts"]):
            for sub in range(2):
                hd = 2 * (g["kh"] * ppk + j) + sub
                lg = lt[:, sub * LANES:(sub + 1) * LANES] + bias_ref[0, hd]
                m = jnp.maximum(jnp.max(lg, axis=0, keepdims=True), sink_ref[hd])
                sinks.append(jnp.exp2(sink_ref[hd] - m))
                pes.append(jnp.exp2(lg - m).astype(BF16))
        ots = [_dg(g["vxt"], jnp.concatenate(pes[2 * j:2 * j + 2], axis=1)) for j in range(ppk)]
        for j, ot in enumerate(ots):
            cs = slice((g["kh"] * ppk + j) * LANES, (g["kh"] * ppk + j + 1) * LANES)
            inv0 = 1.0 / (ot[LANES:LANES + 1, :LANES] + sinks[2 * j])
            inv1 = 1.0 / (ot[LANES:LANES + 1, LANES:] + sinks[2 * j + 1])
            att = jnp.where(lo_row, ot[:LANES, :LANES] * inv0, ot[:LANES, LANES:] * inv1).T
            gate = 1.0 / (1.0 + jnp.exp(-ga_ref[0, g["rs"], cs].astype(F32)))
            o_ref[0, g["rs"], cs] = (gate * att + rw_ref[0, g["rs"], cs].astype(F32)).astype(BF16)

    order = [(sb, kh) for sb in range(ATT_ROWS // BLOCK) for kh in range(N_KV)]
    cur = scores(*order[0])
    for nxt in order[1:]:
        ahead = scores(*nxt)
        finish(cur)
        cur = ahead
    finish(cur)


def _attn_merge(proj16, kvm, bias, sinks, rwg):
    bsz, t_len, _ = proj16.shape
    kv_w = 4 * LANES
    assert B_VX == B_KX + kv_w // 2 and B_KX % kv_w == 0
    per = ATT_ROWS // BLOCK
    rows = lambda w, col: pl.BlockSpec((1, ATT_ROWS, w), lambda b, n: (b, n, col // w))
    return pl.pallas_call(
        _attn_kernel,
        out_shape=jax.ShapeDtypeStruct((bsz, t_len, D_MODEL), BF16),
        grid=(bsz, t_len // ATT_ROWS),
        in_specs=[
            pl.BlockSpec(memory_space=pltpu.SMEM),
            rows(D_MODEL, B_Q), rows(D_MODEL, B_GA), rows(kv_w, B_KX),
            pl.BlockSpec((1, BLOCK, kv_w), lambda b, n: (b, jnp.maximum(n * per - 1, 0), B_KX // kv_w)),
            pl.BlockSpec((BLOCK, kv_w), lambda b, n: (0, B_KX // kv_w)),
            pl.BlockSpec((1, N_HEADS, 2 * BLOCK, BLOCK), lambda b, n: (jnp.minimum(n, 1), 0, 0, 0)),
            pl.BlockSpec((1, N_HEADS, 2 * BLOCK, BLOCK), lambda b, n: (1, 0, 0, 0)),
            rows(D_MODEL, 0),
        ],
        out_specs=rows(D_MODEL, 0),
        compiler_params=pltpu.CompilerParams(
            dimension_semantics=("parallel", "arbitrary"), vmem_limit_bytes=VMEM_LIMIT),
        name="attn_merge",
    )(sinks, proj16, proj16, proj16, proj16, kvm, bias, bias, rwg)


def _out_ffn_kernel(x_ref, mg_ref, ln_ref, wo_ref, w1_ref, w2_ref, o_ref, *, ff_tile, parts):
    ln = ln_ref[...]
    tm = x_ref.shape[0]
    rs = [slice(i * tm // parts, (i + 1) * tm // parts) for i in range(parts)]
    proj = [_dg(mg_ref[r, :], wo_ref[...]) for r in rs]
    h1 = [_layer_norm(ALPHA * _layer_norm(x_ref[r, :], ln[0:1], ln[1:2]) + p, ln[2:3], ln[3:4]) for r, p in zip(rs, proj)]
    for r, h in zip(rs, h1):
        hb = h.astype(BF16)
        acc = ALPHA * h
        for j in range(D_FF // ff_tile):
            cs = slice(j * ff_tile, (j + 1) * ff_tile)
            u = jnp.maximum(_dg(hb, w1_ref[:, cs]), 0.0)
            acc = acc + _dg((u * u).astype(BF16), w2_ref[cs, :])
        o_ref[r, :] = _layer_norm(acc, ln[4:5], ln[5:6])


def _out_ffn(x2, merged2, ln_par, wo, w1, w2, *, tm, ff_tile=1024, parts=2):
    m = x2.shape[0]
    return pl.pallas_call(
        functools.partial(_out_ffn_kernel, ff_tile=ff_tile, parts=parts),
        out_shape=jax.ShapeDtypeStruct((m, D_MODEL), F32),
        grid=(m // tm,),
        in_specs=[
            pl.BlockSpec((tm, D_MODEL), lambda i: (i, 0)),
            pl.BlockSpec((tm, D_MODEL), lambda i: (i, 0)),
            _const_spec((SUBLANES, D_MODEL)),
            _const_spec((D_MODEL, D_MODEL)),
            _const_spec((D_MODEL, D_FF)),
            _const_spec((D_FF, D_MODEL)),
        ],
        out_specs=pl.BlockSpec((tm, D_MODEL), lambda i: (i, 0)),
        compiler_params=pltpu.CompilerParams(
            dimension_semantics=("parallel",), vmem_limit_bytes=VMEM_LIMIT),
        name="out_ffn",
    )(x2, merged2, ln_par, wo, w1, w2)


def kernel(x, meta_tokens, ln0_g, ln0_b, rel_bias, w_in, shift_mu, attn_sinks, decay_w0, decay_w2, iclr_a0, iclr_a2, gate_w2, k_k, k_a, r_k, lnx_g, lnx_b, w_out, ln1_g, ln1_b, w_ff1, w_ff2, ln2_g, ln2_b):
    bsz, seq, _ = x.shape
    assert D_MODEL == x.shape[2] and seq % RWKV_ROWS == 0 and seq % ATT_ROWS == 0
    assert (bsz * seq) % PROJ_ROWS == 0 and (bsz * seq) % FFN_ROWS == 0
    W = D_MODEL
    wi = w_in[0]
    w_perm = wi.astype(BF16)
    assert w_perm.shape[1] == IN_COLS
    mu = shift_mu[0]
    rows_ = [mu[:W], mu[W:2 * W], mu[2 * W:3 * W]]
    vec = lambda a: a.reshape(N_PAIRS, 1, LANES)
    par = jnp.concatenate(
        [vec(rows_[0]), vec(rows_[1]), vec(rows_[2]),
         jnp.broadcast_to(mu[3 * W:3 * W + LANES].reshape(1, 1, LANES), (N_PAIRS, 1, LANES)),
         jnp.broadcast_to(mu[3 * W + LANES:].reshape(1, 1, LANES), (N_PAIRS, 1, LANES)),
         vec(decay_w0[0]), vec(iclr_a0[0]), vec(k_k[0]), vec(k_a[0]), vec(r_k[0].reshape(-1)),
         vec(lnx_g[0]), vec(lnx_b[0]),
         jnp.zeros((N_PAIRS, _P_ROWS - 12, LANES), F32)], axis=1).astype(F32)
    w2p = decay_w2[0].reshape(DECAY_LORA, N_PAIRS, LANES).transpose(1, 0, 2)
    a2p = iclr_a2[0].reshape(ICLR_LORA, N_PAIRS, LANES).transpose(1, 0, 2)
    z = jnp.zeros_like(w2p)
    wl = jnp.concatenate([jnp.concatenate([w2p, z], axis=2), jnp.concatenate([z, a2p], axis=2)], axis=1).astype(BF16)
    g2 = gate_w2[0].astype(BF16)
    ln_par = jnp.stack([ln0_g, ln0_b, ln1_g[0], ln1_b[0], ln2_g[0], ln2_b[0],
                        jnp.zeros_like(ln0_g), jnp.zeros_like(ln0_g)]).astype(F32)
    g0 = ln0_g.reshape(1, W).astype(F32)
    b0 = ln0_b.reshape(1, W).astype(F32)

    bias = _bias_table(rel_bias)

    meta_blk = jnp.concatenate([jnp.zeros((PAD_ROWS, W), F32), meta_tokens.astype(F32)], axis=0)
    proj_m16, proj_m32 = _ln_proj(meta_blk, g0, b0, w_perm, tm=BLOCK, n_pad=PAD_ROWS)
    zeros8 = jnp.zeros((SUBLANES, F_COLS), F32)
    s_zero = jnp.zeros((N_PAIRS, LANES, LANES), F32)
    _, s_meta = _rwkv(proj_m32[None], zeros8, par, wl, g2, s_zero, rows=BLOCK, pps=N_PAIRS)
    prev8 = proj_m32[BLOCK - SUBLANES:]

    x2 = x.reshape(bsz * seq, W)
    proj16, proj32 = _ln_proj(x2, g0, b0, w_perm, tm=PROJ_ROWS)
    proj32 = proj32.reshape(bsz, seq, F_COLS)
    rwg, _ = _rwkv(proj32, prev8, par, wl, g2, s_meta[0], rows=RWKV_ROWS, pps=PAIRS_PER_STEP)
    merged = _attn_merge(proj16.reshape(bsz, seq, B_COLS), proj_m16, bias, attn_sinks[0].astype(F32) * LOG2E, rwg)
    out = _out_ffn(x2, merged.reshape(bsz * seq, W), ln_par, w_out[0].astype(BF16),
                   w_ff1[0].astype(BF16), w_ff2[0].astype(BF16), tm=FFN_ROWS)
    return out.reshape(bsz, seq, W)
```

```python
import functools
import math

import numpy as np
import jax
import jax.numpy as jnp
from jax import lax
from jax.experimental import pallas as pl
from jax.experimental.pallas import tpu as pltpu

F32 = jnp.float32
BF16 = jnp.bfloat16

D_MODEL = 1024
N_META = 16
HEAD_DIM = 64
N_HEADS = D_MODEL // HEAD_DIM
N_KV = 2
BLOCK = 128
N_BUCKETS = 32
MAX_EXACT = 16
MAX_DISTANCE = 128
DECAY_LORA = 64
ICLR_LORA = 64
GATE_LORA = 128
D_FF = 4 * D_MODEL
LN_EPS = 1e-5
GN_EPS = 1e-5 * HEAD_DIM
DEPTH = 1
ALPHA = (2.0 * DEPTH) ** 0.25
LOG2E = math.log2(math.e)

LANES = 128
SUBLANES = 8
N_PAIRS = D_MODEL // LANES
CHUNK = 64
RWKV_ROWS = 2048
RWKV_GROUP = 256
PAIRS_PER_STEP = 2
PROJ_ROWS = 512
FFN_ROWS = 1024
FFN_PARTS = 4
PAD_ROWS = BLOCK - N_META
ATT_ROWS = 8 * BLOCK
SUM_ROWS = 16

B_Q = 0
B_KX = 1024
B_VX = 1280
B_COLS = 1536
F_GA = 0
F_R = 1024
F_K = 2048
F_V = 3072
F_L = 4096
F_GR = 4352
F_COLS = 5376

IN_Q = 0
IN_K = 1024
IN_ZR = 1280
IN_GA = 4608
IN_GR = 5632
IN_COLS = 6656
_PROJ_SEGMENTS = ((IN_GA, D_MODEL, F_GA), (IN_ZR, F_GR - F_R, F_R), (IN_GR, D_MODEL, F_GR))
Q_SCALE = HEAD_DIM ** -0.5 * LOG2E

VMEM_LIMIT = 56 * 1024 * 1024

_NN = (((1,), (0,)), ((), ()))
_NT = (((1,), (1,)), ((), ()))


def _dg(a, b, dims=_NN):
    return lax.dot_general(a, b, dims, preferred_element_type=F32)


def _split(x):
    hi = x.astype(BF16)
    lo = (x - hi.astype(F32)).astype(BF16)
    return hi, lo


def _mm1(a, b, dims=_NN):
    return _dg(a.astype(BF16), b.astype(BF16), dims)


def _layer_norm(x, g, b):
    mu = jnp.mean(x, axis=-1, keepdims=True)
    xc = x - mu
    var = jnp.mean(xc * xc, axis=-1, keepdims=True)
    return xc * lax.rsqrt(var + LN_EPS) * g + b


def _const_spec(shape):
    return pl.BlockSpec(shape, lambda *_: (0,) * len(shape), pipeline_mode=pl.Buffered(1))


def _bias_kernel(bucket_ref, rel_ref, o_ref):
    first, bk = bucket_ref[0], bucket_ref[1]
    for h in range(N_HEADS):
        acc = jnp.where(bk < 0, -jnp.inf, 0.0).astype(F32)
        for b in range(N_BUCKETS):
            acc = jnp.where(bk == b, rel_ref[b, h] * LOG2E, acc)
        o_ref[1, h] = acc
        o_ref[0, h] = jnp.where(first < 0, -jnp.inf, acc)


def _bucket_table():
    q = np.arange(BLOCK)[:, None]
    s = np.arange(2 * BLOCK)[None, :]
    dist = q + BLOCK - s
    in_window = (dist >= 0) & (dist < BLOCK)
    d0 = np.maximum(dist, 0)
    d = np.maximum(d0, 1).astype(np.float32)
    large = MAX_EXACT + (np.log(d / np.float32(MAX_EXACT)) / np.float32(math.log(MAX_DISTANCE / MAX_EXACT))
                         * (N_BUCKETS - MAX_EXACT)).astype(np.int32)
    large = np.minimum(large, N_BUCKETS - 1)
    bucket = np.where(d0 < MAX_EXACT, d0, large).astype(np.int32)
    general = np.where(in_window, bucket, -1)
    first = np.where(in_window & (s >= PAD_ROWS), bucket, -1)
    return np.stack([first.T, general.T]).astype(np.int32)


def _bias_table(rel_bias):
    return pl.pallas_call(
        _bias_kernel,
        out_shape=jax.ShapeDtypeStruct((2, N_HEADS, 2 * BLOCK, BLOCK), F32),
        in_specs=[pl.BlockSpec(memory_space=pltpu.VMEM), pl.BlockSpec(memory_space=pltpu.SMEM)],
        out_specs=pl.BlockSpec(memory_space=pltpu.VMEM),
        name="bias_table",
    )(jnp.asarray(_bucket_table()), rel_bias.astype(F32))


def _ln_proj_kernel(x_ref, g_ref, b_ref, w_ref, o16_ref, o32_ref, *, n_pad, tn):
    tm = x_ref.shape[0]
    halves = [slice(0, tm // 2), slice(tm // 2, tm)]
    ybs = []
    for h_ in halves:
        y = _layer_norm(x_ref[h_, :], g_ref[...], b_ref[...])
        if n_pad:
            row = lax.broadcasted_iota(jnp.int32, y.shape, 0) + h_.start
            y = jnp.where(row < n_pad, 0.0, y)
        ybs.append(y.astype(BF16))
    lo_lane = lax.broadcasted_iota(jnp.int32, (1, LANES), 1) < HEAD_DIM

    def dup(z):
        zr = pltpu.roll(z, HEAD_DIM, 1)
        return jnp.concatenate([jnp.where(lo_lane, z, zr), jnp.where(lo_lane, zr, z)], axis=1)

    for j in range(0, D_MODEL, tn):
        for h_, yh in zip(halves, ybs):
            o16_ref[h_, B_Q + j:B_Q + j + tn] = (_dg(yh, w_ref[:, IN_Q + j:IN_Q + j + tn]) * Q_SCALE).astype(BF16)
    yb = jnp.concatenate(ybs, axis=0)
    kv = _dg(yb, w_ref[:, IN_K:IN_K + 2 * LANES])
    o16_ref[:, B_KX:B_KX + 2 * LANES] = dup(kv[:, :LANES]).astype(BF16)
    o16_ref[:, B_VX:B_VX + 2 * LANES] = dup(kv[:, LANES:]).astype(BF16)
    for src, width, dst in _PROJ_SEGMENTS:
        for j in range(0, width, tn):
            w_ = min(tn, width - j)
            o32_ref[:, dst + j:dst + j + w_] = _dg(yb, w_ref[:, src + j:src + j + w_])


def _ln_proj(x2, g, b, w, *, tm, tn=512, n_pad=0):
    m = x2.shape[0]
    return pl.pallas_call(
        functools.partial(_ln_proj_kernel, n_pad=n_pad, tn=tn),
        out_shape=(jax.ShapeDtypeStruct((m, B_COLS), BF16), jax.ShapeDtypeStruct((m, F_COLS), F32)),
        grid=(m // tm,),
        in_specs=[
            pl.BlockSpec((tm, D_MODEL), lambda i: (i, 0)),
            _const_spec((1, D_MODEL)),
            _const_spec((1, D_MODEL)),
            _const_spec((D_MODEL, IN_COLS)),
        ],
        out_specs=(pl.BlockSpec((tm, B_COLS), lambda i: (i, 0)), pl.BlockSpec((tm, F_COLS), lambda i: (i, 0))),
        compiler_params=pltpu.CompilerParams(
            dimension_semantics=("parallel",), vmem_limit_bytes=VMEM_LIMIT),
        name="ln_proj",
    )(x2, g, b, w)


(_P_MU_R, _P_MU_K, _P_MU_V, _P_MU_WA, _P_MU_G, _P_W0, _P_A0, _P_KK, _P_KA, _P_RK, _P_LNG, _P_LNB) = range(12)
_P_ROWS = 16


def _rwkv_consts():
    r2, c2 = np.indices((LANES, LANES))
    same = (r2 < HEAD_DIM) == (c2 < HEAD_DIM)
    low = (r2 % HEAD_DIM) > (c2 % HEAD_DIM)
    masks = np.stack([low & same,
                      low & ~same,
                      (r2 % HEAD_DIM) >= (c2 % HEAD_DIM),
                      same, ~same, r2 == c2]).astype(np.float32)
    bf = np.stack([same.astype(np.float32), same.astype(np.float32) / HEAD_DIM])
    ones2 = np.kron(np.eye(2, dtype=np.float32), same.astype(np.float32))
    t, s = np.indices((CHUNK, CHUNK))
    return (jnp.asarray(masks), jnp.asarray(bf, dtype=BF16), jnp.asarray(ones2, dtype=BF16),
            jnp.asarray((t >= s).astype(np.float32), dtype=BF16))


def _rwkv_kernel(r_ref, k_ref, v_ref, l_ref, gr_ref, pr_ref, pk_ref, pv_ref, plr_ref,
                 par_ref, wl_ref, g2_ref, s0_ref, msk_ref, bfm_ref, ones2_ref, cum_ref, o_ref, s_ref,
                 h_scr, zr_scr, zk_scr, zv_scr, zl_scr, *, rows, group, pps):
    t = pl.program_id(2)

    @pl.when(t == 0)
    def _():
        h_scr[...] = s0_ref[...]
        zr_scr[...] = pr_ref[...]
        zk_scr[...] = pk_ref[...]
        zv_scr[...] = pv_ref[...]
        zl_scr[...] = plr_ref[...]

    cat = lambda parts, axis=0: jnp.concatenate(parts, axis=axis)
    pairs = range(pps)
    prow = lambda q, i: par_ref[q, i:i + 1, :]
    prow2 = lambda i: cat([prow(q, i) for q in pairs], 1)
    psl = [slice(q * LANES, (q + 1) * LANES) for q in pairs]
    row_id = lax.broadcasted_iota(jnp.int32, (group, 1), 0)
    lane = lax.broadcasted_iota(jnp.int32, (1, LANES), 1)
    lo_half = lane < HEAD_DIM
    m0 = lo_half.astype(F32)
    m1 = 1.0 - m0
    mask_p, mask_a, tri_i = (msk_ref[i].astype(BF16) for i in range(3))
    head_diag, head_anti, eye = (msk_ref[i] for i in range(3, 6))
    avg_bd = bfm_ref[1]
    ones2 = ones2_ref[...]
    cum = cum_ref[...]
    mu_l = cat([prow(0, _P_MU_WA), prow(0, _P_MU_G)], 1)
    cpg = group // CHUNK
    n_groups = rows // group
    csl = [slice(c * CHUNK, (c + 1) * CHUNK) for c in range(cpg)]
    zeros_q = jnp.zeros((CHUNK, LANES), BF16)

    def shifted(ref, scr, mu, g):
        z = ref[0, g * group:(g + 1) * group, :]
        prev_row = scr[SUBLANES - 1:SUBLANES, :] if g == 0 else ref[0, g * group - 1:g * group, :]
        prev = jnp.where(row_id == 0, prev_row, pltpu.roll(z, 1, 0))
        return z + (prev - z) * mu

    def prep(g, out):
        r2 = shifted(r_ref, zr_scr, prow2(_P_MU_R), g)
        k2_ = shifted(k_ref, zk_scr, prow2(_P_MU_K), g)
        v2 = shifted(v_ref, zv_scr, prow2(_P_MU_V), g)
        yield
        zl = shifted(l_ref, zl_scr, mu_l, g)
        wa, gl = zl[:, :LANES], zl[:, LANES:]
        lora_in = jnp.where(lane < DECAY_LORA, jnp.tanh(wa), wa).astype(BF16)
        gate_all = _dg((1.0 / (1.0 + jnp.exp(-gl))).astype(BF16), g2_ref[...])
        yield
        for q in pairs:
            r, k, v = r2[:, psl[q]], k2_[:, psl[q]], v2[:, psl[q]]
            lo = _dg(lora_in, wl_ref[q])
            gate = gate_all[:, psl[q]]
            kk = k * prow(q, _P_KK)
            yield
            wpre = -(prow(q, _P_W0) + lo[:, :LANES])
            w = -(jnp.maximum(wpre, 0.0) + jnp.log(1.0 + jnp.exp(-jnp.abs(wpre)))) - 0.5
            logw = -jnp.exp(w)
            hi, lw = _split(logw)
            hl = cat([hi, lw], 1)
            lcs = [_dg(cum, hl[s]) for s in csl]
            a = 1.0 / (1.0 + jnp.exp(-(prow(q, _P_A0) + lo[:, LANES:])))
            k2 = k * (1.0 + (a - 1.0) * prow(q, _P_KA))
            sums = _dg(cat([kk * kk, r * k2 * prow(q, _P_RK)], 1).astype(BF16), ones2)
            yield
            kk = kk / jnp.maximum(jnp.sqrt(sums[:, :LANES]), 1e-12)
            lc = cat([x[:, :LANES] + x[:, LANES:] for x in lcs])
            e_in, e_neg = jnp.exp(lc), jnp.exp(-lc)
            yield
            rt_ = r * e_in
            at_ = -kk * jnp.exp(lc - logw)
            v_sw = pltpu.roll(v, HEAD_DIM, 1)
            kt_, bt_ = k2 * e_neg, kk * a * e_neg
            gate = gate / (1.0 + jnp.exp(-gr_ref[0, g * group:(g + 1) * group, psl[q]]))
            out.append(dict(q=q, v=v, gate=gate, bonus=sums[:, LANES:] * v, e_in=e_in, rt=rt_, at=at_,
                            kt0=kt_ * m0, kt1=kt_ * m1, bt0=bt_ * m0, bt1=bt_ * m1,
                            vs0=(v_sw * m0).astype(BF16), vs1=(v_sw * m1).astype(BF16)))
            yield

    def phase1(pp, out):
        units = [(p_, s) for p_ in pp for s in csl]
        get = lambda n: [p_[n][s] for p_, s in units]
        at_, rt_, bt0, bt1, kt0, kt1, vs0, vs1 = (get(n) for n in ("at", "rt", "bt0", "bt1", "kt0", "kt1", "vs0", "vs1"))
        gcs = [p_["e_in"][s.stop - 1:s.stop, :] for p_, s in units]
        d_ab = [_mm1(cat([a_, r_]), cat([b0, k0, k1, b1]), _NT)
                for a_, r_, b0, k0, k1, b1 in zip(at_, rt_, bt0, kt0, kt1, bt1)]
        yield
        top = [cat([d[:CHUNK, :LANES], d[:CHUNK, LANES:]]).astype(BF16) for d in d_ab]
        bot = [cat([d[CHUNK:, :LANES], d[CHUNK:, LANES:]]).astype(BF16) * tri_i for d in d_ab]
        p = [tp * mask_p for tp in top]
        xb = [(cat([a_ * m0, a_ * m1]) + _dg(tp * mask_a, cat([w0, w1]))).astype(BF16)
              for a_, tp, w0, w1 in zip(at_, top, vs0, vs1)]
        yield
        for i in range(6):
            if i < 5:
                px = [_dg(pw, cat([x, pw], 1)) for x, pw in zip(xb, p)]
                p = [d[:, LANES:].astype(BF16) for d in px]
                xb = [x + d[:, :LANES].astype(BF16) for x, d in zip(xb, px)]
            else:
                xb = [x + _dg(pw, x).astype(BF16) for x, pw in zip(xb, p)]
            yield
        gst = [cat([x[:CHUNK], w1, w0, x[CHUNK:]]) for x, w0, w1 in zip(xb, vs0, vs1)]
        lhs45 = []
        for b_, b0, k0, k1, b1, gc in zip(bot, bt0, kt0, kt1, bt1, gcs):
            lhs45.append(cat([cat([b_[:CHUNK], zeros_q], 1), cat([zeros_q, b_[CHUNK:]], 1),
                              cat([b0 * gc, k0 * gc, k1 * gc, b1 * gc]).T.astype(BF16)]))
        o45 = [_dg(l_, g_) for l_, g_ in zip(lhs45, gst)]
        yield
        out["lhs"] = [cat([(r_ + jnp.where(lo_half, o[:CHUNK], o[CHUNK:LANES])).astype(BF16),
                           (o[LANES:] * head_diag + eye * gc).astype(BF16)])
                      for r_, o, gc in zip(rt_, o45, gcs)]
        out["y_in"] = [jnp.where(lo_half, o[CHUNK:LANES], o[:CHUNK]) for o in o45]
        out["g_c"] = [o[LANES:] * head_anti for o in o45]

    def chain_step(f, c, hs, ys):
        for q in pairs:
            u = q * cpg + c
            yh = _dg(f["lhs"][u], hs[q].astype(BF16))
            ys[q].append(yh[:CHUNK] + f["y_in"][u])
            hs[q] = yh[CHUNK:] + f["g_c"][u]

    def epilogue(g, pp, ys):
        gsl = slice(g * group, (g + 1) * group)
        for p_ in pp:
            q = p_["q"]
            y = pltpu.roll(cat(ys[q]), HEAD_DIM, 1)
            yc = y - _dg(y.astype(BF16), avg_bd)
            yv = _dg((yc * yc).astype(BF16), avg_bd)
            yn = yc * lax.rsqrt(yv + GN_EPS) * prow(q, _P_LNG) + prow(q, _P_LNB)
            o_ref[0, gsl, psl[q]] = (yn + p_["bonus"]) * p_["gate"]

    n_stages, n_pieces = 9, 2 + 4 * pps
    hs = [h_scr[q] for q in pairs]
    p_all = [[] for _ in range(n_groups)]
    f_all = [{} for _ in range(n_groups)]
    for _ in prep(0, p_all[0]):
        pass
    for g in range(n_groups):
        side = prep(g + 1, p_all[g + 1]) if g + 1 < n_groups else iter(())
        ys = [[] for _ in pairs]
        done = 0
        for i, _ in enumerate(phase1(p_all[g], f_all[g])):
            while done * n_stages < (i + 1) * n_pieces:
                next(side, None)
                done += 1
            if g > 0 and i % 2 == 1 and len(ys[0]) < cpg:
                chain_step(f_all[g - 1], len(ys[0]), hs, ys)
        for _ in side:
            pass
        if g > 0:
            while len(ys[0]) < cpg:
                chain_step(f_all[g - 1], len(ys[0]), hs, ys)
            epilogue(g - 1, p_all[g - 1], ys)
    ys = [[] for _ in pairs]
    for c in range(cpg):
        chain_step(f_all[n_groups - 1], c, hs, ys)
    epilogue(n_groups - 1, p_all[n_groups - 1], ys)
    for q in pairs:
        h_scr[q] = hs[q]
    zr_scr[...] = r_ref[0, rows - SUBLANES:rows, :]
    zk_scr[...] = k_ref[0, rows - SUBLANES:rows, :]
    zv_scr[...] = v_ref[0, rows - SUBLANES:rows, :]
    zl_scr[...] = l_ref[0, rows - SUBLANES:rows, :]

    @pl.when(t == pl.num_programs(2) - 1)
    def _():
        for q in pairs:
            s_ref[0, q] = hs[q]


def _rwkv(proj3, prev8, par, wl, g2, s0, *, rows, pps):
    bsz, t_len, _ = proj3.shape
    masks, bfm, ones2, cum = _rwkv_consts()
    wide = pps * LANES
    col = lambda off: (lambda b, p, t: (b, t, off // wide + p))
    pcol = lambda off: (lambda b, p, t: (0, off // wide + p))
    return pl.pallas_call(
        functools.partial(_rwkv_kernel, rows=rows, group=min(RWKV_GROUP, rows), pps=pps),
        out_shape=(jax.ShapeDtypeStruct((bsz, t_len, D_MODEL), F32),
                   jax.ShapeDtypeStruct((bsz, N_PAIRS, LANES, LANES), F32)),
        grid=(bsz, N_PAIRS // pps, t_len // rows),
        in_specs=[
            pl.BlockSpec((1, rows, wide), col(F_R)),
            pl.BlockSpec((1, rows, wide), col(F_K)),
            pl.BlockSpec((1, rows, wide), col(F_V)),
            pl.BlockSpec((1, rows, 2 * LANES), lambda b, p, t: (b, t, F_L // (2 * LANES))),
            pl.BlockSpec((1, rows, wide), col(F_GR)),
            pl.BlockSpec((SUBLANES, wide), pcol(F_R)),
            pl.BlockSpec((SUBLANES, wide), pcol(F_K)),
            pl.BlockSpec((SUBLANES, wide), pcol(F_V)),
            pl.BlockSpec((SUBLANES, 2 * LANES), lambda b, p, t: (0, F_L // (2 * LANES))),
            pl.BlockSpec((pps, _P_ROWS, LANES), lambda b, p, t: (p, 0, 0)),
            pl.BlockSpec((pps, LANES, 2 * LANES), lambda b, p, t: (p, 0, 0)),
            pl.BlockSpec((GATE_LORA, wide), lambda b, p, t: (0, p)),
            pl.BlockSpec((pps, LANES, LANES), lambda b, p, t: (p, 0, 0)),
            _const_spec(masks.shape), _const_spec(bfm.shape), _const_spec(ones2.shape), _const_spec(cum.shape),
        ],
        out_specs=(pl.BlockSpec((1, rows, wide), lambda b, p, t: (b, t, p)),
                   pl.BlockSpec((1, pps, LANES, LANES), lambda b, p, t: (b, p, 0, 0))),
        scratch_shapes=[pltpu.VMEM((pps, LANES, LANES), F32),
                        pltpu.VMEM((SUBLANES, wide), F32), pltpu.VMEM((SUBLANES, wide), F32),
                        pltpu.VMEM((SUBLANES, wide), F32), pltpu.VMEM((SUBLANES, 2 * LANES), F32)],
        compiler_params=pltpu.CompilerParams(
            dimension_semantics=("parallel", "parallel", "arbitrary"), vmem_limit_bytes=VMEM_LIMIT),
        name="rwkv",
    )(proj3, proj3, proj3, proj3, proj3, prev8, prev8, prev8, prev8, par, wl, g2, s0, masks, bfm, ones2, cum)


def _attn_kernel(sink_ref, q_ref, ga_ref, kvc_ref, kvp_ref, kvm_ref, bf_ref, bg_ref, rw_ref, o_ref):
    first = pl.program_id(1) == 0
    lane = lax.broadcasted_iota(jnp.int32, (1, LANES), 1)
    lo_lane = lane < HEAD_DIM
    lo_row = lax.broadcasted_iota(jnp.int32, (LANES, 1), 0) < HEAD_DIM
    kv_w = 2 * LANES
    kc, vc = kvc_ref[0, :, :kv_w], kvc_ref[0, :, kv_w:]
    kv_prev = jnp.where(first, kvm_ref[...], kvp_ref[0])
    k_prev, v_prev = kv_prev[:, :kv_w], kv_prev[:, kv_w:]
    ppk = N_PAIRS // N_KV

    def scores(sb, kh):
        rs = slice(sb * BLOCK, (sb + 1) * BLOCK)
        kp_, vp_ = (k_prev, v_prev) if sb == 0 else (kc[rs.start - BLOCK:rs.start], vc[rs.start - BLOCK:rs.start])
        kx = jnp.concatenate([kp_, kc[rs]], axis=0)[:, kh * LANES:(kh + 1) * LANES]
        vx = jnp.concatenate([vp_, vc[rs]], axis=0)[:, kh * LANES:(kh + 1) * LANES].astype(F32)
        lts = []
        for p in range(kh * ppk, (kh + 1) * ppk):
            qt = q_ref[0, rs, p * LANES:(p + 1) * LANES]
            zq = jnp.zeros_like(qt)
            q2 = jnp.concatenate([jnp.where(lo_lane, qt, zq), jnp.where(lo_lane, zq, qt)], axis=0)
            lts.append(_dg(kx, q2, _NT))
        vxt = jnp.concatenate([vx.T, jnp.ones((SUM_ROWS, 2 * BLOCK), F32)], axis=0).astype(BF16)
        return dict(sb=sb, kh=kh, rs=rs, lts=lts, vxt=vxt)

    def finish(g):
        bias_ref = bf_ref if g["sb"] == 0 else bg_ref
        pes, sinks = [], []
        for j, lt in enumerate(g["lts"]):
            for sub in range(2):
                hd = 2 * (g["kh"] * ppk + j) + sub
                lg = lt[:, sub * LANES:(sub + 1) * LANES] + bias_ref[0, hd]
                m = jnp.maximum(jnp.max(lg, axis=0, keepdims=True), sink_ref[hd])
                sinks.append(jnp.exp2(sink_ref[hd] - m))
                pes.append(jnp.exp2(lg - m).astype(BF16))
        ots = [_dg(g["vxt"], jnp.concatenate(pes[2 * j:2 * j + 2], axis=1)) for j in range(ppk)]
        for j, ot in enumerate(ots):
            cs = slice((g["kh"] * ppk + j) * LANES, (g["kh"] * ppk + j + 1) * LANES)
            inv0 = 1.0 / (ot[LANES:LANES + 1, :LANES] + sinks[2 * j])
            inv1 = 1.0 / (ot[LANES:LANES + 1, LANES:] + sinks[2 * j + 1])
            att = jnp.where(lo_row, ot[:LANES, :LANES] * inv0, ot[:LANES, LANES:] * inv1).T
            gate = 1.0 / (1.0 + jnp.exp(-ga_ref[0, g["rs"], cs]))
            o_ref[0, g["rs"], cs] = (gate * att + rw_ref[0, g["rs"], cs]).astype(BF16)

    order = [(sb, kh) for sb in range(ATT_ROWS // BLOCK) for kh in range(N_KV)]
    cur = scores(*order[0])
    for nxt in order[1:]:
        ahead = scores(*nxt)
        finish(cur)
        cur = ahead
    finish(cur)


def _attn_merge(proj16, proj32, kvm, bias, sinks, rwg):
    bsz, t_len, _ = proj16.shape
    kv_w = 4 * LANES
    assert B_VX == B_KX + kv_w // 2 and B_KX % kv_w == 0
    per = ATT_ROWS // BLOCK
    rows = lambda w, col: pl.BlockSpec((1, ATT_ROWS, w), lambda b, n: (b, n, col // w))
    return pl.pallas_call(
        _attn_kernel,
        out_shape=jax.ShapeDtypeStruct((bsz, t_len, D_MODEL), BF16),
        grid=(bsz, t_len // ATT_ROWS),
        in_specs=[
            pl.BlockSpec(memory_space=pltpu.SMEM),
            rows(D_MODEL, B_Q), rows(D_MODEL, F_GA), rows(kv_w, B_KX),
            pl.BlockSpec((1, BLOCK, kv_w), lambda b, n: (b, jnp.maximum(n * per - 1, 0), B_KX // kv_w)),
            pl.BlockSpec((BLOCK, kv_w), lambda b, n: (0, B_KX // kv_w)),
            pl.BlockSpec((1, N_HEADS, 2 * BLOCK, BLOCK), lambda b, n: (jnp.minimum(n, 1), 0, 0, 0)),
            pl.BlockSpec((1, N_HEADS, 2 * BLOCK, BLOCK), lambda b, n: (1, 0, 0, 0)),
            rows(D_MODEL, 0),
        ],
        out_specs=rows(D_MODEL, 0),
        compiler_params=pltpu.CompilerParams(
            dimension_semantics=("parallel", "arbitrary"), vmem_limit_bytes=VMEM_LIMIT),
        name="attn_merge",
    )(sinks, proj16, proj32, proj16, proj16, kvm, bias, bias, rwg)


def _out_ffn_kernel(x_ref, mg_ref, ln_ref, wo_ref, w1_ref, w2_ref, o_ref, *, ff_tile, parts):
    ln = ln_ref[...]
    tm = x_ref.shape[0]
    rs = [slice(i * tm // parts, (i + 1) * tm // parts) for i in range(parts)]
    proj = [_dg(mg_ref[r, :], wo_ref[...]) for r in rs]
    h1 = [_layer_norm(ALPHA * _layer_norm(x_ref[r, :], ln[0:1], ln[1:2]) + p, ln[2:3], ln[3:4]) for r, p in zip(rs, proj)]
    for r, h in zip(rs, h1):
        hb = h.astype(BF16)
        acc = ALPHA * h
        for j in range(D_FF // ff_tile):
            cs = slice(j * ff_tile, (j + 1) * ff_tile)
            u = jnp.maximum(_dg(hb, w1_ref[:, cs]), 0.0)
            acc = acc + _dg((u * u).astype(BF16), w2_ref[cs, :])
        o_ref[r, :] = _layer_norm(acc, ln[4:5], ln[5:6])


def _out_ffn(x2, merged2, ln_par, wo, w1, w2, *, tm, ff_tile=1024, parts=FFN_PARTS):
    m = x2.shape[0]
    return pl.pallas_call(
        functools.partial(_out_ffn_kernel, ff_tile=ff_tile, parts=parts),
        out_shape=jax.ShapeDtypeStruct((m, D_MODEL), F32),
        grid=(m // tm,),
        in_specs=[
            pl.BlockSpec((tm, D_MODEL), lambda i: (i, 0)),
            pl.BlockSpec((tm, D_MODEL), lambda i: (i, 0)),
            _const_spec((SUBLANES, D_MODEL)),
            _const_spec((D_MODEL, D_MODEL)),
            _const_spec((D_MODEL, D_FF)),
            _const_spec((D_FF, D_MODEL)),
        ],
        out_specs=pl.BlockSpec((tm, D_MODEL), lambda i: (i, 0)),
        compiler_params=pltpu.CompilerParams(
            dimension_semantics=("parallel",), vmem_limit_bytes=VMEM_LIMIT),
        name="out_ffn",
    )(x2, merged2, ln_par, wo, w1, w2)


def kernel(x, meta_tokens, ln0_g, ln0_b, rel_bias, w_in, shift_mu, attn_sinks, decay_w0, decay_w2, iclr_a0, iclr_a2, gate_w2, k_k, k_a, r_k, lnx_g, lnx_b, w_out, ln1_g, ln1_b, w_ff1, w_ff2, ln2_g, ln2_b):
    bsz, seq, _ = x.shape
    assert D_MODEL == x.shape[2] and seq % RWKV_ROWS == 0 and seq % ATT_ROWS == 0
    assert (bsz * seq) % PROJ_ROWS == 0 and (bsz * seq) % FFN_ROWS == 0
    W = D_MODEL
    wi = w_in[0]
    w_perm = wi.astype(BF16)
    assert w_perm.shape[1] == IN_COLS
    mu = shift_mu[0]
    rows_ = [mu[:W], mu[W:2 * W], mu[2 * W:3 * W]]
    vec = lambda a: a.reshape(N_PAIRS, 1, LANES)
    par = jnp.concatenate(
        [vec(rows_[0]), vec(rows_[1]), vec(rows_[2]),
         jnp.broadcast_to(mu[3 * W:3 * W + LANES].reshape(1, 1, LANES), (N_PAIRS, 1, LANES)),
         jnp.broadcast_to(mu[3 * W + LANES:].reshape(1, 1, LANES), (N_PAIRS, 1, LANES)),
         vec(decay_w0[0]), vec(iclr_a0[0]), vec(k_k[0]), vec(k_a[0]), vec(r_k[0].reshape(-1)),
         vec(lnx_g[0]), vec(lnx_b[0]),
         jnp.zeros((N_PAIRS, _P_ROWS - 12, LANES), F32)], axis=1).astype(F32)
    w2p = decay_w2[0].reshape(DECAY_LORA, N_PAIRS, LANES).transpose(1, 0, 2)
    a2p = iclr_a2[0].reshape(ICLR_LORA, N_PAIRS, LANES).transpose(1, 0, 2)
    z = jnp.zeros_like(w2p)
    wl = jnp.concatenate([jnp.concatenate([w2p, z], axis=2), jnp.concatenate([z, a2p], axis=2)], axis=1).astype(BF16)
    g2 = gate_w2[0].astype(BF16)
    ln_par = jnp.stack([ln0_g, ln0_b, ln1_g[0], ln1_b[0], ln2_g[0], ln2_b[0],
                        jnp.zeros_like(ln0_g), jnp.zeros_like(ln0_g)]).astype(F32)
    g0 = ln0_g.reshape(1, W).astype(F32)
    b0 = ln0_b.reshape(1, W).astype(F32)

    bias = _bias_table(rel_bias)

    meta_blk = jnp.concatenate([jnp.zeros((PAD_ROWS, W), F32), meta_tokens.astype(F32)], axis=0)
    proj_m16, proj_m32 = _ln_proj(meta_blk, g0, b0, w_perm, tm=BLOCK, n_pad=PAD_ROWS)
    zeros8 = jnp.zeros((SUBLANES, F_COLS), F32)
    s_zero = jnp.zeros((N_PAIRS, LANES, LANES), F32)
    _, s_meta = _rwkv(proj_m32[None], zeros8, par, wl, g2, s_zero, rows=BLOCK, pps=N_PAIRS)
    prev8 = proj_m32[BLOCK - SUBLANES:]

    x2 = x.reshape(bsz * seq, W)
    proj16, proj32 = _ln_proj(x2, g0, b0, w_perm, tm=PROJ_ROWS)
    proj32 = proj32.reshape(bsz, seq, F_COLS)
    rwg, _ = _rwkv(proj32, prev8, par, wl, g2, s_meta[0], rows=RWKV_ROWS, pps=PAIRS_PER_STEP)
    merged = _attn_merge(proj16.reshape(bsz, seq, B_COLS), proj32, proj_m16, bias,
                         attn_sinks[0].astype(F32) * LOG2E, rwg)
    out = _out_ffn(x2, merged.reshape(bsz * seq, W), ln_par, w_out[0].astype(BF16),
                   w_ff1[0].astype(BF16), w_ff2[0].astype(BF16), tm=FFN_ROWS)
    return out.reshape(bsz, seq, W)
```

```python
import functools
import math

import numpy as np
import jax
import jax.numpy as jnp
from jax import lax
from jax.experimental import pallas as pl
from jax.experimental.pallas import tpu as pltpu

F32 = jnp.float32
BF16 = jnp.bfloat16

D_MODEL = 1024
N_META = 16
HEAD_DIM = 64
N_HEADS = D_MODEL // HEAD_DIM
N_KV = 2
BLOCK = 128
N_BUCKETS = 32
MAX_EXACT = 16
MAX_DISTANCE = 128
DECAY_LORA = 64
ICLR_LORA = 64
GATE_LORA = 128
D_FF = 4 * D_MODEL
LN_EPS = 1e-5
GN_EPS = 1e-5 * HEAD_DIM
DEPTH = 1
ALPHA = (2.0 * DEPTH) ** 0.25
LOG2E = math.log2(math.e)

LANES = 128
SUBLANES = 8
N_PAIRS = D_MODEL // LANES
CHUNK = 64
RWKV_ROWS = 2048
RWKV_GROUP = 256
PAIRS_PER_STEP = 2
PROJ_ROWS = 512
FFN_ROWS = 1024
FFN_PARTS = 4
PAD_ROWS = BLOCK - N_META
ATT_ROWS = 8 * BLOCK
SUM_ROWS = 16

B_Q = 0
B_KX = 1024
B_VX = 1280
B_COLS = 1536
F_GA = 0
F_R = 1024
F_K = 2048
F_V = 3072
F_L = 4096
F_GR = 4352
F_COLS = 5376

IN_Q = 0
IN_K = 1024
IN_ZR = 1280
IN_GA = 4608
IN_GR = 5632
IN_COLS = 6656
_PROJ_SEGMENTS = ((IN_GA, D_MODEL, F_GA), (IN_ZR, F_GR - F_R, F_R), (IN_GR, D_MODEL, F_GR))
Q_SCALE = HEAD_DIM ** -0.5 * LOG2E

VMEM_LIMIT = 56 * 1024 * 1024

_NN = (((1,), (0,)), ((), ()))
_NT = (((1,), (1,)), ((), ()))


def _dg(a, b, dims=_NN):
    return lax.dot_general(a, b, dims, preferred_element_type=F32)


def _split(x):
    hi = x.astype(BF16)
    lo = (x - hi.astype(F32)).astype(BF16)
    return hi, lo


def _mm1(a, b, dims=_NN):
    return _dg(a.astype(BF16), b.astype(BF16), dims)


def _layer_norm(x, g, b):
    mu = jnp.mean(x, axis=-1, keepdims=True)
    xc = x - mu
    var = jnp.mean(xc * xc, axis=-1, keepdims=True)
    return xc * lax.rsqrt(var + LN_EPS) * g + b


def _const_spec(shape):
    return pl.BlockSpec(shape, lambda *_: (0,) * len(shape), pipeline_mode=pl.Buffered(1))


def _bias_kernel(bucket_ref, rel_ref, o_ref):
    first, bk = bucket_ref[0], bucket_ref[1]
    for h in range(N_HEADS):
        acc = jnp.where(bk < 0, -jnp.inf, 0.0).astype(F32)
        for b in range(N_BUCKETS):
            acc = jnp.where(bk == b, rel_ref[b, h] * LOG2E, acc)
        o_ref[1, h] = acc
        o_ref[0, h] = jnp.where(first < 0, -jnp.inf, acc)


def _bucket_table():
    q = np.arange(BLOCK)[:, None]
    s = np.arange(2 * BLOCK)[None, :]
    dist = q + BLOCK - s
    in_window = (dist >= 0) & (dist < BLOCK)
    d0 = np.maximum(dist, 0)
    d = np.maximum(d0, 1).astype(np.float32)
    large = MAX_EXACT + (np.log(d / np.float32(MAX_EXACT)) / np.float32(math.log(MAX_DISTANCE / MAX_EXACT))
                         * (N_BUCKETS - MAX_EXACT)).astype(np.int32)
    large = np.minimum(large, N_BUCKETS - 1)
    bucket = np.where(d0 < MAX_EXACT, d0, large).astype(np.int32)
    general = np.where(in_window, bucket, -1)
    first = np.where(in_window & (s >= PAD_ROWS), bucket, -1)
    return np.stack([first.T, general.T]).astype(np.int32)


def _bias_table(rel_bias):
    return pl.pallas_call(
        _bias_kernel,
        out_shape=jax.ShapeDtypeStruct((2, N_HEADS, 2 * BLOCK, BLOCK), F32),
        in_specs=[pl.BlockSpec(memory_space=pltpu.VMEM), pl.BlockSpec(memory_space=pltpu.SMEM)],
        out_specs=pl.BlockSpec(memory_space=pltpu.VMEM),
        name="bias_table",
    )(jnp.asarray(_bucket_table()), rel_bias.astype(F32))


def _ln_proj_kernel(x_ref, g_ref, b_ref, w_ref, o16_ref, o32_ref, *, n_pad, tn):
    tm = x_ref.shape[0]
    halves = [slice(0, tm // 2), slice(tm // 2, tm)]
    ybs = []
    for h_ in halves:
        y = _layer_norm(x_ref[h_, :], g_ref[...], b_ref[...])
        if n_pad:
            row = lax.broadcasted_iota(jnp.int32, y.shape, 0) + h_.start
            y = jnp.where(row < n_pad, 0.0, y)
        ybs.append(y.astype(BF16))
    lo_lane = lax.broadcasted_iota(jnp.int32, (1, LANES), 1) < HEAD_DIM

    def dup(z):
        zr = pltpu.roll(z, HEAD_DIM, 1)
        return jnp.concatenate([jnp.where(lo_lane, z, zr), jnp.where(lo_lane, zr, z)], axis=1)

    for j in range(0, D_MODEL, tn):
        for h_, yh in zip(halves, ybs):
            o16_ref[h_, B_Q + j:B_Q + j + tn] = (_dg(yh, w_ref[:, IN_Q + j:IN_Q + j + tn]) * Q_SCALE).astype(BF16)
    yb = jnp.concatenate(ybs, axis=0)
    kv = _dg(yb, w_ref[:, IN_K:IN_K + 2 * LANES])
    o16_ref[:, B_KX:B_KX + 2 * LANES] = dup(kv[:, :LANES]).astype(BF16)
    o16_ref[:, B_VX:B_VX + 2 * LANES] = dup(kv[:, LANES:]).astype(BF16)
    for src, width, dst in _PROJ_SEGMENTS:
        for j in range(0, width, tn):
            w_ = min(tn, width - j)
            o32_ref[:, dst + j:dst + j + w_] = _dg(yb, w_ref[:, src + j:src + j + w_])


def _ln_proj(x2, g, b, w, *, tm, tn=512, n_pad=0):
    m = x2.shape[0]
    return pl.pallas_call(
        functools.partial(_ln_proj_kernel, n_pad=n_pad, tn=tn),
        out_shape=(jax.ShapeDtypeStruct((m, B_COLS), BF16), jax.ShapeDtypeStruct((m, F_COLS), F32)),
        grid=(m // tm,),
        in_specs=[
            pl.BlockSpec((tm, D_MODEL), lambda i: (i, 0)),
            _const_spec((1, D_MODEL)),
            _const_spec((1, D_MODEL)),
            _const_spec((D_MODEL, IN_COLS)),
        ],
        out_specs=(pl.BlockSpec((tm, B_COLS), lambda i: (i, 0)), pl.BlockSpec((tm, F_COLS), lambda i: (i, 0))),
        compiler_params=pltpu.CompilerParams(
            dimension_semantics=("parallel",), vmem_limit_bytes=VMEM_LIMIT),
        name="ln_proj",
    )(x2, g, b, w)


(_P_MU_R, _P_MU_K, _P_MU_V, _P_MU_WA, _P_MU_G, _P_W0, _P_A0, _P_KK, _P_KA, _P_RK, _P_LNG, _P_LNB) = range(12)
_P_ROWS = 16


def _rwkv_consts():
    r2, c2 = np.indices((LANES, LANES))
    same = (r2 < HEAD_DIM) == (c2 < HEAD_DIM)
    low = (r2 % HEAD_DIM) > (c2 % HEAD_DIM)
    masks = np.stack([low & same,
                      low & ~same,
                      (r2 % HEAD_DIM) >= (c2 % HEAD_DIM),
                      same, ~same, r2 == c2]).astype(np.float32)
    bf = np.stack([same.astype(np.float32), same.astype(np.float32) / HEAD_DIM])
    ones2 = np.kron(np.eye(2, dtype=np.float32), same.astype(np.float32))
    t, s = np.indices((CHUNK, CHUNK))
    return (jnp.asarray(masks), jnp.asarray(bf, dtype=BF16), jnp.asarray(ones2, dtype=BF16),
            jnp.asarray((t >= s).astype(np.float32), dtype=BF16))


def _rwkv_kernel(r_ref, k_ref, v_ref, l_ref, gr_ref, pr_ref, pk_ref, pv_ref, plr_ref,
                 par_ref, wl_ref, g2_ref, s0_ref, msk_ref, bfm_ref, ones2_ref, cum_ref, o_ref, s_ref,
                 h_scr, zr_scr, zk_scr, zv_scr, zl_scr, *, rows, group, pps):
    t = pl.program_id(2)

    @pl.when(t == 0)
    def _():
        h_scr[...] = s0_ref[...]
        zr_scr[...] = pr_ref[...]
        zk_scr[...] = pk_ref[...]
        zv_scr[...] = pv_ref[...]
        zl_scr[...] = plr_ref[...]

    cat = lambda parts, axis=0: jnp.concatenate(parts, axis=axis)
    pairs = range(pps)
    prow = lambda q, i: par_ref[q, i:i + 1, :]
    prow2 = lambda i: cat([prow(q, i) for q in pairs], 1)
    psl = [slice(q * LANES, (q + 1) * LANES) for q in pairs]
    row_id = lax.broadcasted_iota(jnp.int32, (group, 1), 0)
    lane = lax.broadcasted_iota(jnp.int32, (1, LANES), 1)
    lo_half = lane < HEAD_DIM
    m0 = lo_half.astype(F32)
    m1 = 1.0 - m0
    mask_p, mask_a, tri_i = (msk_ref[i].astype(BF16) for i in range(3))
    head_diag, head_anti, eye = (msk_ref[i] for i in range(3, 6))
    avg_bd = bfm_ref[1]
    ones2 = ones2_ref[...]
    cum = cum_ref[...]
    mu_l = cat([prow(0, _P_MU_WA), prow(0, _P_MU_G)], 1)
    cpg = group // CHUNK
    n_groups = rows // group
    csl = [slice(c * CHUNK, (c + 1) * CHUNK) for c in range(cpg)]
    zeros_q = jnp.zeros((CHUNK, LANES), BF16)

    def shifted(ref, scr, mu, g):
        z = ref[0, g * group:(g + 1) * group, :]
        prev_row = scr[SUBLANES - 1:SUBLANES, :] if g == 0 else ref[0, g * group - 1:g * group, :]
        prev = jnp.where(row_id == 0, prev_row, pltpu.roll(z, 1, 0))
        return z + (prev - z) * mu

    def prep(g, out):
        r2 = shifted(r_ref, zr_scr, prow2(_P_MU_R), g)
        k2_ = shifted(k_ref, zk_scr, prow2(_P_MU_K), g)
        v2 = shifted(v_ref, zv_scr, prow2(_P_MU_V), g)
        yield
        zl = shifted(l_ref, zl_scr, mu_l, g)
        wa, gl = zl[:, :LANES], zl[:, LANES:]
        lora_in = jnp.where(lane < DECAY_LORA, jnp.tanh(wa), wa).astype(BF16)
        gate_all = _dg((1.0 / (1.0 + jnp.exp(-gl))).astype(BF16), g2_ref[...])
        yield
        for q in pairs:
            r, k, v = r2[:, psl[q]], k2_[:, psl[q]], v2[:, psl[q]]
            lo = _dg(lora_in, wl_ref[q])
            gate = gate_all[:, psl[q]]
            kk = k * prow(q, _P_KK)
            yield
            wpre = -(prow(q, _P_W0) + lo[:, :LANES])
            w = -(jnp.maximum(wpre, 0.0) + jnp.log(1.0 + jnp.exp(-jnp.abs(wpre)))) - 0.5
            logw = -jnp.exp(w)
            hi, lw = _split(logw)
            hl = cat([hi, lw], 1)
            lcs = [_dg(cum, hl[s]) for s in csl]
            a = 1.0 / (1.0 + jnp.exp(-(prow(q, _P_A0) + lo[:, LANES:])))
            k2 = k * (1.0 + (a - 1.0) * prow(q, _P_KA))
            sums = _dg(cat([kk * kk, r * k2 * prow(q, _P_RK)], 1).astype(BF16), ones2)
            yield
            kk = kk / jnp.maximum(jnp.sqrt(sums[:, :LANES]), 1e-12)
            lc = cat([x[:, :LANES] + x[:, LANES:] for x in lcs])
            e_in, e_neg = jnp.exp(lc), jnp.exp(-lc)
            yield
            rt_ = r * e_in
            at_ = -kk * jnp.exp(lc - logw)
            v_sw = pltpu.roll(v, HEAD_DIM, 1)
            kt_, bt_ = k2 * e_neg, kk * a * e_neg
            gate = gate / (1.0 + jnp.exp(-gr_ref[0, g * group:(g + 1) * group, psl[q]]))
            out.append(dict(q=q, v=v, gate=gate, bonus=sums[:, LANES:] * v, e_in=e_in, rt=rt_, at=at_,
                            kt0=kt_ * m0, kt1=kt_ * m1, bt0=bt_ * m0, bt1=bt_ * m1,
                            vs0=(v_sw * m0).astype(BF16), vs1=(v_sw * m1).astype(BF16)))
            yield

    def phase1(pp, out):
        units = [(p_, s) for p_ in pp for s in csl]
        get = lambda n: [p_[n][s] for p_, s in units]
        at_, rt_, bt0, bt1, kt0, kt1, vs0, vs1 = (get(n) for n in ("at", "rt", "bt0", "bt1", "kt0", "kt1", "vs0", "vs1"))
        gcs = [p_["e_in"][s.stop - 1:s.stop, :] for p_, s in units]
        d_ab = [_mm1(cat([a_, r_]), cat([b0, k0, k1, b1]), _NT)
                for a_, r_, b0, k0, k1, b1 in zip(at_, rt_, bt0, kt0, kt1, bt1)]
        yield
        top = [cat([d[:CHUNK, :LANES], d[:CHUNK, LANES:]]).astype(BF16) for d in d_ab]
        bot = [cat([d[CHUNK:, :LANES], d[CHUNK:, LANES:]]).astype(BF16) * tri_i for d in d_ab]
        p = [tp * mask_p for tp in top]
        xb = [(cat([a_ * m0, a_ * m1]) + _dg(tp * mask_a, cat([w0, w1]))).astype(BF16)
              for a_, tp, w0, w1 in zip(at_, top, vs0, vs1)]
        yield
        for i in range(6):
            if i < 5:
                px = [_dg(pw, cat([x, pw], 1)) for x, pw in zip(xb, p)]
                p = [d[:, LANES:].astype(BF16) for d in px]
                xb = [x + d[:, :LANES].astype(BF16) for x, d in zip(xb, px)]
            else:
                xb = [x + _dg(pw, x).astype(BF16) for x, pw in zip(xb, p)]
            yield
        gst = [cat([x[:CHUNK], w1, w0, x[CHUNK:]]) for x, w0, w1 in zip(xb, vs0, vs1)]
        lhs45 = []
        for b_, b0, k0, k1, b1, gc in zip(bot, bt0, kt0, kt1, bt1, gcs):
            lhs45.append(cat([cat([b_[:CHUNK], zeros_q], 1), cat([zeros_q, b_[CHUNK:]], 1),
                              cat([b0 * gc, k0 * gc, k1 * gc, b1 * gc]).T.astype(BF16)]))
        o45 = [_dg(l_, g_) for l_, g_ in zip(lhs45, gst)]
        yield
        out["lhs"] = [cat([(r_ + jnp.where(lo_half, o[:CHUNK], o[CHUNK:LANES])).astype(BF16),
                           (o[LANES:] * head_diag + eye * gc).astype(BF16)])
                      for r_, o, gc in zip(rt_, o45, gcs)]
        out["y_in"] = [jnp.where(lo_half, o[CHUNK:LANES], o[:CHUNK]) for o in o45]
        out["g_c"] = [o[LANES:] * head_anti for o in o45]

    def chain_step(f, c, hs, ys):
        for q in pairs:
            u = q * cpg + c
            yh = _dg(f["lhs"][u], hs[q].astype(BF16))
            ys[q].append(yh[:CHUNK] + f["y_in"][u])
            hs[q] = yh[CHUNK:] + f["g_c"][u]

    def epilogue(g, pp, ys):
        gsl = slice(g * group, (g + 1) * group)
        for p_ in pp:
            q = p_["q"]
            y = pltpu.roll(cat(ys[q]), HEAD_DIM, 1)
            yc = y - _dg(y.astype(BF16), avg_bd)
            yv = _dg((yc * yc).astype(BF16), avg_bd)
            yn = yc * lax.rsqrt(yv + GN_EPS) * prow(q, _P_LNG) + prow(q, _P_LNB)
            o_ref[0, gsl, psl[q]] = (yn + p_["bonus"]) * p_["gate"]

    n_stages, n_pieces = 9, 2 + 4 * pps
    hs = [h_scr[q] for q in pairs]
    p_all = [[] for _ in range(n_groups)]
    f_all = [{} for _ in range(n_groups)]
    for _ in prep(0, p_all[0]):
        pass
    for g in range(n_groups):
        side = prep(g + 1, p_all[g + 1]) if g + 1 < n_groups else iter(())
        ys = [[] for _ in pairs]
        done = 0
        for i, _ in enumerate(phase1(p_all[g], f_all[g])):
            while done * n_stages < (i + 1) * n_pieces:
                next(side, None)
                done += 1
            if g > 0 and i % 2 == 1 and len(ys[0]) < cpg:
                chain_step(f_all[g - 1], len(ys[0]), hs, ys)
        for _ in side:
            pass
        if g > 0:
            while len(ys[0]) < cpg:
                chain_step(f_all[g - 1], len(ys[0]), hs, ys)
            epilogue(g - 1, p_all[g - 1], ys)
    ys = [[] for _ in pairs]
    for c in range(cpg):
        chain_step(f_all[n_groups - 1], c, hs, ys)
    epilogue(n_groups - 1, p_all[n_groups - 1], ys)
    for q in pairs:
        h_scr[q] = hs[q]
    zr_scr[...] = r_ref[0, rows - SUBLANES:rows, :]
    zk_scr[...] = k_ref[0, rows - SUBLANES:rows, :]
    zv_scr[...] = v_ref[0, rows - SUBLANES:rows, :]
    zl_scr[...] = l_ref[0, rows - SUBLANES:rows, :]

    @pl.when(t == pl.num_programs(2) - 1)
    def _():
        for q in pairs:
            s_ref[0, q] = hs[q]


def _rwkv(proj3, prev8, par, wl, g2, s0, *, rows, pps):
    bsz, t_len, _ = proj3.shape
    masks, bfm, ones2, cum = _rwkv_consts()
    wide = pps * LANES
    col = lambda off: (lambda b, p, t: (b, t, off // wide + p))
    pcol = lambda off: (lambda b, p, t: (0, off // wide + p))
    return pl.pallas_call(
        functools.partial(_rwkv_kernel, rows=rows, group=min(RWKV_GROUP, rows), pps=pps),
        out_shape=(jax.ShapeDtypeStruct((bsz, t_len, D_MODEL), F32),
                   jax.ShapeDtypeStruct((bsz, N_PAIRS, LANES, LANES), F32)),
        grid=(bsz, N_PAIRS // pps, t_len // rows),
        in_specs=[
            pl.BlockSpec((1, rows, wide), col(F_R)),
            pl.BlockSpec((1, rows, wide), col(F_K)),
            pl.BlockSpec((1, rows, wide), col(F_V)),
            pl.BlockSpec((1, rows, 2 * LANES), lambda b, p, t: (b, t, F_L // (2 * LANES))),
            pl.BlockSpec((1, rows, wide), col(F_GR)),
            pl.BlockSpec((SUBLANES, wide), pcol(F_R)),
            pl.BlockSpec((SUBLANES, wide), pcol(F_K)),
            pl.BlockSpec((SUBLANES, wide), pcol(F_V)),
            pl.BlockSpec((SUBLANES, 2 * LANES), lambda b, p, t: (0, F_L // (2 * LANES))),
            pl.BlockSpec((pps, _P_ROWS, LANES), lambda b, p, t: (p, 0, 0)),
            pl.BlockSpec((pps, LANES, 2 * LANES), lambda b, p, t: (p, 0, 0)),
            pl.BlockSpec((GATE_LORA, wide), lambda b, p, t: (0, p)),
            pl.BlockSpec((pps, LANES, LANES), lambda b, p, t: (p, 0, 0)),
            _const_spec(masks.shape), _const_spec(bfm.shape), _const_spec(ones2.shape), _const_spec(cum.shape),
        ],
        out_specs=(pl.BlockSpec((1, rows, wide), lambda b, p, t: (b, t, p)),
                   pl.BlockSpec((1, pps, LANES, LANES), lambda b, p, t: (b, p, 0, 0))),
        scratch_shapes=[pltpu.VMEM((pps, LANES, LANES), F32),
                        pltpu.VMEM((SUBLANES, wide), F32), pltpu.VMEM((SUBLANES, wide), F32),
                        pltpu.VMEM((SUBLANES, wide), F32), pltpu.VMEM((SUBLANES, 2 * LANES), F32)],
        compiler_params=pltpu.CompilerParams(
            dimension_semantics=("parallel", "parallel", "arbitrary"), vmem_limit_bytes=VMEM_LIMIT),
        name="rwkv",
    )(proj3, proj3, proj3, proj3, proj3, prev8, prev8, prev8, prev8, par, wl, g2, s0, masks, bfm, ones2, cum)


def _attn_kernel(sink_ref, q_ref, ga_ref, kvc_ref, kvp_ref, kvm_ref, bf_ref, bg_ref, rw_ref, o_ref):
    first = pl.program_id(1) == 0
    lane = lax.broadcasted_iota(jnp.int32, (1, LANES), 1)
    lo_lane = lane < HEAD_DIM
    lo_row = lax.broadcasted_iota(jnp.int32, (LANES, 1), 0) < HEAD_DIM
    kv_w = 2 * LANES
    ppk = N_PAIRS // N_KV

    def window(sb, cols):
        if sb > 0:
            return kvc_ref[0, (sb - 1) * BLOCK:(sb + 1) * BLOCK, cols]
        prev = jnp.where(first, kvm_ref[:, cols], kvp_ref[0, :, cols])
        return jnp.concatenate([prev, kvc_ref[0, :BLOCK, cols]], axis=0)

    def scores(sb, kh):
        rs = slice(sb * BLOCK, (sb + 1) * BLOCK)
        kx = window(sb, slice(kh * LANES, (kh + 1) * LANES))
        vx = window(sb, slice(kv_w + kh * LANES, kv_w + (kh + 1) * LANES)).astype(F32)
        lts = []
        for p in range(kh * ppk, (kh + 1) * ppk):
            qt = q_ref[0, rs, p * LANES:(p + 1) * LANES]
            zq = jnp.zeros_like(qt)
            q2 = jnp.concatenate([jnp.where(lo_lane, qt, zq), jnp.where(lo_lane, zq, qt)], axis=0)
            lts.append(_dg(kx, q2, _NT))
        vxt = jnp.concatenate([vx.T, jnp.ones((SUM_ROWS, 2 * BLOCK), F32)], axis=0).astype(BF16)
        return dict(sb=sb, kh=kh, rs=rs, lts=lts, vxt=vxt)

    def finish(g):
        bias_ref = bf_ref if g["sb"] == 0 else bg_ref
        pes, sinks = [], []
        for j, lt in enumerate(g["lts"]):
            for sub in range(2):
                hd = 2 * (g["kh"] * ppk + j) + sub
                lg = lt[:, sub * LANES:(sub + 1) * LANES] + bias_ref[0, hd]
                m = jnp.maximum(jnp.max(lg, axis=0, keepdims=True), sink_ref[hd])
                sinks.append(jnp.exp2(sink_ref[hd] - m))
                pes.append(jnp.exp2(lg - m).astype(BF16))
        ots = [_dg(g["vxt"], jnp.concatenate(pes[2 * j:2 * j + 2], axis=1)) for j in range(ppk)]
        for j, ot in enumerate(ots):
            cs = slice((g["kh"] * ppk + j) * LANES, (g["kh"] * ppk + j + 1) * LANES)
            inv0 = 1.0 / (ot[LANES:LANES + 1, :LANES] + sinks[2 * j])
            inv1 = 1.0 / (ot[LANES:LANES + 1, LANES:] + sinks[2 * j + 1])
            att = jnp.where(lo_row, ot[:LANES, :LANES] * inv0, ot[:LANES, LANES:] * inv1).T
            gate = 1.0 / (1.0 + jnp.exp(-ga_ref[0, g["rs"], cs]))
            o_ref[0, g["rs"], cs] = (gate * att + rw_ref[0, g["rs"], cs]).astype(BF16)

    order = [(sb, kh) for sb in range(ATT_ROWS // BLOCK) for kh in range(N_KV)]
    cur = scores(*order[0])
    for nxt in order[1:]:
        ahead = scores(*nxt)
        finish(cur)
        cur = ahead
    finish(cur)


def _attn_merge(proj16, proj32, kvm, bias, sinks, rwg):
    bsz, t_len, _ = proj16.shape
    kv_w = 4 * LANES
    assert B_VX == B_KX + kv_w // 2 and B_KX % kv_w == 0
    per = ATT_ROWS // BLOCK
    rows = lambda w, col: pl.BlockSpec((1, ATT_ROWS, w), lambda b, n: (b, n, col // w))
    return pl.pallas_call(
        _attn_kernel,
        out_shape=jax.ShapeDtypeStruct((bsz, t_len, D_MODEL), BF16),
        grid=(bsz, t_len // ATT_ROWS),
        in_specs=[
            pl.BlockSpec(memory_space=pltpu.SMEM),
            rows(D_MODEL, B_Q), rows(D_MODEL, F_GA), rows(kv_w, B_KX),
            pl.BlockSpec((1, BLOCK, kv_w), lambda b, n: (b, jnp.maximum(n * per - 1, 0), B_KX // kv_w)),
            pl.BlockSpec((BLOCK, kv_w), lambda b, n: (0, B_KX // kv_w)),
            pl.BlockSpec((1, N_HEADS, 2 * BLOCK, BLOCK), lambda b, n: (jnp.minimum(n, 1), 0, 0, 0)),
            pl.BlockSpec((1, N_HEADS, 2 * BLOCK, BLOCK), lambda b, n: (1, 0, 0, 0)),
            rows(D_MODEL, 0),
        ],
        out_specs=rows(D_MODEL, 0),
        compiler_params=pltpu.CompilerParams(
            dimension_semantics=("parallel", "arbitrary"), vmem_limit_bytes=VMEM_LIMIT),
        name="attn_merge",
    )(sinks, proj16, proj32, proj16, proj16, kvm, bias, bias, rwg)


def _out_ffn_kernel(x_ref, mg_ref, ln_ref, wo_ref, w1_ref, w2_ref, o_ref, *, ff_tile, parts):
    ln = ln_ref[...]
    tm = x_ref.shape[0]
    rs = [slice(i * tm // parts, (i + 1) * tm // parts) for i in range(parts)]
    proj = [_dg(mg_ref[r, :], wo_ref[...]) for r in rs]
    h1 = [_layer_norm(ALPHA * _layer_norm(x_ref[r, :], ln[0:1], ln[1:2]) + p, ln[2:3], ln[3:4]) for r, p in zip(rs, proj)]
    for r, h in zip(rs, h1):
        hb = h.astype(BF16)
        acc = ALPHA * h
        for j in range(D_FF // ff_tile):
            cs = slice(j * ff_tile, (j + 1) * ff_tile)
            u = jnp.maximum(_dg(hb, w1_ref[:, cs]), 0.0)
            acc = acc + _dg((u * u).astype(BF16), w2_ref[cs, :])
        o_ref[r, :] = _layer_norm(acc, ln[4:5], ln[5:6])


def _out_ffn(x2, merged2, ln_par, wo, w1, w2, *, tm, ff_tile=1024, parts=FFN_PARTS):
    m = x2.shape[0]
    return pl.pallas_call(
        functools.partial(_out_ffn_kernel, ff_tile=ff_tile, parts=parts),
        out_shape=jax.ShapeDtypeStruct((m, D_MODEL), F32),
        grid=(m // tm,),
        in_specs=[
            pl.BlockSpec((tm, D_MODEL), lambda i: (i, 0)),
            pl.BlockSpec((tm, D_MODEL), lambda i: (i, 0)),
            _const_spec((SUBLANES, D_MODEL)),
            _const_spec((D_MODEL, D_MODEL)),
            _const_spec((D_MODEL, D_FF)),
            _const_spec((D_FF, D_MODEL)),
        ],
        out_specs=pl.BlockSpec((tm, D_MODEL), lambda i: (i, 0)),
        compiler_params=pltpu.CompilerParams(
            dimension_semantics=("parallel",), vmem_limit_bytes=VMEM_LIMIT),
        name="out_ffn",
    )(x2, merged2, ln_par, wo, w1, w2)


def kernel(x, meta_tokens, ln0_g, ln0_b, rel_bias, w_in, shift_mu, attn_sinks, decay_w0, decay_w2, iclr_a0, iclr_a2, gate_w2, k_k, k_a, r_k, lnx_g, lnx_b, w_out, ln1_g, ln1_b, w_ff1, w_ff2, ln2_g, ln2_b):
    bsz, seq, _ = x.shape
    assert D_MODEL == x.shape[2] and seq % RWKV_ROWS == 0 and seq % ATT_ROWS == 0
    assert (bsz * seq) % PROJ_ROWS == 0 and (bsz * seq) % FFN_ROWS == 0
    W = D_MODEL
    wi = w_in[0]
    w_perm = wi.astype(BF16)
    assert w_perm.shape[1] == IN_COLS
    mu = shift_mu[0]
    rows_ = [mu[:W], mu[W:2 * W], mu[2 * W:3 * W]]
    vec = lambda a: a.reshape(N_PAIRS, 1, LANES)
    par = jnp.concatenate(
        [vec(rows_[0]), vec(rows_[1]), vec(rows_[2]),
         jnp.broadcast_to(mu[3 * W:3 * W + LANES].reshape(1, 1, LANES), (N_PAIRS, 1, LANES)),
         jnp.broadcast_to(mu[3 * W + LANES:].reshape(1, 1, LANES), (N_PAIRS, 1, LANES)),
         vec(decay_w0[0]), vec(iclr_a0[0]), vec(k_k[0]), vec(k_a[0]), vec(r_k[0].reshape(-1)),
         vec(lnx_g[0]), vec(lnx_b[0]),
         jnp.zeros((N_PAIRS, _P_ROWS - 12, LANES), F32)], axis=1).astype(F32)
    w2p = decay_w2[0].reshape(DECAY_LORA, N_PAIRS, LANES).transpose(1, 0, 2)
    a2p = iclr_a2[0].reshape(ICLR_LORA, N_PAIRS, LANES).transpose(1, 0, 2)
    z = jnp.zeros_like(w2p)
    wl = jnp.concatenate([jnp.concatenate([w2p, z], axis=2), jnp.concatenate([z, a2p], axis=2)], axis=1).astype(BF16)
    g2 = gate_w2[0].astype(BF16)
    ln_par = jnp.stack([ln0_g, ln0_b, ln1_g[0], ln1_b[0], ln2_g[0], ln2_b[0],
                        jnp.zeros_like(ln0_g), jnp.zeros_like(ln0_g)]).astype(F32)
    g0 = ln0_g.reshape(1, W).astype(F32)
    b0 = ln0_b.reshape(1, W).astype(F32)

    bias = _bias_table(rel_bias)

    meta_blk = jnp.concatenate([jnp.zeros((PAD_ROWS, W), F32), meta_tokens.astype(F32)], axis=0)
    proj_m16, proj_m32 = _ln_proj(meta_blk, g0, b0, w_perm, tm=BLOCK, n_pad=PAD_ROWS)
    zeros8 = jnp.zeros((SUBLANES, F_COLS), F32)
    s_zero = jnp.zeros((N_PAIRS, LANES, LANES), F32)
    _, s_meta = _rwkv(proj_m32[None], zeros8, par, wl, g2, s_zero, rows=BLOCK, pps=N_PAIRS)
    prev8 = proj_m32[BLOCK - SUBLANES:]

    x2 = x.reshape(bsz * seq, W)
    proj16, proj32 = _ln_proj(x2, g0, b0, w_perm, tm=PROJ_ROWS)
    proj32 = proj32.reshape(bsz, seq, F_COLS)
    rwg, _ = _rwkv(proj32, prev8, par, wl, g2, s_meta[0], rows=RWKV_ROWS, pps=PAIRS_PER_STEP)
    merged = _attn_merge(proj16.reshape(bsz, seq, B_COLS), proj32, proj_m16, bias,
                         attn_sinks[0].astype(F32) * LOG2E, rwg)
    out = _out_ffn(x2, merged.reshape(bsz * seq, W), ln_par, w_out[0].astype(BF16),
                   w_ff1[0].astype(BF16), w_ff2[0].astype(BF16), tm=FFN_ROWS)
    return out.reshape(bsz, seq, W)
```
